```python
import jax, jax.numpy as jnp
from jax import lax
import numpy as np

D_MODEL = 1024
BATCH = 8
SEQ = 4096
DEPTH = 1

GLA_HEADS = 4
GLA_DK = D_MODEL // 2 // GLA_HEADS
GLA_DV = D_MODEL // GLA_HEADS
GLA_QK_W = GLA_HEADS * GLA_DK
GLA_V_W = GLA_HEADS * GLA_DV
GLA_GATE_RANK = 16
GLA_GATE_NORMALIZER = 16.0
GLA_LOG_GATE_MIN = -0.5
GLA_CHUNK = 64
SWA_HEADS = 8
SWA_KV_HEADS = 2
SWA_HEAD_DIM = 128
SWA_Q_W = SWA_HEADS * SWA_HEAD_DIM
SWA_KV_W = SWA_KV_HEADS * SWA_HEAD_DIM
SWA_WINDOW = 128
SWA_BLOCK = 128
ROPE_THETA = 10000.0
D_FF = -(-8 * D_MODEL // (3 * 256)) * 256
NORM_EPS = 1e-6
IN_SPLITS = (GLA_QK_W, GLA_QK_W, GLA_V_W, GLA_V_W, GLA_GATE_RANK, GLA_GATE_RANK,
             SWA_Q_W, SWA_KV_W, SWA_KV_W, D_MODEL, D_MODEL)
D_IN = sum(IN_SPLITS)

kernel_name = "hybrid_gla_swa_gated_encoder_block"


def _rms_norm(x, g):
    xf = x.astype(jnp.float32)
    y = xf * lax.rsqrt(jnp.mean(xf * xf, axis=-1, keepdims=True) + NORM_EPS)
    return (y * g.astype(jnp.float32)).astype(x.dtype)


def _rope(x, pos):
    half = x.shape[-1] // 2
    inv_freq = ROPE_THETA ** (-jnp.arange(half, dtype=jnp.float32) / half)
    ang = pos.astype(jnp.float32)[:, None] * inv_freq[None, :]
    cos = jnp.cos(ang)[None, :, None, :]
    sin = jnp.sin(ang)[None, :, None, :]
    x1, x2 = x[..., :half], x[..., half:]
    return jnp.concatenate([x1 * cos - x2 * sin, x2 * cos + x1 * sin], axis=-1)


def _gla_direction(q, k, v, log_g, include_diag):
    bsz, nh, s, dk = q.shape
    dv = v.shape[-1]
    c = GLA_CHUNK
    n = s // c
    q = q.reshape(bsz, nh, n, c, dk)
    k = k.reshape(bsz, nh, n, c, dk)
    log_g = log_g.reshape(bsz, nh, n, c, dk)
    v = v.reshape(bsz, nh, n, c, dv)
    b = jnp.cumsum(log_g, axis=3)
    b_last = b[:, :, :, -1:, :]
    q_dec = q * jnp.exp(b)
    k_inv = k * jnp.exp(-b)
    k_tail = k * jnp.exp(b_last - b)
    mask = jnp.tril(jnp.ones((c, c), dtype=bool), k=0 if include_diag else -1)
    scores = jnp.where(mask, jnp.einsum("bhncd,bhnmd->bhncm", q_dec, k_inv), 0.0)
    o_intra = jnp.einsum("bhncm,bhnme->bhnce", scores, v)
    chunk_kv = jnp.einsum("bhnmd,bhnme->nbhde", k_tail, v)
    chunk_decay = jnp.exp(jnp.moveaxis(b_last[:, :, :, 0, :], 2, 0))

    def step(state, inp):
        kv_n, dec_n = inp
        return dec_n[..., None] * state + kv_n, state

    _, states = lax.scan(step, jnp.zeros((bsz, nh, dk, dv), q.dtype), (chunk_kv, chunk_decay))
    o_inter = jnp.einsum("bhncd,nbhde->bhnce", q_dec, states)
    return (o_intra + o_inter).reshape(bsz, nh, s, dv)


def _gla_branch(q, k, v, r, lr_f, lr_b, up_f, bias_f, up_b, bias_b, out_g):
    bsz, s, _ = q.shape
    f32 = jnp.float32

    def heads(t, d):
        return t.astype(f32).reshape(bsz, s, GLA_HEADS, d).transpose(0, 2, 1, 3)

    def log_gate(lr, up, bias):
        z = lr.astype(f32) @ up.astype(f32) + bias.astype(f32)
        lg = jnp.maximum(jax.nn.log_sigmoid(z) / GLA_GATE_NORMALIZER, GLA_LOG_GATE_MIN)
        return heads(lg, GLA_DK)

    qh = heads(q, GLA_DK) * (GLA_DK ** -0.5)
    kh = heads(k, GLA_DK)
    vh = heads(v, GLA_DV)
    o_fwd = _gla_direction(qh, kh, vh, log_gate(lr_f, up_f, bias_f), True)
    flip = lambda t: jnp.flip(t, axis=2)
    o_bwd = flip(_gla_direction(flip(qh), flip(kh), flip(vh),
                                flip(log_gate(lr_b, up_b, bias_b)), False))
    o = _rms_norm(o_fwd + o_bwd, out_g)
    o = o.transpose(0, 2, 1, 3).reshape(bsz, s, GLA_V_W)
    return (o * jax.nn.silu(r.astype(f32))).astype(q.dtype)


def _swa_branch(q, k, v, q_g, k_g, sinks):
    bsz, s, _ = q.shape
    f32 = jnp.float32
    hd, hkv, blk = SWA_HEAD_DIM, SWA_KV_HEADS, SWA_BLOCK
    grp = SWA_HEADS // SWA_KV_HEADS
    n = s // blk
    pos = jnp.arange(s)
    q = _rope(_rms_norm(q.astype(f32).reshape(bsz, s, SWA_HEADS, hd), q_g), pos)
    k = _rope(_rms_norm(k.astype(f32).reshape(bsz, s, hkv, hd), k_g), pos)
    v = v.astype(f32).reshape(bsz, s, hkv, hd)

    def context(t):
        tp = jnp.pad(t, ((0, 0), (blk, blk), (0, 0), (0, 0)))
        return jnp.concatenate(
            [tp[:, o * blk:o * blk + s].reshape(bsz, n, blk, hkv, hd) for o in range(3)], axis=2)

    kc, vc = context(k), context(v)
    qb = q.reshape(bsz, n, blk, hkv, grp, hd)
    scores = jnp.einsum("bnqhgd,bnkhd->bnhgqk", qb, kc) * (hd ** -0.5)
    q_off = jnp.arange(blk)
    k_off = jnp.arange(3 * blk) - blk
    in_window = jnp.abs(k_off[None, :] - q_off[:, None]) <= SWA_WINDOW
    k_abs = jnp.arange(n)[:, None] * blk + k_off[None, :]
    in_seq = (k_abs >= 0) & (k_abs < s)
    mask = in_window[None, :, :] & in_seq[:, None, :]
    scores = jnp.where(mask[None, :, None, None], scores, -jnp.inf)
    sink = sinks.astype(f32).reshape(1, 1, hkv, grp, 1, 1)
    m = jnp.maximum(scores.max(axis=-1, keepdims=True), sink)
    p = jnp.exp(scores - m)
    denom = p.sum(axis=-1, keepdims=True) + jnp.exp(sink - m)
    o = jnp.einsum("bnhgqk,bnkhd->bnqhgd", p / denom, vc)
    return o.reshape(bsz, s, SWA_Q_W)


def setup_inputs(seed: int = 0) -> dict:
    key = jax.random.key(seed)
    ks = jax.random.split(key, 20)
    f32 = jnp.float32
    nrm = lambda k, shape, scale: jax.random.normal(k, shape, f32) * scale
    gain = lambda k, d: 1.0 + 0.02 * jax.random.normal(k, (DEPTH, d), f32)
    return {
        "x": nrm(ks[0], (BATCH, SEQ, D_MODEL), 1.0),
        "norm_mix_g": gain(ks[1], D_MODEL),
        "w_in": nrm(ks[2], (DEPTH, D_MODEL, D_IN), D_MODEL ** -0.5),
        "gla_gate_up_fwd": nrm(ks[3], (DEPTH, GLA_GATE_RANK, GLA_QK_W), GLA_GATE_RANK ** -0.5),
        "gla_gate_bias_fwd": nrm(ks[4], (DEPTH, GLA_QK_W), 0.02),
        "gla_gate_up_bwd": nrm(ks[5], (DEPTH, GLA_GATE_RANK, GLA_QK_W), GLA_GATE_RANK ** -0.5),
        "gla_gate_bias_bwd": nrm(ks[6], (DEPTH, GLA_QK_W), 0.02),
        "gla_out_norm_g": gain(ks[7], GLA_DV),
        "w_o_gla": nrm(ks[8], (DEPTH, GLA_V_W, D_MODEL), GLA_V_W ** -0.5),
        "swa_q_norm_g": gain(ks[9], SWA_HEAD_DIM),
        "swa_k_norm_g": gain(ks[10], SWA_HEAD_DIM),
        "swa_sinks": nrm(ks[11], (DEPTH, SWA_HEADS), 0.5),
        "w_o_swa": nrm(ks[12], (DEPTH, SWA_Q_W, D_MODEL), SWA_Q_W ** -0.5),
        "w_out": nrm(ks[13], (DEPTH, D_MODEL, D_MODEL), D_MODEL ** -0.5),
        "norm_ffn_g": gain(ks[14], D_MODEL),
        "w_ffn_in": nrm(ks[15], (DEPTH, D_MODEL, 2 * D_FF), D_MODEL ** -0.5),
        "w_ffn_out": nrm(ks[16], (DEPTH, D_FF, D_MODEL), D_FF ** -0.5),
    }


def reference(x, norm_mix_g, w_in, gla_gate_up_fwd, gla_gate_bias_fwd, gla_gate_up_bwd,
              gla_gate_bias_bwd, gla_out_norm_g, w_o_gla, swa_q_norm_g, swa_k_norm_g,
              swa_sinks, w_o_swa, w_out, norm_ffn_g, w_ffn_in, w_ffn_out):
    split_at = tuple(int(i) for i in np.cumsum(IN_SPLITS)[:-1])
    for l in range(DEPTH):
        h = _rms_norm(x, norm_mix_g[l])
        proj = h @ w_in[l]
        (g_q, g_k, g_v, g_r, g_lr_f, g_lr_b,
         s_q, s_k, s_v, gate_a, gate_b) = jnp.split(proj, split_at, axis=-1)
        y_gla = _gla_branch(g_q, g_k, g_v, g_r, g_lr_f, g_lr_b,
                            gla_gate_up_fwd[l], gla_gate_bias_fwd[l],
                            gla_gate_up_bwd[l], gla_gate_bias_bwd[l],
                            gla_out_norm_g[l]) @ w_o_gla[l]
        y_swa = _swa_branch(s_q, s_k, s_v, swa_q_norm_g[l], swa_k_norm_g[l],
                            swa_sinks[l]).astype(x.dtype) @ w_o_swa[l]
        merged = jax.nn.sigmoid(gate_a) * y_gla + jax.nn.sigmoid(gate_b) * y_swa
        x = x + merged @ w_out[l]
        h2 = _rms_norm(x, norm_ffn_g[l])
        gu = h2 @ w_ffn_in[l]
        ff_gate, ff_up = gu[..., :D_FF], gu[..., D_FF:]
        x = x + (jax.nn.silu(ff_gate) * ff_up) @ w_ffn_out[l]
    return x
```

```python
import functools
import math

import jax
import jax.numpy as jnp
import numpy as np
from jax import lax
from jax.experimental import pallas as pl
from jax.experimental.pallas import tpu as pltpu

F32 = jnp.float32
BF16 = jnp.bfloat16

D_MODEL = 1024
NORM_EPS = 1e-6
GLA_HEADS = 4
GLA_DK = 128
GLA_DV = 256
GLA_QK_W = GLA_HEADS * GLA_DK
GLA_V_W = GLA_HEADS * GLA_DV
GLA_GATE_RANK = 16
GLA_GATE_NORMALIZER = 16.0
GLA_LOG_GATE_MIN = -0.5
GLA_CHUNK = 128
SWA_HEADS = 8
SWA_KV_HEADS = 2
SWA_GROUP = SWA_HEADS // SWA_KV_HEADS
SWA_HEAD_DIM = 128
SWA_Q_W = SWA_HEADS * SWA_HEAD_DIM
SWA_KV_W = SWA_KV_HEADS * SWA_HEAD_DIM
SWA_BLOCK = 128
ROPE_THETA = 10000.0
D_FF = 2816
IN_SPLITS = (GLA_QK_W, GLA_QK_W, GLA_V_W, GLA_V_W, GLA_GATE_RANK, GLA_GATE_RANK,
             SWA_Q_W, SWA_KV_W, SWA_KV_W, D_MODEL, D_MODEL)

LANES = 128
MASK_VALUE = -1e30
VMEM_LIMIT = 56 * 1024 * 1024

IN_TILE_M = 512
SWA_TILE_Q = 512
OUT_TILE_M = 512
FFN_CHUNK = D_FF // 2


def _const_spec(shape):
    zeros = (0,) * len(shape)
    return pl.BlockSpec(shape, lambda *_: zeros, pipeline_mode=pl.Buffered(1))


def _dot(a, b):
    return jnp.dot(a, b, preferred_element_type=F32)


def _dot_nt(a, b):
    return lax.dot_general(a, b, (((1,), (1,)), ((), ())), preferred_element_type=F32)


def _dot_tn(a, b):
    return lax.dot_general(a, b, (((0,), (0,)), ((), ())), preferred_element_type=F32)


def _rms(x, gain):
    ms = jnp.mean(x * x, axis=-1, keepdims=True)
    return x * lax.rsqrt(ms + NORM_EPS) * gain


def _inproj_kernel(x_ref, g_ref, cos_ref, sin_ref, qg_ref, kg_ref,
                   wq_ref, wk_ref, wv_ref, wr_ref, wlr_ref, wsq_ref, wsk_ref, wsv_ref,
                   wga_ref, wgb_ref,
                   q_ref, k_ref, v_ref, r_ref, lr_ref, sq_ref, sk_ref, sv_ref, ga_ref, gb_ref):
    h = _rms(x_ref[...], g_ref[...]).astype(BF16)
    q_ref[...] = (_dot(h, wq_ref[...]) * (GLA_DK ** -0.5)).astype(BF16)
    k_ref[...] = _dot(h, wk_ref[...]).astype(BF16)
    v_ref[...] = _dot(h, wv_ref[...]).astype(BF16)
    r_ref[...] = _dot(h, wr_ref[...]).astype(BF16)
    lr_ref[...] = _dot(h, wlr_ref[...])
    sv_ref[...] = _dot(h, wsv_ref[...]).astype(BF16)
    ga_ref[...] = jax.nn.sigmoid(_dot(h, wga_ref[...])).astype(BF16)
    gb_ref[...] = jax.nn.sigmoid(_dot(h, wgb_ref[...])).astype(BF16)

    cos = cos_ref[...]
    sin = sin_ref[...]

    def norm_rope(t, gain, scale):
        y = _rms(t, gain)
        rot = pltpu.roll(y, SWA_HEAD_DIM // 2, axis=1)
        return ((y * cos + rot * sin) * scale).astype(BF16)

    sq = _dot(h, wsq_ref[...])
    for hh in range(SWA_HEADS):
        cols = slice(hh * SWA_HEAD_DIM, (hh + 1) * SWA_HEAD_DIM)
        sq_ref[:, cols] = norm_rope(sq[:, cols], qg_ref[...], SWA_HEAD_DIM ** -0.5)
    sk = _dot(h, wsk_ref[...])
    for hh in range(SWA_KV_HEADS):
        cols = slice(hh * SWA_HEAD_DIM, (hh + 1) * SWA_HEAD_DIM)
        sk_ref[:, cols] = norm_rope(sk[:, cols], kg_ref[...], 1.0)


def _in_projection(x2, norm_g, cos, sin, q_g, k_g, weights, seq):
    m = x2.shape[0]
    tm = IN_TILE_M
    pos_blocks = seq // tm
    row = lambda w: pl.BlockSpec((tm, w), lambda i: (i, 0))
    pos = pl.BlockSpec((tm, SWA_HEAD_DIM), lambda i: (i % pos_blocks, 0))
    widths = (GLA_QK_W, GLA_QK_W, GLA_V_W, GLA_V_W, LANES, SWA_Q_W, SWA_KV_W, SWA_KV_W,
              D_MODEL, D_MODEL)
    dtypes = (BF16, BF16, BF16, BF16, F32, BF16, BF16, BF16, BF16, BF16)
    return pl.pallas_call(
        _inproj_kernel,
        grid=(m // tm,),
        in_specs=[row(D_MODEL), _const_spec((1, D_MODEL)), pos, pos,
                  _const_spec((1, SWA_HEAD_DIM)), _const_spec((1, SWA_HEAD_DIM))]
                 + [_const_spec(w.shape) for w in weights],
        out_specs=[row(w) for w in widths],
        out_shape=[jax.ShapeDtypeStruct((m, w), d) for w, d in zip(widths, dtypes)],
        compiler_params=pltpu.CompilerParams(
            dimension_semantics=("parallel",), vmem_limit_bytes=VMEM_LIMIT),
        name="in_projection",
    )(x2, norm_g, cos, sin, q_g, k_g, *weights)


def _split3(x):
    hi = x.astype(BF16)
    r1 = x - hi.astype(F32)
    mid = r1.astype(BF16)
    lo = (r1 - mid.astype(F32)).astype(BF16)
    return hi, mid, lo


def _gla_kernel(q_ref, k_ref, v_ref, r_ref, lr_ref, upf_ref, bf_ref, upb_ref, bb_ref, og_ref,
                o_ref, obwd_ref, state_ref):
    c = GLA_CHUNK
    n_chunks = q_ref.shape[0] // c
    ri = lax.broadcasted_iota(jnp.int32, (c, c), 0)
    ci = lax.broadcasted_iota(jnp.int32, (c, c), 1)
    lower_incl = ri >= ci
    upper_strict = ci > ri
    tri_prefix = jnp.where(lower_incl, 1.0, 0.0).astype(BF16)
    tri_suffix = jnp.where(ci >= ri, 1.0, 0.0).astype(BF16)

    def direction(n, up_ref, bias_ref, tri, mask, total_row):
        rows = pl.ds(pl.multiple_of(n * c, c), c)
        z = _dot(lr_ref[rows, :].astype(BF16), up_ref[...]) + bias_ref[...]
        log_g = jnp.maximum(jax.nn.log_sigmoid(z) * (1.0 / GLA_GATE_NORMALIZER),
                            GLA_LOG_GATE_MIN)
        hi, mid, lo = _split3(log_g)
        b = _dot(tri, hi) + _dot(tri, mid) + _dot(tri, lo)
        decay = jnp.exp(b)
        total = decay[total_row:total_row + 1, :]
        q_dec = (q_ref[rows, :].astype(F32) * decay).astype(BF16)
        k_inv = k_ref[rows, :].astype(F32) * jnp.exp(-b)
        k_tail = (k_inv * total).astype(BF16)
        v = v_ref[rows, :]
        scores = jnp.where(mask, _dot_nt(q_dec, k_inv.astype(BF16)), 0.0).astype(BF16)
        state_t = state_ref[...]
        out = _dot(scores, v) + _dot_nt(q_dec, state_t.astype(BF16))
        state_ref[...] = state_t * total + _dot_tn(v, k_tail)
        return rows, out

    state_ref[...] = jnp.zeros_like(state_ref)

    def bwd_body(i, carry):
        rows, out = direction(n_chunks - 1 - i, upb_ref, bb_ref, tri_suffix, upper_strict, 0)
        obwd_ref[rows, :] = out
        return carry

    lax.fori_loop(0, n_chunks, bwd_body, 0)

    state_ref[...] = jnp.zeros_like(state_ref)

    def fwd_body(n, carry):
        rows, out = direction(n, upf_ref, bf_ref, tri_prefix, lower_incl, c - 1)
        o = _rms(out + obwd_ref[rows, :], og_ref[...])
        o_ref[rows, :] = (o * jax.nn.silu(r_ref[rows, :].astype(F32))).astype(BF16)
        return carry

    lax.fori_loop(0, n_chunks, fwd_body, 0)


def _gla_branch(q, k, v, r, lr, up_f, bias_f, up_b, bias_b, out_g, batch, seq):
    qk = pl.BlockSpec((seq, GLA_DK), lambda b, h: (b, h))
    vv = pl.BlockSpec((seq, GLA_DV), lambda b, h: (b, h))
    up = pl.BlockSpec((None, LANES, GLA_DK), lambda b, h: (h, 0, 0))
    bias = pl.BlockSpec((None, 1, GLA_DK), lambda b, h: (h, 0, 0))
    return pl.pallas_call(
        _gla_kernel,
        grid=(batch, GLA_HEADS),
        in_specs=[qk, qk, vv, vv, pl.BlockSpec((seq, LANES), lambda b, h: (b, 0)),
                  up, bias, up, bias, _const_spec((1, GLA_DV))],
        out_specs=vv,
        out_shape=jax.ShapeDtypeStruct((batch * seq, GLA_V_W), BF16),
        scratch_shapes=[pltpu.VMEM((seq, GLA_DV), F32),
                        pltpu.VMEM((GLA_DV, GLA_DK), F32)],
        compiler_params=pltpu.CompilerParams(
            dimension_semantics=("parallel", "parallel"), vmem_limit_bytes=VMEM_LIMIT),
        name="gla_branch",
    )(q, k, v, r, lr, up_f, bias_f, up_b, bias_b, out_g)


def _swa_kernel(sink_ref, q_ref, k_ref, v_ref, o_ref):
    blk = SWA_BLOCK
    hd = SWA_HEAD_DIM
    tile_blocks = q_ref.shape[0] // blk
    seq_blocks = k_ref.shape[0] // blk
    tile = pl.program_id(1)
    rows_g = SWA_GROUP * blk
    qi = lax.broadcasted_iota(jnp.int32, (rows_g, blk), 0) % blk
    kj = lax.broadcasted_iota(jnp.int32, (rows_g, blk), 1)

    def block(t, carry):
        n = tile * tile_blocks + t
        q_rows = pl.ds(pl.multiple_of(t * blk, blk), blk)
        kv_rows = [pl.ds(pl.multiple_of(jnp.maximum(n - 1, 0) * blk, blk), blk),
                   pl.ds(pl.multiple_of(n * blk, blk), blk),
                   pl.ds(pl.multiple_of(jnp.minimum(n + 1, seq_blocks - 1) * blk, blk), blk)]
        masks = [kj >= qi + jnp.where(n > 0, 0, blk), None,
                 kj <= qi - jnp.where(n < seq_blocks - 1, 0, blk)]
        for hk in range(SWA_KV_HEADS):
            kv_cols = slice(hk * hd, (hk + 1) * hd)
            heads = [hk * SWA_GROUP + g for g in range(SWA_GROUP)]
            q4 = jnp.concatenate([q_ref[q_rows, hh * hd:(hh + 1) * hd] for hh in heads], axis=0)
            sink = jnp.concatenate([jnp.full((blk, 1), sink_ref[hh], F32) for hh in heads], axis=0)
            s = []
            for rows, mask in zip(kv_rows, masks):
                sj = _dot_nt(q4, k_ref[rows, kv_cols])
                s.append(sj if mask is None else jnp.where(mask, sj, MASK_VALUE))
            m = jnp.maximum(jnp.maximum(s[0], s[1]), s[2]).max(axis=-1, keepdims=True)
            m = jnp.maximum(m, sink)
            denom = jnp.exp(sink - m)
            acc = jnp.zeros((rows_g, hd), F32)
            for rows, sj in zip(kv_rows, s):
                p = jnp.exp(sj - m)
                denom = denom + p.sum(axis=-1, keepdims=True)
                acc = acc + _dot(p.astype(BF16), v_ref[rows, kv_cols])
            o = acc / denom
            for g, hh in enumerate(heads):
                o_ref[q_rows, hh * hd:(hh + 1) * hd] = o[g * blk:(g + 1) * blk, :].astype(BF16)
        return carry

    lax.fori_loop(0, tile_blocks, block, 0)


def _swa_branch(sq, sk, sv, sinks, batch, seq):
    tq = SWA_TILE_Q
    tiles = seq // tq
    return pl.pallas_call(
        _swa_kernel,
        grid=(batch, tiles),
        in_specs=[pl.BlockSpec(memory_space=pltpu.SMEM),
                  pl.BlockSpec((tq, SWA_Q_W), lambda b, t: (b * tiles + t, 0)),
                  pl.BlockSpec((seq, SWA_KV_W), lambda b, t: (b, 0)),
                  pl.BlockSpec((seq, SWA_KV_W), lambda b, t: (b, 0))],
        out_specs=pl.BlockSpec((tq, SWA_Q_W), lambda b, t: (b * tiles + t, 0)),
        out_shape=jax.ShapeDtypeStruct((batch * seq, SWA_Q_W), BF16),
        compiler_params=pltpu.CompilerParams(
            dimension_semantics=("parallel", "parallel"), vmem_limit_bytes=VMEM_LIMIT),
        name="swa_branch",
    )(sinks, sq, sk, sv)


def _out_ffn_kernel(x_ref, og_ref, os_ref, ga_ref, gb_ref, wog_ref, wos_ref, wout_ref, g2_ref,
                    wfg_ref, wfu_ref, wfo_ref, o_ref):
    y_gla = _dot(og_ref[...], wog_ref[...])
    y_swa = _dot(os_ref[...], wos_ref[...])
    merged = ga_ref[...].astype(F32) * y_gla + gb_ref[...].astype(F32) * y_swa
    x1 = x_ref[...] + _dot(merged.astype(BF16), wout_ref[...])
    h2 = _rms(x1, g2_ref[...]).astype(BF16)
    acc = x1
    for c0 in range(0, D_FF, FFN_CHUNK):
        cols = slice(c0, c0 + FFN_CHUNK)
        gate = _dot(h2, wfg_ref[:, cols])
        up = _dot(h2, wfu_ref[:, cols])
        act = (jax.nn.silu(gate) * up).astype(BF16)
        acc = acc + _dot(act, wfo_ref[cols, :])
    o_ref[...] = acc


def _out_ffn(x2, o_gla, o_swa, ga, gb, w_o_gla, w_o_swa, w_out, norm_g, w_fg, w_fu, w_fo):
    m = x2.shape[0]
    tm = OUT_TILE_M
    row = pl.BlockSpec((tm, D_MODEL), lambda i: (i, 0))
    consts = (w_o_gla, w_o_swa, w_out, norm_g, w_fg, w_fu, w_fo)
    return pl.pallas_call(
        _out_ffn_kernel,
        grid=(m // tm,),
        in_specs=[row] * 5 + [_const_spec(w.shape) for w in consts],
        out_specs=row,
        out_shape=jax.ShapeDtypeStruct((m, D_MODEL), F32),
        compiler_params=pltpu.CompilerParams(
            dimension_semantics=("parallel",), vmem_limit_bytes=VMEM_LIMIT),
        name="out_ffn",
    )(x2, o_gla, o_swa, ga, gb, *consts)


def _rope_tables(seq):
    half = SWA_HEAD_DIM // 2
    inv_freq = ROPE_THETA ** (-jnp.arange(half, dtype=F32) / half)
    ang = jnp.arange(seq).astype(F32)[:, None] * inv_freq[None, :]
    cos, sin = jnp.cos(ang), jnp.sin(ang)
    return jnp.concatenate([cos, cos], axis=-1), jnp.concatenate([-sin, sin], axis=-1)


def _gate_up(up, lr_col0):
    per_head = up.reshape(GLA_GATE_RANK, GLA_HEADS, GLA_DK).transpose(1, 0, 2)
    pad = ((0, 0), (lr_col0, LANES - GLA_GATE_RANK - lr_col0), (0, 0))
    return jnp.pad(per_head, pad).astype(BF16)


def _layer(x2, batch, seq, norm_mix_g, w_in, up_f, bias_f, up_b, bias_b, gla_out_g, w_o_gla,
           q_g, k_g, sinks, w_o_swa, w_out, norm_ffn_g, w_ffn_in, w_ffn_out):
    split_at = [int(i) for i in np.cumsum(IN_SPLITS)[:-1]]
    (w_q, w_k, w_v, w_r, w_lrf, w_lrb, w_sq, w_sk, w_sv, w_ga, w_gb) = [
        w.astype(BF16) for w in jnp.split(w_in, split_at, axis=-1)]
    w_lr = jnp.pad(jnp.concatenate([w_lrf, w_lrb], axis=-1),
                   ((0, 0), (0, LANES - 2 * GLA_GATE_RANK)))
    cos, sin = _rope_tables(seq)
    row = lambda t: t.reshape(1, -1)
    q, k, v, r, lr, sq, sk, sv, ga, gb = _in_projection(
        x2, row(norm_mix_g), cos, sin, row(q_g), row(k_g),
        (w_q, w_k, w_v, w_r, w_lr, w_sq, w_sk, w_sv, w_ga, w_gb), seq)
    o_gla = _gla_branch(
        q, k, v, r, lr,
        _gate_up(up_f, 0), bias_f.reshape(GLA_HEADS, 1, GLA_DK),
        _gate_up(up_b, GLA_GATE_RANK), bias_b.reshape(GLA_HEADS, 1, GLA_DK),
        row(gla_out_g), batch, seq)
    o_swa = _swa_branch(sq, sk, sv, sinks, batch, seq)
    return _out_ffn(x2, o_gla, o_swa, ga, gb, w_o_gla.astype(BF16), w_o_swa.astype(BF16),
                    w_out.astype(BF16), row(norm_ffn_g), w_ffn_in[:, :D_FF].astype(BF16),
                    w_ffn_in[:, D_FF:].astype(BF16), w_ffn_out.astype(BF16))


def kernel(x, norm_mix_g, w_in, gla_gate_up_fwd, gla_gate_bias_fwd, gla_gate_up_bwd,
           gla_gate_bias_bwd, gla_out_norm_g, w_o_gla, swa_q_norm_g, swa_k_norm_g,
           swa_sinks, w_o_swa, w_out, norm_ffn_g, w_ffn_in, w_ffn_out):
    batch, seq, d = x.shape
    x2 = x.reshape(batch * seq, d)
    for l in range(w_in.shape[0]):
        x2 = _layer(x2, batch, seq, norm_mix_g[l], w_in[l], gla_gate_up_fwd[l],
                    gla_gate_bias_fwd[l], gla_gate_up_bwd[l], gla_gate_bias_bwd[l],
                    gla_out_norm_g[l], w_o_gla[l], swa_q_norm_g[l], swa_k_norm_g[l],
                    swa_sinks[l], w_o_swa[l], w_out[l], norm_ffn_g[l], w_ffn_in[l],
                    w_ffn_out[l])
    return x2.reshape(batch, seq, d)
```

```python
import functools
import math

import jax
import jax.numpy as jnp
import numpy as np
from jax import lax
from jax.experimental import pallas as pl
from jax.experimental.pallas import tpu as pltpu

F32 = jnp.float32
BF16 = jnp.bfloat16

D_MODEL = 1024
NORM_EPS = 1e-6
GLA_HEADS = 4
GLA_DK = 128
GLA_DV = 256
GLA_QK_W = GLA_HEADS * GLA_DK
GLA_V_W = GLA_HEADS * GLA_DV
GLA_GATE_RANK = 16
GLA_GATE_NORMALIZER = 16.0
GLA_LOG_GATE_MIN = -0.5
GLA_CHUNK = 128
GLA_UNROLL = 4
SWA_HEADS = 8
SWA_KV_HEADS = 2
SWA_GROUP = SWA_HEADS // SWA_KV_HEADS
SWA_HEAD_DIM = 128
SWA_Q_W = SWA_HEADS * SWA_HEAD_DIM
SWA_KV_W = SWA_KV_HEADS * SWA_HEAD_DIM
SWA_BLOCK = 128
ROPE_THETA = 10000.0
D_FF = 2816
IN_SPLITS = (GLA_QK_W, GLA_QK_W, GLA_V_W, GLA_V_W, GLA_GATE_RANK, GLA_GATE_RANK,
             SWA_Q_W, SWA_KV_W, SWA_KV_W, D_MODEL, D_MODEL)

LANES = 128
MASK_VALUE = -1e30
VMEM_LIMIT = 56 * 1024 * 1024

IN_TILE_M = 512
SWA_TILE_Q = 512
OUT_TILE_M = 512
FFN_CHUNK = D_FF // 2


def _const_spec(shape):
    zeros = (0,) * len(shape)
    return pl.BlockSpec(shape, lambda *_: zeros, pipeline_mode=pl.Buffered(1))


def _dot(a, b):
    return jnp.dot(a, b, preferred_element_type=F32)


def _dot_nt(a, b):
    return lax.dot_general(a, b, (((1,), (1,)), ((), ())), preferred_element_type=F32)


def _dot_tn(a, b):
    return lax.dot_general(a, b, (((0,), (0,)), ((), ())), preferred_element_type=F32)


def _rms(x, gain):
    ms = jnp.mean(x * x, axis=-1, keepdims=True)
    return x * lax.rsqrt(ms + NORM_EPS) * gain


def _inproj_kernel(x_ref, g_ref, cos_ref, sin_ref, qg_ref, kg_ref,
                   wq_ref, wk_ref, wv_ref, wr_ref, wlr_ref, wsq_ref, wsk_ref, wsv_ref,
                   wga_ref, wgb_ref,
                   q_ref, k_ref, v_ref, r_ref, lr_ref, sq_ref, sk_ref, sv_ref, ga_ref, gb_ref):
    h = _rms(x_ref[...], g_ref[...]).astype(BF16)
    q_ref[...] = (_dot(h, wq_ref[...]) * (GLA_DK ** -0.5)).astype(BF16)
    k_ref[...] = _dot(h, wk_ref[...]).astype(BF16)
    v_ref[...] = _dot(h, wv_ref[...]).astype(BF16)
    r_ref[...] = _dot(h, wr_ref[...]).astype(BF16)
    lr_ref[...] = _dot(h, wlr_ref[...])
    sv_ref[...] = _dot(h, wsv_ref[...]).astype(BF16)
    ga_ref[...] = jax.nn.sigmoid(_dot(h, wga_ref[...])).astype(BF16)
    gb_ref[...] = jax.nn.sigmoid(_dot(h, wgb_ref[...])).astype(BF16)

    cos = cos_ref[...]
    sin = sin_ref[...]

    def norm_rope(t, gain, scale):
        y = _rms(t, gain)
        rot = pltpu.roll(y, SWA_HEAD_DIM // 2, axis=1)
        return ((y * cos + rot * sin) * scale).astype(BF16)

    sq = _dot(h, wsq_ref[...])
    for hh in range(SWA_HEADS):
        cols = slice(hh * SWA_HEAD_DIM, (hh + 1) * SWA_HEAD_DIM)
        sq_ref[:, cols] = norm_rope(sq[:, cols], qg_ref[...], SWA_HEAD_DIM ** -0.5)
    sk = _dot(h, wsk_ref[...])
    for hh in range(SWA_KV_HEADS):
        cols = slice(hh * SWA_HEAD_DIM, (hh + 1) * SWA_HEAD_DIM)
        sk_ref[:, cols] = norm_rope(sk[:, cols], kg_ref[...], 1.0)


def _in_projection(x2, norm_g, cos, sin, q_g, k_g, weights, seq):
    m = x2.shape[0]
    tm = IN_TILE_M
    pos_blocks = seq // tm
    row = lambda w: pl.BlockSpec((tm, w), lambda i: (i, 0))
    pos = pl.BlockSpec((tm, SWA_HEAD_DIM), lambda i: (i % pos_blocks, 0))
    widths = (GLA_QK_W, GLA_QK_W, GLA_V_W, GLA_V_W, LANES, SWA_Q_W, SWA_KV_W, SWA_KV_W,
              D_MODEL, D_MODEL)
    dtypes = (BF16, BF16, BF16, BF16, F32, BF16, BF16, BF16, BF16, BF16)
    return pl.pallas_call(
        _inproj_kernel,
        grid=(m // tm,),
        in_specs=[row(D_MODEL), _const_spec((1, D_MODEL)), pos, pos,
                  _const_spec((1, SWA_HEAD_DIM)), _const_spec((1, SWA_HEAD_DIM))]
                 + [_const_spec(w.shape) for w in weights],
        out_specs=[row(w) for w in widths],
        out_shape=[jax.ShapeDtypeStruct((m, w), d) for w, d in zip(widths, dtypes)],
        compiler_params=pltpu.CompilerParams(
            dimension_semantics=("parallel",), vmem_limit_bytes=VMEM_LIMIT),
        name="in_projection",
    )(x2, norm_g, cos, sin, q_g, k_g, *weights)


def _split3(x):
    hi = x.astype(BF16)
    r1 = x - hi.astype(F32)
    mid = r1.astype(BF16)
    lo = (r1 - mid.astype(F32)).astype(BF16)
    return hi, mid, lo


def _gla_kernel(q_ref, k_ref, v_ref, r_ref, lr_ref, upf_ref, bf_ref, upb_ref, bb_ref, og_ref,
                o_ref, osum_ref, fstate_ref, bstate_ref):
    c = GLA_CHUNK
    n_chunks = q_ref.shape[0] // c
    ri = lax.broadcasted_iota(jnp.int32, (c, c), 0)
    ci = lax.broadcasted_iota(jnp.int32, (c, c), 1)
    lower_incl = ri >= ci
    upper_strict = ci > ri
    tri_prefix = jnp.where(lower_incl, 1.0, 0.0).astype(BF16)
    tri_suffix = jnp.where(ci >= ri, 1.0, 0.0).astype(BF16)

    fwd = (upf_ref, bf_ref, tri_prefix, lower_incl, c - 1)
    bwd = (upb_ref, bb_ref, tri_suffix, upper_strict, 0)

    def run(fwd_chunks, bwd_chunks):
        jobs = [(n, fwd) for n in fwd_chunks] + [(n, bwd) for n in bwd_chunks]
        rows = [pl.ds(pl.multiple_of(n * c, c), c) for n, _ in jobs]
        z = [_dot(lr_ref[r, :].astype(BF16), d[0][...]) + d[1][...]
             for r, (_, d) in zip(rows, jobs)]
        log_g = [jnp.maximum(jax.nn.log_sigmoid(zj) * (1.0 / GLA_GATE_NORMALIZER),
                             GLA_LOG_GATE_MIN) for zj in z]
        parts = [_split3(g) for g in log_g]
        b = [_dot(d[2], hi) + _dot(d[2], mid) + _dot(d[2], lo)
             for (hi, mid, lo), (_, d) in zip(parts, jobs)]
        decay = [jnp.exp(bj) for bj in b]
        total = [dj[d[4]:d[4] + 1, :] for dj, (_, d) in zip(decay, jobs)]
        q_dec = [(q_ref[r, :].astype(F32) * dj).astype(BF16) for r, dj in zip(rows, decay)]
        k_inv = [k_ref[r, :].astype(F32) * jnp.exp(-bj) for r, bj in zip(rows, b)]
        k_tail = [(kj * tj).astype(BF16) for kj, tj in zip(k_inv, total)]
        v = [v_ref[r, :] for r in rows]
        scores = [jnp.where(d[3], _dot_nt(qj, kj.astype(BF16)), 0.0).astype(BF16)
                  for qj, kj, (_, d) in zip(q_dec, k_inv, jobs)]
        intra = [_dot(sj, vj) for sj, vj in zip(scores, v)]
        kv_t = [_dot_tn(vj, kj) for vj, kj in zip(v, k_tail)]
        out = []
        for state_ref, lo, hi in ((fstate_ref, 0, len(fwd_chunks)),
                                  (bstate_ref, len(fwd_chunks), len(jobs))):
            state_t = state_ref[...]
            for j in range(lo, hi):
                out.append(intra[j] + _dot_nt(q_dec[j], state_t.astype(BF16)))
                state_t = state_t * total[j] + kv_t[j]
            state_ref[...] = state_t
        return rows, out

    def finalize(rows, o_sum):
        o = _rms(o_sum, og_ref[...])
        return (o * jax.nn.silu(r_ref[rows, :].astype(F32))).astype(BF16)

    fstate_ref[...] = jnp.zeros_like(fstate_ref)
    bstate_ref[...] = jnp.zeros_like(bstate_ref)

    u = GLA_UNROLL

    def chunks(i):
        f = [i * u + t for t in range(u)]
        return f, [n_chunks - 1 - n for n in f]

    def first_half(i, carry):
        rows, out = run(*chunks(i))
        for r, o in zip(rows, out):
            osum_ref[r, :] = o
        return carry

    def second_half(i, carry):
        rows, out = run(*chunks(i))
        finals = [finalize(r, o + osum_ref[r, :]) for r, o in zip(rows, out)]
        for r, o in zip(rows, finals):
            o_ref[r, :] = o
        return carry

    steps = n_chunks // u
    lax.fori_loop(0, steps // 2, first_half, 0)
    lax.fori_loop(steps // 2, steps, second_half, 0)


def _gla_branch(q, k, v, r, lr, up_f, bias_f, up_b, bias_b, out_g, batch, seq):
    qk = pl.BlockSpec((seq, GLA_DK), lambda b, h: (b, h))
    vv = pl.BlockSpec((seq, GLA_DV), lambda b, h: (b, h))
    up = pl.BlockSpec((None, LANES, GLA_DK), lambda b, h: (h, 0, 0))
    bias = pl.BlockSpec((None, 1, GLA_DK), lambda b, h: (h, 0, 0))
    return pl.pallas_call(
        _gla_kernel,
        grid=(batch, GLA_HEADS),
        in_specs=[qk, qk, vv, vv, pl.BlockSpec((seq, LANES), lambda b, h: (b, 0)),
                  up, bias, up, bias, _const_spec((1, GLA_DV))],
        out_specs=vv,
        out_shape=jax.ShapeDtypeStruct((batch * seq, GLA_V_W), BF16),
        scratch_shapes=[pltpu.VMEM((seq, GLA_DV), F32),
                        pltpu.VMEM((GLA_DV, GLA_DK), F32),
                        pltpu.VMEM((GLA_DV, GLA_DK), F32)],
        compiler_params=pltpu.CompilerParams(
            dimension_semantics=("parallel", "parallel"), vmem_limit_bytes=VMEM_LIMIT),
        name="gla_branch",
    )(q, k, v, r, lr, up_f, bias_f, up_b, bias_b, out_g)


def _swa_kernel(sink_ref, q_ref, k_ref, v_ref, o_ref):
    blk = SWA_BLOCK
    hd = SWA_HEAD_DIM
    tile_blocks = q_ref.shape[0] // blk
    seq_blocks = k_ref.shape[0] // blk
    tile = pl.program_id(1)
    rows_g = SWA_GROUP * blk
    qi = lax.broadcasted_iota(jnp.int32, (rows_g, blk), 0) % blk
    kj = lax.broadcasted_iota(jnp.int32, (rows_g, blk), 1)

    def block(t, carry):
        n = tile * tile_blocks + t
        q_rows = pl.ds(pl.multiple_of(t * blk, blk), blk)
        kv_rows = [pl.ds(pl.multiple_of(jnp.maximum(n - 1, 0) * blk, blk), blk),
                   pl.ds(pl.multiple_of(n * blk, blk), blk),
                   pl.ds(pl.multiple_of(jnp.minimum(n + 1, seq_blocks - 1) * blk, blk), blk)]
        masks = [kj >= qi + jnp.where(n > 0, 0, blk), None,
                 kj <= qi - jnp.where(n < seq_blocks - 1, 0, blk)]
        for hk in range(SWA_KV_HEADS):
            kv_cols = slice(hk * hd, (hk + 1) * hd)
            heads = [hk * SWA_GROUP + g for g in range(SWA_GROUP)]
            q4 = jnp.concatenate([q_ref[q_rows, hh * hd:(hh + 1) * hd] for hh in heads], axis=0)
            sink = jnp.concatenate([jnp.full((blk, 1), sink_ref[hh], F32) for hh in heads], axis=0)
            s = []
            for rows, mask in zip(kv_rows, masks):
                sj = _dot_nt(q4, k_ref[rows, kv_cols])
                s.append(sj if mask is None else jnp.where(mask, sj, MASK_VALUE))
            m = jnp.maximum(jnp.maximum(s[0], s[1]), s[2]).max(axis=-1, keepdims=True)
            m = jnp.maximum(m, sink)
            denom = jnp.exp(sink - m)
            acc = jnp.zeros((rows_g, hd), F32)
            for rows, sj in zip(kv_rows, s):
                p = jnp.exp(sj - m)
                denom = denom + p.sum(axis=-1, keepdims=True)
                acc = acc + _dot(p.astype(BF16), v_ref[rows, kv_cols])
            o = acc / denom
            for g, hh in enumerate(heads):
                o_ref[q_rows, hh * hd:(hh + 1) * hd] = o[g * blk:(g + 1) * blk, :].astype(BF16)
        return carry

    lax.fori_loop(0, tile_blocks, block, 0)


def _swa_branch(sq, sk, sv, sinks, batch, seq):
    tq = SWA_TILE_Q
    tiles = seq // tq
    return pl.pallas_call(
        _swa_kernel,
        grid=(batch, tiles),
        in_specs=[pl.BlockSpec(memory_space=pltpu.SMEM),
                  pl.BlockSpec((tq, SWA_Q_W), lambda b, t: (b * tiles + t, 0)),
                  pl.BlockSpec((seq, SWA_KV_W), lambda b, t: (b, 0)),
                  pl.BlockSpec((seq, SWA_KV_W), lambda b, t: (b, 0))],
        out_specs=pl.BlockSpec((tq, SWA_Q_W), lambda b, t: (b * tiles + t, 0)),
        out_shape=jax.ShapeDtypeStruct((batch * seq, SWA_Q_W), BF16),
        compiler_params=pltpu.CompilerParams(
            dimension_semantics=("parallel", "parallel"), vmem_limit_bytes=VMEM_LIMIT),
        name="swa_branch",
    )(sinks, sq, sk, sv)


def _out_ffn_kernel(x_ref, og_ref, os_ref, ga_ref, gb_ref, wog_ref, wos_ref, wout_ref, g2_ref,
                    wfg_ref, wfu_ref, wfo_ref, o_ref):
    y_gla = _dot(og_ref[...], wog_ref[...])
    y_swa = _dot(os_ref[...], wos_ref[...])
    merged = ga_ref[...].astype(F32) * y_gla + gb_ref[...].astype(F32) * y_swa
    x1 = x_ref[...] + _dot(merged.astype(BF16), wout_ref[...])
    h2 = _rms(x1, g2_ref[...]).astype(BF16)
    acc = x1
    for c0 in range(0, D_FF, FFN_CHUNK):
        cols = slice(c0, c0 + FFN_CHUNK)
        gate = _dot(h2, wfg_ref[:, cols])
        up = _dot(h2, wfu_ref[:, cols])
        act = (jax.nn.silu(gate) * up).astype(BF16)
        acc = acc + _dot(act, wfo_ref[cols, :])
    o_ref[...] = acc


def _out_ffn(x2, o_gla, o_swa, ga, gb, w_o_gla, w_o_swa, w_out, norm_g, w_fg, w_fu, w_fo):
    m = x2.shape[0]
    tm = OUT_TILE_M
    row = pl.BlockSpec((tm, D_MODEL), lambda i: (i, 0))
    consts = (w_o_gla, w_o_swa, w_out, norm_g, w_fg, w_fu, w_fo)
    return pl.pallas_call(
        _out_ffn_kernel,
        grid=(m // tm,),
        in_specs=[row] * 5 + [_const_spec(w.shape) for w in consts],
        out_specs=row,
        out_shape=jax.ShapeDtypeStruct((m, D_MODEL), F32),
        compiler_params=pltpu.CompilerParams(
            dimension_semantics=("parallel",), vmem_limit_bytes=VMEM_LIMIT),
        name="out_ffn",
    )(x2, o_gla, o_swa, ga, gb, *consts)


def _rope_tables(seq):
    half = SWA_HEAD_DIM // 2
    inv_freq = ROPE_THETA ** (-jnp.arange(half, dtype=F32) / half)
    ang = jnp.arange(seq).astype(F32)[:, None] * inv_freq[None, :]
    cos, sin = jnp.cos(ang), jnp.sin(ang)
    return jnp.concatenate([cos, cos], axis=-1), jnp.concatenate([-sin, sin], axis=-1)


def _gate_up(up, lr_col0):
    per_head = up.reshape(GLA_GATE_RANK, GLA_HEADS, GLA_DK).transpose(1, 0, 2)
    pad = ((0, 0), (lr_col0, LANES - GLA_GATE_RANK - lr_col0), (0, 0))
    return jnp.pad(per_head, pad).astype(BF16)


def _layer(x2, batch, seq, norm_mix_g, w_in, up_f, bias_f, up_b, bias_b, gla_out_g, w_o_gla,
           q_g, k_g, sinks, w_o_swa, w_out, norm_ffn_g, w_ffn_in, w_ffn_out):
    split_at = [int(i) for i in np.cumsum(IN_SPLITS)[:-1]]
    (w_q, w_k, w_v, w_r, w_lrf, w_lrb, w_sq, w_sk, w_sv, w_ga, w_gb) = [
        w.astype(BF16) for w in jnp.split(w_in, split_at, axis=-1)]
    w_lr = jnp.pad(jnp.concatenate([w_lrf, w_lrb], axis=-1),
                   ((0, 0), (0, LANES - 2 * GLA_GATE_RANK)))
    cos, sin = _rope_tables(seq)
    row = lambda t: t.reshape(1, -1)
    q, k, v, r, lr, sq, sk, sv, ga, gb = _in_projection(
        x2, row(norm_mix_g), cos, sin, row(q_g), row(k_g),
        (w_q, w_k, w_v, w_r, w_lr, w_sq, w_sk, w_sv, w_ga, w_gb), seq)
    o_gla = _gla_branch(
        q, k, v, r, lr,
        _gate_up(up_f, 0), bias_f.reshape(GLA_HEADS, 1, GLA_DK),
        _gate_up(up_b, GLA_GATE_RANK), bias_b.reshape(GLA_HEADS, 1, GLA_DK),
        row(gla_out_g), batch, seq)
    o_swa = _swa_branch(sq, sk, sv, sinks, batch, seq)
    return _out_ffn(x2, o_gla, o_swa, ga, gb, w_o_gla.astype(BF16), w_o_swa.astype(BF16),
                    w_out.astype(BF16), row(norm_ffn_g), w_ffn_in[:, :D_FF].astype(BF16),
                    w_ffn_in[:, D_FF:].astype(BF16), w_ffn_out.astype(BF16))


def kernel(x, norm_mix_g, w_in, gla_gate_up_fwd, gla_gate_bias_fwd, gla_gate_up_bwd,
           gla_gate_bias_bwd, gla_out_norm_g, w_o_gla, swa_q_norm_g, swa_k_norm_g,
           swa_sinks, w_o_swa, w_out, norm_ffn_g, w_ffn_in, w_ffn_out):
    batch, seq, d = x.shape
    x2 = x.reshape(batch * seq, d)
    for l in range(w_in.shape[0]):
        x2 = _layer(x2, batch, seq, norm_mix_g[l], w_in[l], gla_gate_up_fwd[l],
                    gla_gate_bias_fwd[l], gla_gate_up_bwd[l], gla_gate_bias_bwd[l],
                    gla_out_norm_g[l], w_o_gla[l], swa_q_norm_g[l], swa_k_norm_g[l],
                    swa_sinks[l], w_o_swa[l], w_out[l], norm_ffn_g[l], w_ffn_in[l],
                    w_ffn_out[l])
    return x2.reshape(batch, seq, d)
```

```python
import functools
import math

import jax
import jax.numpy as jnp
import numpy as np
from jax import lax
from jax.experimental import pallas as pl
from jax.experimental.pallas import tpu as pltpu

F32 = jnp.float32
BF16 = jnp.bfloat16

D_MODEL = 1024
NORM_EPS = 1e-6
GLA_HEADS = 4
GLA_DK = 128
GLA_DV = 256
GLA_QK_W = GLA_HEADS * GLA_DK
GLA_V_W = GLA_HEADS * GLA_DV
GLA_GATE_RANK = 16
GLA_GATE_NORMALIZER = 16.0
GLA_LOG_GATE_MIN = -0.5
GLA_CHUNK = 128
GLA_UNROLL = 4
SWA_HEADS = 8
SWA_KV_HEADS = 2
SWA_GROUP = SWA_HEADS // SWA_KV_HEADS
SWA_HEAD_DIM = 128
SWA_Q_W = SWA_HEADS * SWA_HEAD_DIM
SWA_KV_W = SWA_KV_HEADS * SWA_HEAD_DIM
SWA_BLOCK = 128
ROPE_THETA = 10000.0
D_FF = 2816
IN_SPLITS = (GLA_QK_W, GLA_QK_W, GLA_V_W, GLA_V_W, GLA_GATE_RANK, GLA_GATE_RANK,
             SWA_Q_W, SWA_KV_W, SWA_KV_W, D_MODEL, D_MODEL)

LANES = 128
MXU_COLS = 256
MASK_VALUE = -1e30
LOG2_E = math.log2(math.e)
VMEM_LIMIT = 56 * 1024 * 1024

IN_TILE_M = 512
SWA_TILE_Q = 512
SWA_UNROLL = 2
OUT_TILE_M = 512
FFN_CHUNK = D_FF // 2


def _const_spec(shape):
    zeros = (0,) * len(shape)
    return pl.BlockSpec(shape, lambda *_: zeros, pipeline_mode=pl.Buffered(1))


def _dot(a, b):
    return jnp.dot(a, b, preferred_element_type=F32)


def _dot_nt(a, b):
    return lax.dot_general(a, b, (((1,), (1,)), ((), ())), preferred_element_type=F32)


def _dot_tn(a, b):
    return lax.dot_general(a, b, (((0,), (0,)), ((), ())), preferred_element_type=F32)


def _rms(x, gain):
    ms = jnp.mean(x * x, axis=-1, keepdims=True)
    return x * lax.rsqrt(ms + NORM_EPS) * gain


def _inproj_kernel(x_ref, g_ref, cos_ref, sin_ref, qg_ref, kg_ref,
                   wq_ref, wk_ref, wv_ref, wr_ref, wlr_ref, wsq_ref, wsk_ref, wsv_ref,
                   wga_ref, wgb_ref,
                   q_ref, k_ref, v_ref, r_ref, lr_ref, sq_ref, sk_ref, sv_ref, ga_ref, gb_ref):
    h = _rms(x_ref[...], g_ref[...]).astype(BF16)
    cos = cos_ref[...]
    sin = sin_ref[...]

    def norm_rope(t, gain, scale):
        y = _rms(t, gain)
        rot = pltpu.roll(y, SWA_HEAD_DIM // 2, axis=1)
        return ((y * cos + rot * sin) * scale).astype(BF16)

    hd = SWA_HEAD_DIM
    for w_ref, o_ref, gain_ref, scale in ((wsk_ref, sk_ref, kg_ref, 1.0),
                                          (wsq_ref, sq_ref, qg_ref, hd ** -0.5 * LOG2_E)):
        for c0 in range(0, o_ref.shape[1], MXU_COLS):
            t = _dot(h, w_ref[:, c0:c0 + MXU_COLS])
            for c1 in range(0, MXU_COLS, hd):
                o_ref[:, c0 + c1:c0 + c1 + hd] = norm_rope(t[:, c1:c1 + hd], gain_ref[...], scale)
    ga_ref[...] = jax.nn.sigmoid(_dot(h, wga_ref[...])).astype(BF16)
    gb_ref[...] = jax.nn.sigmoid(_dot(h, wgb_ref[...])).astype(BF16)
    q_ref[...] = (_dot(h, wq_ref[...]) * (GLA_DK ** -0.5)).astype(BF16)
    k_ref[...] = _dot(h, wk_ref[...]).astype(BF16)
    lr_ref[...] = _dot(h, wlr_ref[...])
    sv_ref[...] = _dot(h, wsv_ref[...]).astype(BF16)
    v_ref[...] = _dot(h, wv_ref[...]).astype(BF16)
    r_ref[...] = _dot(h, wr_ref[...]).astype(BF16)


def _in_projection(x2, norm_g, cos, sin, q_g, k_g, weights, seq):
    m = x2.shape[0]
    tm = IN_TILE_M
    pos_blocks = seq // tm
    row = lambda w: pl.BlockSpec((tm, w), lambda i: (i, 0))
    pos = pl.BlockSpec((tm, SWA_HEAD_DIM), lambda i: (i % pos_blocks, 0))
    widths = (GLA_QK_W, GLA_QK_W, GLA_V_W, GLA_V_W, LANES, SWA_Q_W, SWA_KV_W, SWA_KV_W,
              D_MODEL, D_MODEL)
    dtypes = (BF16, BF16, BF16, BF16, F32, BF16, BF16, BF16, BF16, BF16)
    return pl.pallas_call(
        _inproj_kernel,
        grid=(m // tm,),
        in_specs=[row(D_MODEL), _const_spec((1, D_MODEL)), pos, pos,
                  _const_spec((1, SWA_HEAD_DIM)), _const_spec((1, SWA_HEAD_DIM))]
                 + [_const_spec(w.shape) for w in weights],
        out_specs=[row(w) for w in widths],
        out_shape=[jax.ShapeDtypeStruct((m, w), d) for w, d in zip(widths, dtypes)],
        compiler_params=pltpu.CompilerParams(
            dimension_semantics=("parallel",), vmem_limit_bytes=VMEM_LIMIT),
        name="in_projection",
    )(x2, norm_g, cos, sin, q_g, k_g, *weights)


def _split3(x):
    hi = x.astype(BF16)
    r1 = x - hi.astype(F32)
    mid = r1.astype(BF16)
    lo = (r1 - mid.astype(F32)).astype(BF16)
    return hi, mid, lo


def _gla_kernel(q_ref, k_ref, v_ref, r_ref, lr_ref, upf_ref, bf_ref, upb_ref, bb_ref, og_ref,
                o_ref, osum_ref, fstate_ref, bstate_ref):
    c = GLA_CHUNK
    n_chunks = q_ref.shape[0] // c
    ri = lax.broadcasted_iota(jnp.int32, (c, c), 0)
    ci = lax.broadcasted_iota(jnp.int32, (c, c), 1)
    lower_incl = ri >= ci
    upper_strict = ci > ri
    tri_prefix = jnp.where(lower_incl, 1.0, 0.0).astype(BF16)
    tri_suffix = jnp.where(ci >= ri, 1.0, 0.0).astype(BF16)

    fwd = (upf_ref, bf_ref, tri_prefix, lower_incl, c - 1)
    bwd = (upb_ref, bb_ref, tri_suffix, upper_strict, 0)

    def run(fwd_chunks, bwd_chunks):
        jobs = [(n, fwd) for n in fwd_chunks] + [(n, bwd) for n in bwd_chunks]
        rows = [pl.ds(pl.multiple_of(n * c, c), c) for n, _ in jobs]
        z = [_dot(lr_ref[r, :].astype(BF16), d[0][...]) + d[1][...]
             for r, (_, d) in zip(rows, jobs)]
        log_g = [jnp.maximum(jax.nn.log_sigmoid(zj) * (1.0 / GLA_GATE_NORMALIZER),
                             GLA_LOG_GATE_MIN) for zj in z]
        parts = [_split3(g) for g in log_g]
        b = [_dot(d[2], hi) + _dot(d[2], mid) + _dot(d[2], lo)
             for (hi, mid, lo), (_, d) in zip(parts, jobs)]
        decay = [jnp.exp(bj) for bj in b]
        total = [dj[d[4]:d[4] + 1, :] for dj, (_, d) in zip(decay, jobs)]
        q_dec = [(q_ref[r, :].astype(F32) * dj).astype(BF16) for r, dj in zip(rows, decay)]
        k_inv = [k_ref[r, :].astype(F32) * jnp.exp(-bj) for r, bj in zip(rows, b)]
        k_tail = [(kj * tj).astype(BF16) for kj, tj in zip(k_inv, total)]
        v = [v_ref[r, :] for r in rows]
        scores = [jnp.where(d[3], _dot_nt(qj, kj.astype(BF16)), 0.0).astype(BF16)
                  for qj, kj, (_, d) in zip(q_dec, k_inv, jobs)]
        intra = [_dot(sj, vj) for sj, vj in zip(scores, v)]
        kv_t = [_dot_tn(vj, kj) for vj, kj in zip(v, k_tail)]
        out = []
        for state_ref, lo, hi in ((fstate_ref, 0, len(fwd_chunks)),
                                  (bstate_ref, len(fwd_chunks), len(jobs))):
            state_t = state_ref[...]
            for j in range(lo, hi):
                out.append(intra[j] + _dot_nt(q_dec[j], state_t.astype(BF16)))
                state_t = state_t * total[j] + kv_t[j]
            state_ref[...] = state_t
        return rows, out

    def finalize(rows, o_sum):
        o = _rms(o_sum, og_ref[...])
        return (o * jax.nn.silu(r_ref[rows, :].astype(F32))).astype(BF16)

    fstate_ref[...] = jnp.zeros_like(fstate_ref)
    bstate_ref[...] = jnp.zeros_like(bstate_ref)

    u = GLA_UNROLL

    def chunks(i):
        f = [i * u + t for t in range(u)]
        return f, [n_chunks - 1 - n for n in f]

    def first_half(i, carry):
        rows, out = run(*chunks(i))
        for r, o in zip(rows, out):
            osum_ref[r, :] = o
        return carry

    def second_half(i, carry):
        rows, out = run(*chunks(i))
        finals = [finalize(r, o + osum_ref[r, :]) for r, o in zip(rows, out)]
        for r, o in zip(rows, finals):
            o_ref[r, :] = o
        return carry

    steps = n_chunks // u
    lax.fori_loop(0, steps // 2, first_half, 0)
    lax.fori_loop(steps // 2, steps, second_half, 0)


def _gla_branch(q, k, v, r, lr, up_f, bias_f, up_b, bias_b, out_g, batch, seq):
    qk = pl.BlockSpec((seq, GLA_DK), lambda b, h: (b, h))
    vv = pl.BlockSpec((seq, GLA_DV), lambda b, h: (b, h))
    up = pl.BlockSpec((None, LANES, GLA_DK), lambda b, h: (h, 0, 0))
    bias = pl.BlockSpec((None, 1, GLA_DK), lambda b, h: (h, 0, 0))
    return pl.pallas_call(
        _gla_kernel,
        grid=(batch, GLA_HEADS),
        in_specs=[qk, qk, vv, vv, pl.BlockSpec((seq, LANES), lambda b, h: (b, 0)),
                  up, bias, up, bias, _const_spec((1, GLA_DV))],
        out_specs=vv,
        out_shape=jax.ShapeDtypeStruct((batch * seq, GLA_V_W), BF16),
        scratch_shapes=[pltpu.VMEM((seq, GLA_DV), F32),
                        pltpu.VMEM((GLA_DV, GLA_DK), F32),
                        pltpu.VMEM((GLA_DV, GLA_DK), F32)],
        compiler_params=pltpu.CompilerParams(
            dimension_semantics=("parallel", "parallel"), vmem_limit_bytes=VMEM_LIMIT),
        name="gla_branch",
    )(q, k, v, r, lr, up_f, bias_f, up_b, bias_b, out_g)


def _swa_kernel(sink_ref, q_ref, k_ref, v_ref, o_ref):
    blk = SWA_BLOCK
    hd = SWA_HEAD_DIM
    tile_blocks = q_ref.shape[0] // blk
    seq_blocks = k_ref.shape[0] // blk
    tile = pl.program_id(1)
    rows_g = SWA_GROUP * blk
    qi = lax.broadcasted_iota(jnp.int32, (rows_g, blk), 0) % blk
    kj = lax.broadcasted_iota(jnp.int32, (rows_g, blk), 1)
    ones = jnp.ones((blk, hd), BF16)
    head_cols = lambda hh: slice(hh * hd, (hh + 1) * hd)

    def step(i, carry):
        jobs = [(i * SWA_UNROLL + t, hk) for t in range(SWA_UNROLL) for hk in range(SWA_KV_HEADS)]
        q4, sink, kv_rows, masks = [], [], [], []
        for t, hk in jobs:
            n = tile * tile_blocks + t
            heads = [hk * SWA_GROUP + g for g in range(SWA_GROUP)]
            q_rows = pl.ds(pl.multiple_of(t * blk, blk), blk)
            q4.append(jnp.concatenate([q_ref[q_rows, head_cols(hh)] for hh in heads], axis=0))
            sink.append(jnp.concatenate(
                [jnp.full((blk, 1), sink_ref[hh] * LOG2_E, F32) for hh in heads], axis=0))
            kv_rows.append([pl.ds(pl.multiple_of(nn * blk, blk), blk) for nn in
                            (jnp.maximum(n - 1, 0), n, jnp.minimum(n + 1, seq_blocks - 1))])
            masks.append([kj >= qi + jnp.where(n > 0, 0, blk), None,
                          kj <= qi - jnp.where(n < seq_blocks - 1, 0, blk)])
        s = [[_dot_nt(q, k_ref[r, head_cols(hk)]) for r in rows]
             for q, rows, (_, hk) in zip(q4, kv_rows, jobs)]
        s = [[sj if mask is None else jnp.where(mask, sj, MASK_VALUE)
              for sj, mask in zip(sb, mb)] for sb, mb in zip(s, masks)]
        m = [jnp.maximum(jnp.maximum(jnp.maximum(sb[0], sb[1]), sb[2]).max(axis=-1, keepdims=True),
                         sk) for sb, sk in zip(s, sink)]
        p = [[jnp.exp2(sj - mb).astype(BF16) for sj in sb] for sb, mb in zip(s, m)]
        pv = [sum(_dot(pj, jnp.concatenate([v_ref[r, head_cols(hk)], ones], axis=1))
                  for pj, r in zip(pb, rows))
              for pb, rows, (_, hk) in zip(p, kv_rows, jobs)]
        for acc, mb, sk, (t, hk) in zip(pv, m, sink, jobs):
            o = acc[:, :hd] / (acc[:, hd:] + jnp.exp2(sk - mb))
            q_rows = pl.ds(pl.multiple_of(t * blk, blk), blk)
            for g in range(SWA_GROUP):
                o_ref[q_rows, head_cols(hk * SWA_GROUP + g)] = (
                    o[g * blk:(g + 1) * blk, :].astype(BF16))
        return carry

    lax.fori_loop(0, tile_blocks // SWA_UNROLL, step, 0)


def _swa_branch(sq, sk, sv, sinks, batch, seq):
    tq = SWA_TILE_Q
    tiles = seq // tq
    return pl.pallas_call(
        _swa_kernel,
        grid=(batch, tiles),
        in_specs=[pl.BlockSpec(memory_space=pltpu.SMEM),
                  pl.BlockSpec((tq, SWA_Q_W), lambda b, t: (b * tiles + t, 0)),
                  pl.BlockSpec((seq, SWA_KV_W), lambda b, t: (b, 0)),
                  pl.BlockSpec((seq, SWA_KV_W), lambda b, t: (b, 0))],
        out_specs=pl.BlockSpec((tq, SWA_Q_W), lambda b, t: (b * tiles + t, 0)),
        out_shape=jax.ShapeDtypeStruct((batch * seq, SWA_Q_W), BF16),
        compiler_params=pltpu.CompilerParams(
            dimension_semantics=("parallel", "parallel"), vmem_limit_bytes=VMEM_LIMIT),
        name="swa_branch",
    )(sinks, sq, sk, sv)


def _out_ffn_kernel(x_ref, og_ref, os_ref, ga_ref, gb_ref, wog_ref, wos_ref, wout_ref, g2_ref,
                    wfg_ref, wfu_ref, wfo_ref, o_ref):
    y_gla = _dot(og_ref[...], wog_ref[...])
    y_swa = _dot(os_ref[...], wos_ref[...])
    merged = ga_ref[...].astype(F32) * y_gla + gb_ref[...].astype(F32) * y_swa
    x1 = x_ref[...] + _dot(merged.astype(BF16), wout_ref[...])
    h2 = _rms(x1, g2_ref[...]).astype(BF16)
    acc = x1
    for c0 in range(0, D_FF, FFN_CHUNK):
        cols = slice(c0, c0 + FFN_CHUNK)
        gate = _dot(h2, wfg_ref[:, cols])
        up = _dot(h2, wfu_ref[:, cols])
        act = (jax.nn.silu(gate) * up).astype(BF16)
        acc = acc + _dot(act, wfo_ref[cols, :])
    o_ref[...] = acc


def _out_ffn(x2, o_gla, o_swa, ga, gb, w_o_gla, w_o_swa, w_out, norm_g, w_fg, w_fu, w_fo):
    m = x2.shape[0]
    tm = OUT_TILE_M
    row = pl.BlockSpec((tm, D_MODEL), lambda i: (i, 0))
    consts = (w_o_gla, w_o_swa, w_out, norm_g, w_fg, w_fu, w_fo)
    return pl.pallas_call(
        _out_ffn_kernel,
        grid=(m // tm,),
        in_specs=[row] * 5 + [_const_spec(w.shape) for w in consts],
        out_specs=row,
        out_shape=jax.ShapeDtypeStruct((m, D_MODEL), F32),
        compiler_params=pltpu.CompilerParams(
            dimension_semantics=("parallel",), vmem_limit_bytes=VMEM_LIMIT),
        name="out_ffn",
    )(x2, o_gla, o_swa, ga, gb, *consts)


def _rope_tables(seq):
    half = SWA_HEAD_DIM // 2
    inv_freq = ROPE_THETA ** (-jnp.arange(half, dtype=F32) / half)
    ang = jnp.arange(seq).astype(F32)[:, None] * inv_freq[None, :]
    cos, sin = jnp.cos(ang), jnp.sin(ang)
    return jnp.concatenate([cos, cos], axis=-1), jnp.concatenate([-sin, sin], axis=-1)


def _gate_up(up, lr_col0):
    per_head = up.reshape(GLA_GATE_RANK, GLA_HEADS, GLA_DK).transpose(1, 0, 2)
    pad = ((0, 0), (lr_col0, LANES - GLA_GATE_RANK - lr_col0), (0, 0))
    return jnp.pad(per_head, pad).astype(BF16)


def _layer(x2, batch, seq, norm_mix_g, w_in, up_f, bias_f, up_b, bias_b, gla_out_g, w_o_gla,
           q_g, k_g, sinks, w_o_swa, w_out, norm_ffn_g, w_ffn_in, w_ffn_out):
    split_at = [int(i) for i in np.cumsum(IN_SPLITS)[:-1]]
    (w_q, w_k, w_v, w_r, w_lrf, w_lrb, w_sq, w_sk, w_sv, w_ga, w_gb) = [
        w.astype(BF16) for w in jnp.split(w_in, split_at, axis=-1)]
    w_lr = jnp.pad(jnp.concatenate([w_lrf, w_lrb], axis=-1),
                   ((0, 0), (0, LANES - 2 * GLA_GATE_RANK)))
    cos, sin = _rope_tables(seq)
    row = lambda t: t.reshape(1, -1)
    q, k, v, r, lr, sq, sk, sv, ga, gb = _in_projection(
        x2, row(norm_mix_g), cos, sin, row(q_g), row(k_g),
        (w_q, w_k, w_v, w_r, w_lr, w_sq, w_sk, w_sv, w_ga, w_gb), seq)
    o_gla = _gla_branch(
        q, k, v, r, lr,
        _gate_up(up_f, 0), bias_f.reshape(GLA_HEADS, 1, GLA_DK),
        _gate_up(up_b, GLA_GATE_RANK), bias_b.reshape(GLA_HEADS, 1, GLA_DK),
        row(gla_out_g), batch, seq)
    o_swa = _swa_branch(sq, sk, sv, sinks, batch, seq)
    return _out_ffn(x2, o_gla, o_swa, ga, gb, w_o_gla.astype(BF16), w_o_swa.astype(BF16),
                    w_out.astype(BF16), row(norm_ffn_g), w_ffn_in[:, :D_FF].astype(BF16),
                    w_ffn_in[:, D_FF:].astype(BF16), w_ffn_out.astype(BF16))


def kernel(x, norm_mix_g, w_in, gla_gate_up_fwd, gla_gate_bias_fwd, gla_gate_up_bwd,
           gla_gate_bias_bwd, gla_out_norm_g, w_o_gla, swa_q_norm_g, swa_k_norm_g,
           swa_sinks, w_o_swa, w_out, norm_ffn_g, w_ffn_in, w_ffn_out):
    batch, seq, d = x.shape
    x2 = x.reshape(batch * seq, d)
    for l in range(w_in.shape[0]):
        x2 = _layer(x2, batch, seq, norm_mix_g[l], w_in[l], gla_gate_up_fwd[l],
                    gla_gate_bias_fwd[l], gla_gate_up_bwd[l], gla_gate_bias_bwd[l],
                    gla_out_norm_g[l], w_o_gla[l], swa_q_norm_g[l], swa_k_norm_g[l],
                    swa_sinks[l], w_o_swa[l], w_out[l], norm_ffn_g[l], w_ffn_in[l],
                    w_ffn_out[l])
    return x2.reshape(batch, seq, d)
```

```python
import math

import jax
import jax.numpy as jnp
import numpy as np
from jax import lax
from jax.experimental import pallas as pl
from jax.experimental.pallas import tpu as pltpu

F32 = jnp.float32
BF16 = jnp.bfloat16

D_MODEL = 1024
NORM_EPS = 1e-6
GLA_HEADS = 4
GLA_DK = 128
GLA_DV = 256
GLA_QK_W = GLA_HEADS * GLA_DK
GLA_V_W = GLA_HEADS * GLA_DV
GLA_GATE_RANK = 16
GLA_GATE_NORMALIZER = 16.0
GLA_LOG_GATE_MIN = -0.5
GLA_CHUNK = 128
GLA_UNROLL = 4
SWA_HEADS = 8
SWA_KV_HEADS = 2
SWA_GROUP = SWA_HEADS // SWA_KV_HEADS
SWA_HEAD_DIM = 128
SWA_Q_W = SWA_HEADS * SWA_HEAD_DIM
SWA_KV_W = SWA_KV_HEADS * SWA_HEAD_DIM
SWA_BLOCK = 128
ROPE_THETA = 10000.0
D_FF = 2816
IN_SPLITS = (GLA_QK_W, GLA_QK_W, GLA_V_W, GLA_V_W, GLA_GATE_RANK, GLA_GATE_RANK,
             SWA_Q_W, SWA_KV_W, SWA_KV_W, D_MODEL, D_MODEL)

LANES = 128
MXU_COLS = 256
MASK_VALUE = -1e30
LOG2_E = math.log2(math.e)
VMEM_LIMIT = 56 * 1024 * 1024

IN_TILE_M = 512
SWA_TILE_Q = 512
SWA_UNROLL = 2
OUT_TILE_M = 512
FFN_CHUNK = D_FF // 2


def _const_spec(shape):
    zeros = (0,) * len(shape)
    return pl.BlockSpec(shape, lambda *_: zeros, pipeline_mode=pl.Buffered(1))


def _dot(a, b):
    return jnp.dot(a, b, preferred_element_type=F32)


def _dot_nt(a, b):
    return lax.dot_general(a, b, (((1,), (1,)), ((), ())), preferred_element_type=F32)


def _dot_tn(a, b):
    return lax.dot_general(a, b, (((0,), (0,)), ((), ())), preferred_element_type=F32)


def _rms(x, gain):
    ms = jnp.mean(x * x, axis=-1, keepdims=True)
    return x * lax.rsqrt(ms + NORM_EPS) * gain


def _inproj_kernel(x_ref, g_ref, cos_ref, sin_ref, qg_ref, kg_ref,
                   wlr_ref, up_ref, gbias_ref, wq_ref, wk_ref, wv_ref, wr_ref,
                   wsq_ref, wsk_ref, wsv_ref, wga_ref, wgb_ref,
                   qd_ref, ki_ref, kt_ref, tot_ref, v_ref, r_ref, sq_ref, sk_ref, sv_ref,
                   ga_ref, gb_ref):
    tm = x_ref.shape[0]
    c = GLA_CHUNK
    h = _rms(x_ref[...], g_ref[...]).astype(BF16)

    lr = _dot(h, wlr_ref[...]).astype(BF16)
    z = _dot(lr, up_ref[...]) + gbias_ref[...]
    log2_g = jnp.maximum(
        (jnp.minimum(z, 0.0) * LOG2_E - jnp.log2(1.0 + jnp.exp2(jnp.abs(z) * -LOG2_E)))
        * (1.0 / GLA_GATE_NORMALIZER), GLA_LOG_GATE_MIN * LOG2_E)
    hi = log2_g.astype(BF16)
    lo = (log2_g - hi.astype(F32)).astype(BF16)

    cos = cos_ref[...]
    sin = sin_ref[...]

    def norm_rope(t, gain, scale):
        y = _rms(t, gain)
        rot = pltpu.roll(y, SWA_HEAD_DIM // 2, axis=1)
        return ((y * cos + rot * sin) * scale).astype(BF16)

    hd = SWA_HEAD_DIM
    for w_ref, o_ref, gain_ref, scale in ((wsk_ref, sk_ref, kg_ref, 1.0),
                                          (wsq_ref, sq_ref, qg_ref, hd ** -0.5 * LOG2_E)):
        for c0 in range(0, o_ref.shape[1], MXU_COLS):
            t = _dot(h, w_ref[:, c0:c0 + MXU_COLS])
            for c1 in range(0, MXU_COLS, hd):
                o_ref[:, c0 + c1:c0 + c1 + hd] = norm_rope(t[:, c1:c1 + hd], gain_ref[...], scale)
    ga_ref[...] = jax.nn.sigmoid(_dot(h, wga_ref[...])).astype(BF16)
    gb_ref[...] = jax.nn.sigmoid(_dot(h, wgb_ref[...])).astype(BF16)
    q = _dot(h, wq_ref[...]) * (GLA_DK ** -0.5)
    k = _dot(h, wk_ref[...])

    ri = lax.broadcasted_iota(jnp.int32, (c, 2 * c), 0)
    ci = lax.broadcasted_iota(jnp.int32, (c, 2 * c), 1) % c
    tri = (jnp.where(ri >= ci, 1.0, 0.0).astype(BF16), jnp.where(ci >= ri, 1.0, 0.0).astype(BF16))
    total_row = (c - 1, 0)
    for t in range(tm // c):
        rows = slice(t * c, (t + 1) * c)
        totals = []
        for d in range(2):
            cols = slice(d * GLA_QK_W, (d + 1) * GLA_QK_W)
            b = _dot(tri[d], jnp.concatenate([hi[rows, cols], lo[rows, cols]], axis=0))
            decay = jnp.exp2(b)
            total = decay[total_row[d]:total_row[d] + 1, :]
            totals.append(total)
            q_dec = (q[rows, :] * decay).astype(BF16)
            k_inv = k[rows, :] * jnp.exp2(-b)
            k_tail = (k_inv * total).astype(BF16)
            k_inv = k_inv.astype(BF16)
            for hh in range(GLA_HEADS):
                src = slice(hh * GLA_DK, (hh + 1) * GLA_DK)
                dst = slice((2 * hh + d) * GLA_DK, (2 * hh + d + 1) * GLA_DK)
                qd_ref[rows, dst] = q_dec[:, src]
                ki_ref[rows, dst] = k_inv[:, src]
                kt_ref[rows, dst] = k_tail[:, src]
        tot_ref[t] = jnp.concatenate(
            [totals[d][:, hh * GLA_DK:(hh + 1) * GLA_DK]
             for hh in range(GLA_HEADS) for d in range(2)], axis=1)

    sv_ref[...] = _dot(h, wsv_ref[...]).astype(BF16)
    v_ref[...] = _dot(h, wv_ref[...]).astype(BF16)
    r_ref[...] = _dot(h, wr_ref[...]).astype(BF16)


def _in_projection(x2, norm_g, cos, sin, q_g, k_g, weights, seq):
    m = x2.shape[0]
    tm = IN_TILE_M
    pos_blocks = seq // tm
    row = lambda w: pl.BlockSpec((tm, w), lambda i: (i, 0))
    pos = pl.BlockSpec((tm, SWA_HEAD_DIM), lambda i: (i % pos_blocks, 0))
    widths = (2 * GLA_QK_W, 2 * GLA_QK_W, 2 * GLA_QK_W, None, GLA_V_W, GLA_V_W,
              SWA_Q_W, SWA_KV_W, SWA_KV_W, D_MODEL, D_MODEL)
    tot_spec = pl.BlockSpec((tm // GLA_CHUNK, 1, 2 * GLA_QK_W), lambda i: (i, 0, 0))
    tot_shape = jax.ShapeDtypeStruct((m // GLA_CHUNK, 1, 2 * GLA_QK_W), F32)
    return pl.pallas_call(
        _inproj_kernel,
        grid=(m // tm,),
        in_specs=[row(D_MODEL), _const_spec((1, D_MODEL)), pos, pos,
                  _const_spec((1, SWA_HEAD_DIM)), _const_spec((1, SWA_HEAD_DIM))]
                 + [_const_spec(w.shape) for w in weights],
        out_specs=[tot_spec if w is None else row(w) for w in widths],
        out_shape=[tot_shape if w is None else jax.ShapeDtypeStruct((m, w), BF16)
                   for w in widths],
        compiler_params=pltpu.CompilerParams(
            dimension_semantics=("parallel",), vmem_limit_bytes=VMEM_LIMIT),
        name="in_projection",
    )(x2, norm_g, cos, sin, q_g, k_g, *weights)


def _gla_kernel(qd_ref, ki_ref, kt_ref, tot_ref, v_ref, r_ref, og_ref,
                o_ref, sf_ref, kvb_ref, state_ref):
    c = GLA_CHUNK
    dk = GLA_DK
    u = GLA_UNROLL
    n_chunks = v_ref.shape[0] // c
    lower_incl = (lax.broadcasted_iota(jnp.int32, (c, c), 0)
                  >= lax.broadcasted_iota(jnp.int32, (c, c), 1))
    rows_of = lambda n: pl.ds(pl.multiple_of(n * c, c), c)

    state_ref[...] = jnp.zeros_like(state_ref)

    def sweep_right(i, carry):
        chunks = [i * u + t for t in range(u)]
        kv_t = [_dot_tn(v_ref[rows_of(n), :], kt_ref[rows_of(n), :]) for n in chunks]
        s = state_ref[...]
        for n, kv in zip(chunks, kv_t):
            sf_ref[n] = s.astype(BF16)
            kvb_ref[n] = kv[:, dk:]
            s = s * tot_ref[n][:, :dk] + kv[:, :dk]
        state_ref[...] = s
        return carry

    lax.fori_loop(0, n_chunks // u, sweep_right, 0)

    state_ref[...] = jnp.zeros_like(state_ref)

    def sweep_left(i, carry):
        chunks = [n_chunks - 1 - (i * u + t) for t in range(u)]
        rows = [rows_of(n) for n in chunks]
        qd = [qd_ref[r, :] for r in rows]
        ki = [ki_ref[r, :] for r in rows]
        v = [v_ref[r, :] for r in rows]
        a = [jnp.where(lower_incl, _dot_nt(q[:, :dk], k[:, :dk]),
                       _dot_nt(q[:, dk:], k[:, dk:])).astype(BF16) for q, k in zip(qd, ki)]
        intra = [_dot(aj, vj) for aj, vj in zip(a, v)]
        s = state_ref[...]
        outs = []
        for j, n in enumerate(chunks):
            both = jnp.concatenate([sf_ref[n], s.astype(BF16)], axis=1)
            outs.append(intra[j] + _dot_nt(qd[j], both))
            s = s * tot_ref[n][:, dk:] + kvb_ref[n]
        state_ref[...] = s
        finals = [(_rms(o, og_ref[...]) * jax.nn.silu(r_ref[r, :].astype(F32))).astype(BF16)
                  for r, o in zip(rows, outs)]
        for r, o in zip(rows, finals):
            o_ref[r, :] = o
        return carry

    lax.fori_loop(0, n_chunks // u, sweep_left, 0)


def _gla_branch(qd, ki, kt, tot, v, r, out_g, batch, seq):
    n_chunks = seq // GLA_CHUNK
    qk = pl.BlockSpec((seq, 2 * GLA_DK), lambda b, h: (b, h))
    vv = pl.BlockSpec((seq, GLA_DV), lambda b, h: (b, h))
    return pl.pallas_call(
        _gla_kernel,
        grid=(batch, GLA_HEADS),
        in_specs=[qk, qk, qk, pl.BlockSpec((n_chunks, 1, 2 * GLA_DK), lambda b, h: (b, 0, h)),
                  vv, vv, _const_spec((1, GLA_DV))],
        out_specs=vv,
        out_shape=jax.ShapeDtypeStruct((batch * seq, GLA_V_W), BF16),
        scratch_shapes=[pltpu.VMEM((n_chunks, GLA_DV, GLA_DK), BF16),
                        pltpu.VMEM((n_chunks, GLA_DV, GLA_DK), F32),
                        pltpu.VMEM((GLA_DV, GLA_DK), F32)],
        compiler_params=pltpu.CompilerParams(
            dimension_semantics=("parallel", "parallel"), vmem_limit_bytes=VMEM_LIMIT),
        name="gla_branch",
    )(qd, ki, kt, tot, v, r, out_g)


def _swa_kernel(sink_ref, q_ref, k_ref, v_ref, o_ref):
    blk = SWA_BLOCK
    hd = SWA_HEAD_DIM
    tile_blocks = q_ref.shape[0] // blk
    seq_blocks = k_ref.shape[0] // blk
    tile = pl.program_id(1)
    rows_g = SWA_GROUP * blk
    qi = lax.broadcasted_iota(jnp.int32, (rows_g, blk), 0) % blk
    kj = lax.broadcasted_iota(jnp.int32, (rows_g, blk), 1)
    ones = jnp.ones((blk, hd), BF16)
    head_cols = lambda hh: slice(hh * hd, (hh + 1) * hd)

    def step(i, carry):
        jobs = [(i * SWA_UNROLL + t, hk) for t in range(SWA_UNROLL) for hk in range(SWA_KV_HEADS)]
        q4, sink, kv_rows, masks = [], [], [], []
        for t, hk in jobs:
            n = tile * tile_blocks + t
            heads = [hk * SWA_GROUP + g for g in range(SWA_GROUP)]
            q_rows = pl.ds(pl.multiple_of(t * blk, blk), blk)
            q4.append(jnp.concatenate([q_ref[q_rows, head_cols(hh)] for hh in heads], axis=0))
            sink.append(jnp.concatenate(
                [jnp.full((blk, 1), sink_ref[hh] * LOG2_E, F32) for hh in heads], axis=0))
            kv_rows.append([pl.ds(pl.multiple_of(nn * blk, blk), blk) for nn in
                            (jnp.maximum(n - 1, 0), n, jnp.minimum(n + 1, seq_blocks - 1))])
            masks.append([kj >= qi + jnp.where(n > 0, 0, blk), None,
                          kj <= qi - jnp.where(n < seq_blocks - 1, 0, blk)])
        s = [[_dot_nt(q, k_ref[r, head_cols(hk)]) for r in rows]
             for q, rows, (_, hk) in zip(q4, kv_rows, jobs)]
        s = [[sj if mask is None else jnp.where(mask, sj, MASK_VALUE)
              for sj, mask in zip(sb, mb)] for sb, mb in zip(s, masks)]
        m = [jnp.maximum(jnp.maximum(jnp.maximum(sb[0], sb[1]), sb[2]).max(axis=-1, keepdims=True),
                         sk) for sb, sk in zip(s, sink)]
        p = [[jnp.exp2(sj - mb).astype(BF16) for sj in sb] for sb, mb in zip(s, m)]
        pv = [sum(_dot(pj, jnp.concatenate([v_ref[r, head_cols(hk)], ones], axis=1))
                  for pj, r in zip(pb, rows))
              for pb, rows, (_, hk) in zip(p, kv_rows, jobs)]
        for acc, mb, sk, (t, hk) in zip(pv, m, sink, jobs):
            o = acc[:, :hd] / (acc[:, hd:] + jnp.exp2(sk - mb))
            q_rows = pl.ds(pl.multiple_of(t * blk, blk), blk)
            for g in range(SWA_GROUP):
                o_ref[q_rows, head_cols(hk * SWA_GROUP + g)] = (
                    o[g * blk:(g + 1) * blk, :].astype(BF16))
        return carry

    lax.fori_loop(0, tile_blocks // SWA_UNROLL, step, 0)


def _swa_branch(sq, sk, sv, sinks, batch, seq):
    tq = SWA_TILE_Q
    tiles = seq // tq
    return pl.pallas_call(
        _swa_kernel,
        grid=(batch, tiles),
        in_specs=[pl.BlockSpec(memory_space=pltpu.SMEM),
                  pl.BlockSpec((tq, SWA_Q_W), lambda b, t: (b * tiles + t, 0)),
                  pl.BlockSpec((seq, SWA_KV_W), lambda b, t: (b, 0)),
                  pl.BlockSpec((seq, SWA_KV_W), lambda b, t: (b, 0))],
        out_specs=pl.BlockSpec((tq, SWA_Q_W), lambda b, t: (b * tiles + t, 0)),
        out_shape=jax.ShapeDtypeStruct((batch * seq, SWA_Q_W), BF16),
        compiler_params=pltpu.CompilerParams(
            dimension_semantics=("parallel", "parallel"), vmem_limit_bytes=VMEM_LIMIT),
        name="swa_branch",
    )(sinks, sq, sk, sv)


def _out_ffn_kernel(x_ref, og_ref, os_ref, ga_ref, gb_ref, wog_ref, wos_ref, wout_ref, g2_ref,
                    wfg_ref, wfu_ref, wfo_ref, o_ref):
    y_gla = _dot(og_ref[...], wog_ref[...])
    y_swa = _dot(os_ref[...], wos_ref[...])
    merged = ga_ref[...].astype(F32) * y_gla + gb_ref[...].astype(F32) * y_swa
    x1 = x_ref[...] + _dot(merged.astype(BF16), wout_ref[...])
    h2 = _rms(x1, g2_ref[...]).astype(BF16)
    acc = x1
    for c0 in range(0, D_FF, FFN_CHUNK):
        cols = slice(c0, c0 + FFN_CHUNK)
        gate = _dot(h2, wfg_ref[:, cols])
        up = _dot(h2, wfu_ref[:, cols])
        act = (jax.nn.silu(gate) * up).astype(BF16)
        acc = acc + _dot(act, wfo_ref[cols, :])
    o_ref[...] = acc


def _out_ffn(x2, o_gla, o_swa, ga, gb, w_o_gla, w_o_swa, w_out, norm_g, w_fg, w_fu, w_fo):
    m = x2.shape[0]
    tm = OUT_TILE_M
    row = pl.BlockSpec((tm, D_MODEL), lambda i: (i, 0))
    consts = (w_o_gla, w_o_swa, w_out, norm_g, w_fg, w_fu, w_fo)
    return pl.pallas_call(
        _out_ffn_kernel,
        grid=(m // tm,),
        in_specs=[row] * 5 + [_const_spec(w.shape) for w in consts],
        out_specs=row,
        out_shape=jax.ShapeDtypeStruct((m, D_MODEL), F32),
        compiler_params=pltpu.CompilerParams(
            dimension_semantics=("parallel",), vmem_limit_bytes=VMEM_LIMIT),
        name="out_ffn",
    )(x2, o_gla, o_swa, ga, gb, *consts)


def _rope_tables(seq):
    half = SWA_HEAD_DIM // 2
    inv_freq = ROPE_THETA ** (-jnp.arange(half, dtype=F32) / half)
    ang = jnp.arange(seq).astype(F32)[:, None] * inv_freq[None, :]
    cos, sin = jnp.cos(ang), jnp.sin(ang)
    return jnp.concatenate([cos, cos], axis=-1), jnp.concatenate([-sin, sin], axis=-1)


def _gate_up(up_f, up_b):
    zeros = jnp.zeros_like(up_f)
    top = jnp.concatenate([up_f, zeros], axis=1)
    bot = jnp.concatenate([zeros, up_b], axis=1)
    pad = jnp.zeros((LANES - 2 * GLA_GATE_RANK, 2 * GLA_QK_W), up_f.dtype)
    return jnp.concatenate([top, bot, pad], axis=0).astype(BF16)


def _layer(x2, batch, seq, norm_mix_g, w_in, up_f, bias_f, up_b, bias_b, gla_out_g, w_o_gla,
           q_g, k_g, sinks, w_o_swa, w_out, norm_ffn_g, w_ffn_in, w_ffn_out):
    split_at = [int(i) for i in np.cumsum(IN_SPLITS)[:-1]]
    (w_q, w_k, w_v, w_r, w_lrf, w_lrb, w_sq, w_sk, w_sv, w_ga, w_gb) = [
        w.astype(BF16) for w in jnp.split(w_in, split_at, axis=-1)]
    w_lr = jnp.pad(jnp.concatenate([w_lrf, w_lrb], axis=-1),
                   ((0, 0), (0, LANES - 2 * GLA_GATE_RANK)))
    cos, sin = _rope_tables(seq)
    row = lambda t: t.reshape(1, -1)
    qd, ki, kt, tot, v, r, sq, sk, sv, ga, gb = _in_projection(
        x2, row(norm_mix_g), cos, sin, row(q_g), row(k_g),
        (w_lr, _gate_up(up_f, up_b), row(jnp.concatenate([bias_f, bias_b])),
         w_q, w_k, w_v, w_r, w_sq, w_sk, w_sv, w_ga, w_gb), seq)
    o_gla = _gla_branch(qd, ki, kt, tot, v, r, row(gla_out_g), batch, seq)
    o_swa = _swa_branch(sq, sk, sv, sinks, batch, seq)
    return _out_ffn(x2, o_gla, o_swa, ga, gb, w_o_gla.astype(BF16), w_o_swa.astype(BF16),
                    w_out.astype(BF16), row(norm_ffn_g), w_ffn_in[:, :D_FF].astype(BF16),
                    w_ffn_in[:, D_FF:].astype(BF16), w_ffn_out.astype(BF16))


def kernel(x, norm_mix_g, w_in, gla_gate_up_fwd, gla_gate_bias_fwd, gla_gate_up_bwd,
           gla_gate_bias_bwd, gla_out_norm_g, w_o_gla, swa_q_norm_g, swa_k_norm_g,
           swa_sinks, w_o_swa, w_out, norm_ffn_g, w_ffn_in, w_ffn_out):
    batch, seq, d = x.shape
    x2 = x.reshape(batch * seq, d)
    for l in range(w_in.shape[0]):
        x2 = _layer(x2, batch, seq, norm_mix_g[l], w_in[l], gla_gate_up_fwd[l],
                    gla_gate_bias_fwd[l], gla_gate_up_bwd[l], gla_gate_bias_bwd[l],
                    gla_out_norm_g[l], w_o_gla[l], swa_q_norm_g[l], swa_k_norm_g[l],
                    swa_sinks[l], w_o_swa[l], w_out[l], norm_ffn_g[l], w_ffn_in[l],
                    w_ffn_out[l])
    return x2.reshape(batch, seq, d)
```

```python
import math

import jax
import jax.numpy as jnp
import numpy as np
from jax import lax
from jax.experimental import pallas as pl
from jax.experimental.pallas import tpu as pltpu

F32 = jnp.float32
BF16 = jnp.bfloat16

D_MODEL = 1024
NORM_EPS = 1e-6
GLA_HEADS = 4
GLA_DK = 128
GLA_DV = 256
GLA_QK_W = GLA_HEADS * GLA_DK
GLA_V_W = GLA_HEADS * GLA_DV
GLA_GATE_RANK = 16
GLA_GATE_NORMALIZER = 16.0
GLA_LOG_GATE_MIN = -0.5
GLA_CHUNK = 128
GLA_UNROLL = 4
SWA_HEADS = 8
SWA_KV_HEADS = 2
SWA_GROUP = SWA_HEADS // SWA_KV_HEADS
SWA_HEAD_DIM = 128
SWA_Q_W = SWA_HEADS * SWA_HEAD_DIM
SWA_KV_W = SWA_KV_HEADS * SWA_HEAD_DIM
SWA_BLOCK = 128
ROPE_THETA = 10000.0
D_FF = 2816
IN_SPLITS = (GLA_QK_W, GLA_QK_W, GLA_V_W, GLA_V_W, GLA_GATE_RANK, GLA_GATE_RANK,
             SWA_Q_W, SWA_KV_W, SWA_KV_W, D_MODEL, D_MODEL)

LANES = 128
MXU_COLS = 256
MASK_VALUE = -1e30
LOG2_E = math.log2(math.e)
VMEM_LIMIT = 56 * 1024 * 1024

IN_TILE_M = 512
SWA_TILE_Q = 512
SWA_UNROLL = 2
OUT_TILE_M = 512
FFN_CHUNK = D_FF // 2


def _const_spec(shape):
    zeros = (0,) * len(shape)
    return pl.BlockSpec(shape, lambda *_: zeros, pipeline_mode=pl.Buffered(1))


def _dot(a, b):
    return jnp.dot(a, b, preferred_element_type=F32)


def _dot_nt(a, b):
    return lax.dot_general(a, b, (((1,), (1,)), ((), ())), preferred_element_type=F32)


def _dot_tn(a, b):
    return lax.dot_general(a, b, (((0,), (0,)), ((), ())), preferred_element_type=F32)


def _rms(x, gain):
    ms = jnp.mean(x * x, axis=-1, keepdims=True)
    return x * lax.rsqrt(ms + NORM_EPS) * gain


def _inproj_kernel(x_ref, g_ref, cos_ref, sin_ref, qg_ref, kg_ref,
                   wlr_ref, up_ref, gbias_ref, wq_ref, wk_ref, wv_ref, wr_ref,
                   wsq_ref, wsk_ref, wsv_ref, wga_ref, wgb_ref,
                   qd_ref, ki_ref, kt_ref, tot_ref, v_ref, r_ref, sq_ref, sk_ref, sv_ref,
                   ga_ref, gb_ref):
    tm = x_ref.shape[0]
    c = GLA_CHUNK
    h = _rms(x_ref[...], g_ref[...]).astype(BF16)
    cos = cos_ref[...]
    sin = sin_ref[...]

    def norm_rope(t, gain, scale):
        y = _rms(t, gain)
        rot = pltpu.roll(y, SWA_HEAD_DIM // 2, axis=1)
        return ((y * cos + rot * sin) * scale).astype(BF16)

    hd = SWA_HEAD_DIM
    for w_ref, o_ref, gain_ref, scale in ((wsk_ref, sk_ref, kg_ref, 1.0),
                                          (wsq_ref, sq_ref, qg_ref, hd ** -0.5 * LOG2_E)):
        for c0 in range(0, o_ref.shape[1], MXU_COLS):
            t = _dot(h, w_ref[:, c0:c0 + MXU_COLS])
            for c1 in range(0, MXU_COLS, hd):
                o_ref[:, c0 + c1:c0 + c1 + hd] = norm_rope(t[:, c1:c1 + hd], gain_ref[...], scale)

    lr = _dot(h, wlr_ref[...]).astype(BF16)
    ga_ref[...] = jax.nn.sigmoid(_dot(h, wga_ref[...])).astype(BF16)
    z = _dot(lr, up_ref[...]) + gbias_ref[...]
    log2_g = jnp.maximum(
        (jnp.minimum(z, 0.0) * LOG2_E - jnp.log2(1.0 + jnp.exp2(jnp.abs(z) * -LOG2_E)))
        * (1.0 / GLA_GATE_NORMALIZER), GLA_LOG_GATE_MIN * LOG2_E)
    hi = log2_g.astype(BF16)
    lo = (log2_g - hi.astype(F32)).astype(BF16)

    gb_ref[...] = jax.nn.sigmoid(_dot(h, wgb_ref[...])).astype(BF16)
    q = _dot(h, wq_ref[...]) * (GLA_DK ** -0.5)
    k = _dot(h, wk_ref[...])

    ri = lax.broadcasted_iota(jnp.int32, (c, 2 * c), 0)
    ci = lax.broadcasted_iota(jnp.int32, (c, 2 * c), 1) % c
    tri = (jnp.where(ri >= ci, 1.0, 0.0).astype(BF16), jnp.where(ci >= ri, 1.0, 0.0).astype(BF16))
    total_row = (c - 1, 0)
    for t in range(tm // c):
        rows = slice(t * c, (t + 1) * c)
        totals = []
        for d in range(2):
            cols = slice(d * GLA_QK_W, (d + 1) * GLA_QK_W)
            b = _dot(tri[d], jnp.concatenate([hi[rows, cols], lo[rows, cols]], axis=0))
            decay = jnp.exp2(b)
            total = decay[total_row[d]:total_row[d] + 1, :]
            totals.append(total)
            q_dec = (q[rows, :] * decay).astype(BF16)
            k_inv = k[rows, :] * jnp.exp2(-b)
            k_tail = (k_inv * total).astype(BF16)
            k_inv = k_inv.astype(BF16)
            for hh in range(GLA_HEADS):
                src = slice(hh * GLA_DK, (hh + 1) * GLA_DK)
                dst = slice((2 * hh + d) * GLA_DK, (2 * hh + d + 1) * GLA_DK)
                qd_ref[rows, dst] = q_dec[:, src]
                ki_ref[rows, dst] = k_inv[:, src]
                kt_ref[rows, dst] = k_tail[:, src]
        tot_ref[t] = jnp.concatenate(
            [totals[d][:, hh * GLA_DK:(hh + 1) * GLA_DK]
             for hh in range(GLA_HEADS) for d in range(2)], axis=1)

    sv_ref[...] = _dot(h, wsv_ref[...]).astype(BF16)
    v_ref[...] = _dot(h, wv_ref[...]).astype(BF16)
    r_ref[...] = _dot(h, wr_ref[...]).astype(BF16)


def _in_projection(x2, norm_g, cos, sin, q_g, k_g, weights, seq):
    m = x2.shape[0]
    tm = IN_TILE_M
    pos_blocks = seq // tm
    row = lambda w: pl.BlockSpec((tm, w), lambda i: (i, 0))
    pos = pl.BlockSpec((tm, SWA_HEAD_DIM), lambda i: (i % pos_blocks, 0))
    widths = (2 * GLA_QK_W, 2 * GLA_QK_W, 2 * GLA_QK_W, None, GLA_V_W, GLA_V_W,
              SWA_Q_W, SWA_KV_W, SWA_KV_W, D_MODEL, D_MODEL)
    tot_spec = pl.BlockSpec((tm // GLA_CHUNK, 1, 2 * GLA_QK_W), lambda i: (i, 0, 0))
    tot_shape = jax.ShapeDtypeStruct((m // GLA_CHUNK, 1, 2 * GLA_QK_W), F32)
    return pl.pallas_call(
        _inproj_kernel,
        grid=(m // tm,),
        in_specs=[row(D_MODEL), _const_spec((1, D_MODEL)), pos, pos,
                  _const_spec((1, SWA_HEAD_DIM)), _const_spec((1, SWA_HEAD_DIM))]
                 + [_const_spec(w.shape) for w in weights],
        out_specs=[tot_spec if w is None else row(w) for w in widths],
        out_shape=[tot_shape if w is None else jax.ShapeDtypeStruct((m, w), BF16)
                   for w in widths],
        compiler_params=pltpu.CompilerParams(
            dimension_semantics=("parallel",), vmem_limit_bytes=VMEM_LIMIT),
        name="in_projection",
    )(x2, norm_g, cos, sin, q_g, k_g, *weights)


def _gla_kernel(qd_ref, ki_ref, kt_ref, tot_ref, v_ref, r_ref, og_ref,
                o_ref, sf_ref, kvb_ref, state_ref, oraw_ref):
    c = GLA_CHUNK
    dk = GLA_DK
    u = GLA_UNROLL
    n_chunks = v_ref.shape[0] // c
    lower_incl = (lax.broadcasted_iota(jnp.int32, (c, c), 0)
                  >= lax.broadcasted_iota(jnp.int32, (c, c), 1))
    rows_of = lambda n: pl.ds(n * c if isinstance(n, int) else pl.multiple_of(n * c, c), c)

    state_ref[...] = jnp.zeros_like(state_ref)

    def sweep_right(i, carry):
        chunks = [i * u + t for t in range(u)]
        kv_t = [_dot_tn(v_ref[rows_of(n), :], kt_ref[rows_of(n), :]) for n in chunks]
        s = state_ref[...]
        for n, kv in zip(chunks, kv_t):
            sf_ref[n] = s.astype(BF16)
            kvb_ref[n] = kv[:, dk:]
            s = s * tot_ref[n][:, :dk] + kv[:, :dk]
        state_ref[...] = s
        return carry

    lax.fori_loop(0, n_chunks // u, sweep_right, 0)

    state_ref[...] = jnp.zeros_like(state_ref)
    trips = n_chunks // u
    chunks_of = lambda i: [n_chunks - 1 - (i * u + t) for t in range(u)]

    def matmuls(i):
        chunks = chunks_of(i)
        rows = [rows_of(n) for n in chunks]
        qd = [qd_ref[r, :] for r in rows]
        ki = [ki_ref[r, :] for r in rows]
        a = [jnp.where(lower_incl, _dot_nt(q[:, :dk], k[:, :dk]),
                       _dot_nt(q[:, dk:], k[:, dk:])).astype(BF16) for q, k in zip(qd, ki)]
        s = state_ref[...]
        for j, n in enumerate(chunks):
            both = jnp.concatenate([sf_ref[n], s.astype(BF16)], axis=1)
            oraw_ref[i % 2, j] = _dot(a[j], v_ref[rows[j], :]) + _dot_nt(qd[j], both)
            s = s * tot_ref[n][:, dk:] + kvb_ref[n]
        state_ref[...] = s

    def tail(i):
        rows = [rows_of(n) for n in chunks_of(i)]
        finals = [(_rms(oraw_ref[i % 2, j], og_ref[...])
                   * jax.nn.silu(r_ref[r, :].astype(F32))).astype(BF16)
                  for j, r in enumerate(rows)]
        for r, o in zip(rows, finals):
            o_ref[r, :] = o

    def sweep_left(i, carry):
        tail(i - 1)
        matmuls(i)
        return carry

    matmuls(0)
    lax.fori_loop(1, trips, sweep_left, 0)
    tail(trips - 1)


def _gla_branch(qd, ki, kt, tot, v, r, out_g, batch, seq):
    n_chunks = seq // GLA_CHUNK
    qk = pl.BlockSpec((seq, 2 * GLA_DK), lambda b, h: (b, h))
    vv = pl.BlockSpec((seq, GLA_DV), lambda b, h: (b, h))
    return pl.pallas_call(
        _gla_kernel,
        grid=(batch, GLA_HEADS),
        in_specs=[qk, qk, qk, pl.BlockSpec((n_chunks, 1, 2 * GLA_DK), lambda b, h: (b, 0, h)),
                  vv, vv, _const_spec((1, GLA_DV))],
        out_specs=vv,
        out_shape=jax.ShapeDtypeStruct((batch * seq, GLA_V_W), BF16),
        scratch_shapes=[pltpu.VMEM((n_chunks, GLA_DV, GLA_DK), BF16),
                        pltpu.VMEM((n_chunks, GLA_DV, GLA_DK), F32),
                        pltpu.VMEM((GLA_DV, GLA_DK), F32),
                        pltpu.VMEM((2, GLA_UNROLL, GLA_CHUNK, GLA_DV), F32)],
        compiler_params=pltpu.CompilerParams(
            dimension_semantics=("parallel", "parallel"), vmem_limit_bytes=VMEM_LIMIT),
        name="gla_branch",
    )(qd, ki, kt, tot, v, r, out_g)


def _swa_kernel(sink_ref, q_ref, k_ref, v_ref, o_ref):
    blk = SWA_BLOCK
    hd = SWA_HEAD_DIM
    tile_blocks = q_ref.shape[0] // blk
    seq_blocks = k_ref.shape[0] // blk
    tile = pl.program_id(1)
    rows_g = SWA_GROUP * blk
    qi = lax.broadcasted_iota(jnp.int32, (rows_g, blk), 0) % blk
    kj = lax.broadcasted_iota(jnp.int32, (rows_g, blk), 1)
    ones = jnp.ones((3 * blk, hd), BF16)
    head_cols = lambda hh: slice(hh * hd, (hh + 1) * hd)

    def step(i, carry):
        jobs = [(i * SWA_UNROLL + t, hk) for t in range(SWA_UNROLL) for hk in range(SWA_KV_HEADS)]
        q4, kv_rows, masks = [], [], []
        for t, hk in jobs:
            n = tile * tile_blocks + t
            q_rows = pl.ds(pl.multiple_of(t * blk, blk), blk)
            q4.append(jnp.concatenate(
                [q_ref[q_rows, head_cols(hk * SWA_GROUP + g)] for g in range(SWA_GROUP)], axis=0))
            kv_rows.append([pl.ds(pl.multiple_of(nn * blk, blk), blk) for nn in
                            (jnp.maximum(n - 1, 0), n, jnp.minimum(n + 1, seq_blocks - 1))])
            masks.append([kj >= qi + jnp.where(n > 0, 0, blk), None,
                          kj <= qi - jnp.where(n < seq_blocks - 1, 0, blk)])
        s = [[_dot_nt(q, k_ref[r, head_cols(hk)]) for r in rows]
             for q, rows, (_, hk) in zip(q4, kv_rows, jobs)]
        s = [[sj if mask is None else jnp.where(mask, sj, MASK_VALUE)
              for sj, mask in zip(sb, mb)] for sb, mb in zip(s, masks)]
        row_max = [jnp.broadcast_to(
            jnp.maximum(jnp.maximum(sb[0], sb[1]), sb[2]).max(axis=-1, keepdims=True),
            (rows_g, blk)) for sb in s]
        m, sink_w = [], []
        for rm, (_, hk) in zip(row_max, jobs):
            sinks = [sink_ref[hk * SWA_GROUP + g] * LOG2_E for g in range(SWA_GROUP)]
            mg = [jnp.maximum(rm[g * blk:(g + 1) * blk], sk) for g, sk in enumerate(sinks)]
            m.append(jnp.concatenate(mg, axis=0))
            sink_w.append(jnp.concatenate([jnp.exp2(sk - x) for sk, x in zip(sinks, mg)], axis=0))
        p = [jnp.concatenate([jnp.exp2(sj - mb).astype(BF16) for sj in sb], axis=1)
             for sb, mb in zip(s, m)]
        pv = [_dot(pj, jnp.concatenate(
                  [jnp.concatenate([v_ref[r, head_cols(hk)] for r in rows], axis=0), ones], axis=1))
              for pj, rows, (_, hk) in zip(p, kv_rows, jobs)]
        for acc, sw, (t, hk) in zip(pv, sink_w, jobs):
            o = acc[:, :hd] / (acc[:, hd:] + sw)
            q_rows = pl.ds(pl.multiple_of(t * blk, blk), blk)
            for g in range(SWA_GROUP):
                o_ref[q_rows, head_cols(hk * SWA_GROUP + g)] = (
                    o[g * blk:(g + 1) * blk, :].astype(BF16))
        return carry

    lax.fori_loop(0, tile_blocks // SWA_UNROLL, step, 0)


def _swa_branch(sq, sk, sv, sinks, batch, seq):
    tq = SWA_TILE_Q
    tiles = seq // tq
    return pl.pallas_call(
        _swa_kernel,
        grid=(batch, tiles),
        in_specs=[pl.BlockSpec(memory_space=pltpu.SMEM),
                  pl.BlockSpec((tq, SWA_Q_W), lambda b, t: (b * tiles + t, 0)),
                  pl.BlockSpec((seq, SWA_KV_W), lambda b, t: (b, 0)),
                  pl.BlockSpec((seq, SWA_KV_W), lambda b, t: (b, 0))],
        out_specs=pl.BlockSpec((tq, SWA_Q_W), lambda b, t: (b * tiles + t, 0)),
        out_shape=jax.ShapeDtypeStruct((batch * seq, SWA_Q_W), BF16),
        compiler_params=pltpu.CompilerParams(
            dimension_semantics=("parallel", "parallel"), vmem_limit_bytes=VMEM_LIMIT),
        name="swa_branch",
    )(sinks, sq, sk, sv)


def _out_ffn_kernel(x_ref, og_ref, os_ref, ga_ref, gb_ref, wog_ref, wos_ref, wout_ref, g2_ref,
                    wfg_ref, wfu_ref, wfo_ref, o_ref):
    y_gla = _dot(og_ref[...], wog_ref[...])
    y_swa = _dot(os_ref[...], wos_ref[...])
    merged = ga_ref[...].astype(F32) * y_gla + gb_ref[...].astype(F32) * y_swa
    x1 = x_ref[...] + _dot(merged.astype(BF16), wout_ref[...])
    h2 = _rms(x1, g2_ref[...]).astype(BF16)
    acc = x1
    for c0 in range(0, D_FF, FFN_CHUNK):
        cols = slice(c0, c0 + FFN_CHUNK)
        gate = _dot(h2, wfg_ref[:, cols])
        up = _dot(h2, wfu_ref[:, cols])
        act = (jax.nn.silu(gate) * up).astype(BF16)
        acc = acc + _dot(act, wfo_ref[cols, :])
    o_ref[...] = acc


def _out_ffn(x2, o_gla, o_swa, ga, gb, w_o_gla, w_o_swa, w_out, norm_g, w_fg, w_fu, w_fo):
    m = x2.shape[0]
    tm = OUT_TILE_M
    row = pl.BlockSpec((tm, D_MODEL), lambda i: (i, 0))
    consts = (w_o_gla, w_o_swa, w_out, norm_g, w_fg, w_fu, w_fo)
    return pl.pallas_call(
        _out_ffn_kernel,
        grid=(m // tm,),
        in_specs=[row] * 5 + [_const_spec(w.shape) for w in consts],
        out_specs=row,
        out_shape=jax.ShapeDtypeStruct((m, D_MODEL), F32),
        compiler_params=pltpu.CompilerParams(
            dimension_semantics=("parallel",), vmem_limit_bytes=VMEM_LIMIT),
        name="out_ffn",
    )(x2, o_gla, o_swa, ga, gb, *consts)


def _rope_tables(seq):
    half = SWA_HEAD_DIM // 2
    inv_freq = ROPE_THETA ** (-jnp.arange(half, dtype=F32) / half)
    ang = jnp.arange(seq).astype(F32)[:, None] * inv_freq[None, :]
    cos, sin = jnp.cos(ang), jnp.sin(ang)
    return jnp.concatenate([cos, cos], axis=-1), jnp.concatenate([-sin, sin], axis=-1)


def _gate_up(up_f, up_b):
    zeros = jnp.zeros_like(up_f)
    top = jnp.concatenate([up_f, zeros], axis=1)
    bot = jnp.concatenate([zeros, up_b], axis=1)
    pad = jnp.zeros((LANES - 2 * GLA_GATE_RANK, 2 * GLA_QK_W), up_f.dtype)
    return jnp.concatenate([top, bot, pad], axis=0).astype(BF16)


def _layer(x2, batch, seq, norm_mix_g, w_in, up_f, bias_f, up_b, bias_b, gla_out_g, w_o_gla,
           q_g, k_g, sinks, w_o_swa, w_out, norm_ffn_g, w_ffn_in, w_ffn_out):
    split_at = [int(i) for i in np.cumsum(IN_SPLITS)[:-1]]
    (w_q, w_k, w_v, w_r, w_lrf, w_lrb, w_sq, w_sk, w_sv, w_ga, w_gb) = [
        w.astype(BF16) for w in jnp.split(w_in, split_at, axis=-1)]
    w_lr = jnp.pad(jnp.concatenate([w_lrf, w_lrb], axis=-1),
                   ((0, 0), (0, LANES - 2 * GLA_GATE_RANK)))
    cos, sin = _rope_tables(seq)
    row = lambda t: t.reshape(1, -1)
    qd, ki, kt, tot, v, r, sq, sk, sv, ga, gb = _in_projection(
        x2, row(norm_mix_g), cos, sin, row(q_g), row(k_g),
        (w_lr, _gate_up(up_f, up_b), row(jnp.concatenate([bias_f, bias_b])),
         w_q, w_k, w_v, w_r, w_sq, w_sk, w_sv, w_ga, w_gb), seq)
    o_gla = _gla_branch(qd, ki, kt, tot, v, r, row(gla_out_g), batch, seq)
    o_swa = _swa_branch(sq, sk, sv, sinks, batch, seq)
    return _out_ffn(x2, o_gla, o_swa, ga, gb, w_o_gla.astype(BF16), w_o_swa.astype(BF16),
                    w_out.astype(BF16), row(norm_ffn_g), w_ffn_in[:, :D_FF].astype(BF16),
                    w_ffn_in[:, D_FF:].astype(BF16), w_ffn_out.astype(BF16))


def kernel(x, norm_mix_g, w_in, gla_gate_up_fwd, gla_gate_bias_fwd, gla_gate_up_bwd,
           gla_gate_bias_bwd, gla_out_norm_g, w_o_gla, swa_q_norm_g, swa_k_norm_g,
           swa_sinks, w_o_swa, w_out, norm_ffn_g, w_ffn_in, w_ffn_out):
    batch, seq, d = x.shape
    x2 = x.reshape(batch * seq, d)
    for l in range(w_in.shape[0]):
        x2 = _layer(x2, batch, seq, norm_mix_g[l], w_in[l], gla_gate_up_fwd[l],
                    gla_gate_bias_fwd[l], gla_gate_up_bwd[l], gla_gate_bias_bwd[l],
                    gla_out_norm_g[l], w_o_gla[l], swa_q_norm_g[l], swa_k_norm_g[l],
                    swa_sinks[l], w_o_swa[l], w_out[l], norm_ffn_g[l], w_ffn_in[l],
                    w_ffn_out[l])
    return x2.reshape(batch, seq, d)
```

```python
import math

import jax
import jax.numpy as jnp
import numpy as np
from jax import lax
from jax.experimental import pallas as pl
from jax.experimental.pallas import tpu as pltpu

F32 = jnp.float32
BF16 = jnp.bfloat16

D_MODEL = 1024
NORM_EPS = 1e-6
GLA_HEADS = 4
GLA_DK = 128
GLA_DV = 256
GLA_QK_W = GLA_HEADS * GLA_DK
GLA_V_W = GLA_HEADS * GLA_DV
GLA_GATE_RANK = 16
GLA_GATE_NORMALIZER = 16.0
GLA_LOG_GATE_MIN = -0.5
GLA_CHUNK = 128
GLA_UNROLL = 4
SWA_HEADS = 8
SWA_KV_HEADS = 2
SWA_GROUP = SWA_HEADS // SWA_KV_HEADS
SWA_HEAD_DIM = 128
SWA_Q_W = SWA_HEADS * SWA_HEAD_DIM
SWA_KV_W = SWA_KV_HEADS * SWA_HEAD_DIM
SWA_BLOCK = 128
ROPE_THETA = 10000.0
D_FF = 2816
IN_SPLITS = (GLA_QK_W, GLA_QK_W, GLA_V_W, GLA_V_W, GLA_GATE_RANK, GLA_GATE_RANK,
             SWA_Q_W, SWA_KV_W, SWA_KV_W, D_MODEL, D_MODEL)

LANES = 128
W_IN_OFFSETS = tuple(int(o) for o in np.cumsum(
    (0, GLA_QK_W, GLA_QK_W, GLA_V_W, GLA_V_W, SWA_Q_W, SWA_KV_W, SWA_KV_W, D_MODEL, D_MODEL,
     LANES)))
MXU_COLS = 256
MASK_VALUE = -1e30
LOG2_E = math.log2(math.e)
VMEM_LIMIT = 56 * 1024 * 1024

IN_TILE_M = 512
SWA_TILE_Q = 512
SWA_UNROLL = 2
OUT_TILE_M = 512
FFN_CHUNK = D_FF // 2


def _const_spec(shape):
    zeros = (0,) * len(shape)
    return pl.BlockSpec(shape, lambda *_: zeros, pipeline_mode=pl.Buffered(1))


def _dot(a, b):
    return jnp.dot(a, b, preferred_element_type=F32)


def _dot_nt(a, b):
    return lax.dot_general(a, b, (((1,), (1,)), ((), ())), preferred_element_type=F32)


def _dot_tn(a, b):
    return lax.dot_general(a, b, (((0,), (0,)), ((), ())), preferred_element_type=F32)


def _rms(x, gain):
    ms = jnp.mean(x * x, axis=-1, keepdims=True)
    return x * lax.rsqrt(ms + NORM_EPS) * gain


def _inproj_kernel(x_ref, g_ref, cos_ref, sin_ref, qg_ref, kg_ref, w_ref, up_ref, gbias_ref,
                   qd_ref, ki_ref, kt_ref, tot_ref, v_ref, r_ref, sq_ref, sk_ref, sv_ref,
                   ga_ref, gb_ref):
    tm = x_ref.shape[0]
    c = GLA_CHUNK
    (wq_ref, wk_ref, wv_ref, wr_ref, wsq_ref, wsk_ref, wsv_ref, wga_ref, wgb_ref,
     wlr_ref) = [w_ref.at[:, a:b] for a, b in zip(W_IN_OFFSETS[:-1], W_IN_OFFSETS[1:])]
    h = _rms(x_ref[...], g_ref[...]).astype(BF16)
    cos = cos_ref[...]
    sin = sin_ref[...]

    def norm_rope(t, gain, scale):
        y = _rms(t, gain)
        rot = pltpu.roll(y, SWA_HEAD_DIM // 2, axis=1)
        return ((y * cos + rot * sin) * scale).astype(BF16)

    hd = SWA_HEAD_DIM
    for w_ref, o_ref, gain_ref, scale in ((wsk_ref, sk_ref, kg_ref, 1.0),
                                          (wsq_ref, sq_ref, qg_ref, hd ** -0.5 * LOG2_E)):
        for c0 in range(0, o_ref.shape[1], MXU_COLS):
            t = _dot(h, w_ref[:, c0:c0 + MXU_COLS])
            for c1 in range(0, MXU_COLS, hd):
                o_ref[:, c0 + c1:c0 + c1 + hd] = norm_rope(t[:, c1:c1 + hd], gain_ref[...], scale)

    lr = _dot(h, wlr_ref[...]).astype(BF16)
    ga_ref[...] = jax.nn.sigmoid(_dot(h, wga_ref[...])).astype(BF16)
    z = _dot(lr, up_ref[...]) + gbias_ref[...]
    log2_g = jnp.maximum(
        (jnp.minimum(z, 0.0) * LOG2_E - jnp.log2(1.0 + jnp.exp2(jnp.abs(z) * -LOG2_E)))
        * (1.0 / GLA_GATE_NORMALIZER), GLA_LOG_GATE_MIN * LOG2_E)
    hi = log2_g.astype(BF16)
    lo = (log2_g - hi.astype(F32)).astype(BF16)

    gb_ref[...] = jax.nn.sigmoid(_dot(h, wgb_ref[...])).astype(BF16)
    q = _dot(h, wq_ref[...]) * (GLA_DK ** -0.5)
    k = _dot(h, wk_ref[...])

    ri = lax.broadcasted_iota(jnp.int32, (c, 2 * c), 0)
    ci = lax.broadcasted_iota(jnp.int32, (c, 2 * c), 1) % c
    tri = (jnp.where(ri >= ci, 1.0, 0.0).astype(BF16), jnp.where(ci >= ri, 1.0, 0.0).astype(BF16))
    total_row = (c - 1, 0)
    for t in range(tm // c):
        rows = slice(t * c, (t + 1) * c)
        totals = []
        for d in range(2):
            cols = slice(d * GLA_QK_W, (d + 1) * GLA_QK_W)
            b = _dot(tri[d], jnp.concatenate([hi[rows, cols], lo[rows, cols]], axis=0))
            decay = jnp.exp2(b)
            total = decay[total_row[d]:total_row[d] + 1, :]
            totals.append(total)
            q_dec = (q[rows, :] * decay).astype(BF16)
            k_inv = k[rows, :] * jnp.exp2(-b)
            k_tail = (k_inv * total).astype(BF16)
            k_inv = k_inv.astype(BF16)
            for hh in range(GLA_HEADS):
                src = slice(hh * GLA_DK, (hh + 1) * GLA_DK)
                dst = slice(d * GLA_DK, (d + 1) * GLA_DK)
                qd_ref[hh, rows, dst] = q_dec[:, src]
                ki_ref[hh, rows, dst] = k_inv[:, src]
                kt_ref[hh, rows, dst] = k_tail[:, src]
        tot_ref[t] = jnp.concatenate(
            [totals[d][:, hh * GLA_DK:(hh + 1) * GLA_DK]
             for hh in range(GLA_HEADS) for d in range(2)], axis=1)

    sv_ref[...] = _dot(h, wsv_ref[...]).astype(BF16)
    for w_ref, o_ref in ((wv_ref, v_ref), (wr_ref, r_ref)):
        for hh in range(GLA_HEADS):
            o_ref[hh] = _dot(h, w_ref[:, hh * GLA_DV:(hh + 1) * GLA_DV]).astype(BF16)


def _in_projection(x2, norm_g, cos, sin, q_g, k_g, weights, seq):
    m = x2.shape[0]
    tm = IN_TILE_M
    pos_blocks = seq // tm
    row = lambda w: pl.BlockSpec((tm, w), lambda i: (i, 0))
    pos = pl.BlockSpec((tm, SWA_HEAD_DIM), lambda i: (i % pos_blocks, 0))
    head_major = (pl.BlockSpec((GLA_HEADS, tm, GLA_DV), lambda i: (0, i, 0)),
                  jax.ShapeDtypeStruct((GLA_HEADS, m, GLA_DV), BF16))
    tot = (pl.BlockSpec((tm // GLA_CHUNK, 1, 2 * GLA_QK_W), lambda i: (i, 0, 0)),
           jax.ShapeDtypeStruct((m // GLA_CHUNK, 1, 2 * GLA_QK_W), F32))
    flat = lambda w: (row(w), jax.ShapeDtypeStruct((m, w), BF16))
    outs = (head_major, head_major, head_major, tot, head_major, head_major,
            flat(SWA_Q_W), flat(SWA_KV_W), flat(SWA_KV_W), flat(D_MODEL), flat(D_MODEL))
    return pl.pallas_call(
        _inproj_kernel,
        grid=(m // tm,),
        in_specs=[row(D_MODEL), _const_spec((1, D_MODEL)), pos, pos,
                  _const_spec((1, SWA_HEAD_DIM)), _const_spec((1, SWA_HEAD_DIM))]
                 + [_const_spec(w.shape) for w in weights],
        out_specs=[spec for spec, _ in outs],
        out_shape=[shape for _, shape in outs],
        compiler_params=pltpu.CompilerParams(
            dimension_semantics=("parallel",), vmem_limit_bytes=VMEM_LIMIT),
        name="in_projection",
    )(x2, norm_g, cos, sin, q_g, k_g, *weights)


def _gla_kernel(qd_ref, ki_ref, kt_ref, tot_ref, v_ref, r_ref, og_ref,
                o_ref, sf_ref, kvb_ref, state_ref, oraw_ref):
    c = GLA_CHUNK
    dk = GLA_DK
    u = GLA_UNROLL
    n_chunks = v_ref.shape[0] // c
    lower_incl = (lax.broadcasted_iota(jnp.int32, (c, c), 0)
                  >= lax.broadcasted_iota(jnp.int32, (c, c), 1))
    rows_of = lambda n: pl.ds(n * c if isinstance(n, int) else pl.multiple_of(n * c, c), c)

    state_ref[...] = jnp.zeros_like(state_ref)

    def sweep_right(i, carry):
        chunks = [i * u + t for t in range(u)]
        kv_t = [_dot_tn(v_ref[rows_of(n), :], kt_ref[rows_of(n), :]) for n in chunks]
        s = state_ref[...]
        for n, kv in zip(chunks, kv_t):
            sf_ref[n] = s.astype(BF16)
            kvb_ref[n] = kv[:, dk:]
            s = s * tot_ref[n][:, :dk] + kv[:, :dk]
        state_ref[...] = s
        return carry

    lax.fori_loop(0, n_chunks // u, sweep_right, 0)

    state_ref[...] = jnp.zeros_like(state_ref)
    trips = n_chunks // u
    chunks_of = lambda i: [n_chunks - 1 - (i * u + t) for t in range(u)]

    def matmuls(i):
        chunks = chunks_of(i)
        rows = [rows_of(n) for n in chunks]
        qd = [qd_ref[r, :] for r in rows]
        ki = [ki_ref[r, :] for r in rows]
        a = [jnp.where(lower_incl, _dot_nt(q[:, :dk], k[:, :dk]),
                       _dot_nt(q[:, dk:], k[:, dk:])).astype(BF16) for q, k in zip(qd, ki)]
        s = state_ref[...]
        for j, n in enumerate(chunks):
            both = jnp.concatenate([sf_ref[n], s.astype(BF16)], axis=1)
            oraw_ref[i % 2, j] = _dot(a[j], v_ref[rows[j], :]) + _dot_nt(qd[j], both)
            s = s * tot_ref[n][:, dk:] + kvb_ref[n]
        state_ref[...] = s

    def tail(i):
        rows = [rows_of(n) for n in chunks_of(i)]
        finals = [(_rms(oraw_ref[i % 2, j], og_ref[...])
                   * jax.nn.silu(r_ref[r, :].astype(F32))).astype(BF16)
                  for j, r in enumerate(rows)]
        for r, o in zip(rows, finals):
            o_ref[r, :] = o

    def sweep_left(i, carry):
        tail(i - 1)
        matmuls(i)
        return carry

    matmuls(0)
    lax.fori_loop(1, trips, sweep_left, 0)
    tail(trips - 1)


def _gla_branch(qd, ki, kt, tot, v, r, out_g, batch, seq):
    n_chunks = seq // GLA_CHUNK
    assert 2 * GLA_DK == GLA_DV
    vv = pl.BlockSpec((None, seq, GLA_DV), lambda b, h: (h, b, 0))
    return pl.pallas_call(
        _gla_kernel,
        grid=(batch, GLA_HEADS),
        in_specs=[vv, vv, vv, pl.BlockSpec((n_chunks, 1, 2 * GLA_DK), lambda b, h: (b, 0, h)),
                  vv, vv, _const_spec((1, GLA_DV))],
        out_specs=vv,
        out_shape=jax.ShapeDtypeStruct((GLA_HEADS, batch * seq, GLA_DV), BF16),
        scratch_shapes=[pltpu.VMEM((n_chunks, GLA_DV, GLA_DK), BF16),
                        pltpu.VMEM((n_chunks, GLA_DV, GLA_DK), F32),
                        pltpu.VMEM((GLA_DV, GLA_DK), F32),
                        pltpu.VMEM((2, GLA_UNROLL, GLA_CHUNK, GLA_DV), F32)],
        compiler_params=pltpu.CompilerParams(
            dimension_semantics=("parallel", "parallel"), vmem_limit_bytes=VMEM_LIMIT),
        name="gla_branch",
    )(qd, ki, kt, tot, v, r, out_g)


def _swa_kernel(sink_ref, q_ref, k_ref, v_ref, o_ref):
    blk = SWA_BLOCK
    hd = SWA_HEAD_DIM
    tile_blocks = q_ref.shape[0] // blk
    seq_blocks = k_ref.shape[0] // blk
    tile = pl.program_id(1)
    rows_g = SWA_GROUP * blk
    qi = lax.broadcasted_iota(jnp.int32, (rows_g, blk), 0) % blk
    kj = lax.broadcasted_iota(jnp.int32, (rows_g, blk), 1)
    ones = jnp.ones((3 * blk, hd), BF16)
    head_cols = lambda hh: slice(hh * hd, (hh + 1) * hd)

    def step(i, carry):
        jobs = [(i * SWA_UNROLL + t, hk) for t in range(SWA_UNROLL) for hk in range(SWA_KV_HEADS)]
        q4, kv_rows, masks = [], [], []
        for t, hk in jobs:
            n = tile * tile_blocks + t
            q_rows = pl.ds(pl.multiple_of(t * blk, blk), blk)
            q4.append(jnp.concatenate(
                [q_ref[q_rows, head_cols(hk * SWA_GROUP + g)] for g in range(SWA_GROUP)], axis=0))
            kv_rows.append([pl.ds(pl.multiple_of(nn * blk, blk), blk) for nn in
                            (jnp.maximum(n - 1, 0), n, jnp.minimum(n + 1, seq_blocks - 1))])
            masks.append([kj >= qi + jnp.where(n > 0, 0, blk), None,
                          kj <= qi - jnp.where(n < seq_blocks - 1, 0, blk)])
        s = [[_dot_nt(q, k_ref[r, head_cols(hk)]) for r in rows]
             for q, rows, (_, hk) in zip(q4, kv_rows, jobs)]
        s = [[sj if mask is None else jnp.where(mask, sj, MASK_VALUE)
              for sj, mask in zip(sb, mb)] for sb, mb in zip(s, masks)]
        row_max = [jnp.broadcast_to(
            jnp.maximum(jnp.maximum(sb[0], sb[1]), sb[2]).max(axis=-1, keepdims=True),
            (rows_g, blk)) for sb in s]
        m, sink_w = [], []
        for rm, (_, hk) in zip(row_max, jobs):
            sinks = [sink_ref[hk * SWA_GROUP + g] * LOG2_E for g in range(SWA_GROUP)]
            mg = [jnp.maximum(rm[g * blk:(g + 1) * blk], sk) for g, sk in enumerate(sinks)]
            m.append(jnp.concatenate(mg, axis=0))
            sink_w.append(jnp.concatenate([jnp.exp2(sk - x) for sk, x in zip(sinks, mg)], axis=0))
        p = [jnp.concatenate([jnp.exp2(sj - mb).astype(BF16) for sj in sb], axis=1)
             for sb, mb in zip(s, m)]
        pv = [_dot(pj, jnp.concatenate(
                  [jnp.concatenate([v_ref[r, head_cols(hk)] for r in rows], axis=0), ones], axis=1))
              for pj, rows, (_, hk) in zip(p, kv_rows, jobs)]
        for acc, sw, (t, hk) in zip(pv, sink_w, jobs):
            o = acc[:, :hd] / (acc[:, hd:] + sw)
            q_rows = pl.ds(pl.multiple_of(t * blk, blk), blk)
            for g in range(SWA_GROUP):
                o_ref[q_rows, head_cols(hk * SWA_GROUP + g)] = (
                    o[g * blk:(g + 1) * blk, :].astype(BF16))
        return carry

    lax.fori_loop(0, tile_blocks // SWA_UNROLL, step, 0)


def _swa_branch(sq, sk, sv, sinks, batch, seq):
    tq = SWA_TILE_Q
    tiles = seq // tq
    return pl.pallas_call(
        _swa_kernel,
        grid=(batch, tiles),
        in_specs=[pl.BlockSpec(memory_space=pltpu.SMEM),
                  pl.BlockSpec((tq, SWA_Q_W), lambda b, t: (b * tiles + t, 0)),
                  pl.BlockSpec((seq, SWA_KV_W), lambda b, t: (b, 0)),
                  pl.BlockSpec((seq, SWA_KV_W), lambda b, t: (b, 0))],
        out_specs=pl.BlockSpec((tq, SWA_Q_W), lambda b, t: (b * tiles + t, 0)),
        out_shape=jax.ShapeDtypeStruct((batch * seq, SWA_Q_W), BF16),
        compiler_params=pltpu.CompilerParams(
            dimension_semantics=("parallel", "parallel"), vmem_limit_bytes=VMEM_LIMIT),
        name="swa_branch",
    )(sinks, sq, sk, sv)


def _out_ffn_kernel(x_ref, og_ref, os_ref, ga_ref, gb_ref, wog_ref, wos_ref, wout_ref, g2_ref,
                    wfi_ref, wfo_ref, o_ref):
    o_gla = jnp.concatenate([og_ref[hh] for hh in range(GLA_HEADS)], axis=1)
    y_gla = _dot(o_gla, wog_ref[...])
    y_swa = _dot(os_ref[...], wos_ref[...])
    merged = ga_ref[...].astype(F32) * y_gla + gb_ref[...].astype(F32) * y_swa
    x1 = x_ref[...] + _dot(merged.astype(BF16), wout_ref[...])
    h2 = _rms(x1, g2_ref[...]).astype(BF16)
    acc = x1
    for c0 in range(0, D_FF, FFN_CHUNK):
        cols = slice(c0, c0 + FFN_CHUNK)
        gate = _dot(h2, wfi_ref[:, cols])
        up = _dot(h2, wfi_ref[:, D_FF + c0:D_FF + c0 + FFN_CHUNK])
        act = (jax.nn.silu(gate) * up).astype(BF16)
        acc = acc + _dot(act, wfo_ref[cols, :])
    o_ref[...] = acc


def _out_ffn(x2, o_gla, o_swa, ga, gb, w_o_gla, w_o_swa, w_out, norm_g, w_fi, w_fo):
    m = x2.shape[0]
    tm = OUT_TILE_M
    row = pl.BlockSpec((tm, D_MODEL), lambda i: (i, 0))
    consts = (w_o_gla, w_o_swa, w_out, norm_g, w_fi, w_fo)
    return pl.pallas_call(
        _out_ffn_kernel,
        grid=(m // tm,),
        in_specs=[row, pl.BlockSpec((GLA_HEADS, tm, GLA_DV), lambda i: (0, i, 0))] + [row] * 3
                 + [_const_spec(w.shape) for w in consts],
        out_specs=row,
        out_shape=jax.ShapeDtypeStruct((m, D_MODEL), F32),
        compiler_params=pltpu.CompilerParams(
            dimension_semantics=("parallel",), vmem_limit_bytes=VMEM_LIMIT),
        name="out_ffn",
    )(x2, o_gla, o_swa, ga, gb, *consts)


def _rope_tables(seq):
    half = SWA_HEAD_DIM // 2
    inv_freq = ROPE_THETA ** (-np.arange(half, dtype=np.float64) / half)
    ang = np.arange(seq, dtype=np.float64)[:, None] * inv_freq[None, :]
    cos, sin = np.cos(ang), np.sin(ang)
    return (jnp.asarray(np.concatenate([cos, cos], axis=-1), F32),
            jnp.asarray(np.concatenate([-sin, sin], axis=-1), F32))


def _gate_up(up_f, up_b):
    zeros = jnp.zeros_like(up_f)
    top = jnp.concatenate([up_f, zeros], axis=1)
    bot = jnp.concatenate([zeros, up_b], axis=1)
    pad = jnp.zeros((LANES - 2 * GLA_GATE_RANK, 2 * GLA_QK_W), up_f.dtype)
    return jnp.concatenate([top, bot, pad], axis=0).astype(BF16)


def _layer(x2, batch, seq, norm_mix_g, w_in, up_f, bias_f, up_b, bias_b, gla_out_g, w_o_gla,
           q_g, k_g, sinks, w_o_swa, w_out, norm_ffn_g, w_ffn_in, w_ffn_out):
    lr0 = sum(IN_SPLITS[:4])
    lr1 = lr0 + 2 * GLA_GATE_RANK
    w = jnp.concatenate(
        [w_in[:, :lr0], w_in[:, lr1:], w_in[:, lr0:lr1],
         jnp.zeros((D_MODEL, LANES - 2 * GLA_GATE_RANK), w_in.dtype)], axis=1).astype(BF16)
    cos, sin = _rope_tables(seq)
    row = lambda t: t.reshape(1, -1)
    qd, ki, kt, tot, v, r, sq, sk, sv, ga, gb = _in_projection(
        x2, row(norm_mix_g), cos, sin, row(q_g), row(k_g),
        (w, _gate_up(up_f, up_b), row(jnp.concatenate([bias_f, bias_b]))), seq)
    o_gla = _gla_branch(qd, ki, kt, tot, v, r, row(gla_out_g), batch, seq)
    o_swa = _swa_branch(sq, sk, sv, sinks, batch, seq)
    return _out_ffn(x2, o_gla, o_swa, ga, gb, w_o_gla.astype(BF16), w_o_swa.astype(BF16),
                    w_out.astype(BF16), row(norm_ffn_g), w_ffn_in.astype(BF16),
                    w_ffn_out.astype(BF16))


def kernel(x, norm_mix_g, w_in, gla_gate_up_fwd, gla_gate_bias_fwd, gla_gate_up_bwd,
           gla_gate_bias_bwd, gla_out_norm_g, w_o_gla, swa_q_norm_g, swa_k_norm_g,
           swa_sinks, w_o_swa, w_out, norm_ffn_g, w_ffn_in, w_ffn_out):
    batch, seq, d = x.shape
    x2 = x.reshape(batch * seq, d)
    for l in range(w_in.shape[0]):
        x2 = _layer(x2, batch, seq, norm_mix_g[l], w_in[l], gla_gate_up_fwd[l],
                    gla_gate_bias_fwd[l], gla_gate_up_bwd[l], gla_gate_bias_bwd[l],
                    gla_out_norm_g[l], w_o_gla[l], swa_q_norm_g[l], swa_k_norm_g[l],
                    swa_sinks[l], w_o_swa[l], w_out[l], norm_ffn_g[l], w_ffn_in[l],
                    w_ffn_out[l])
    return x2.reshape(batch, seq, d)
```

```python
import math

import jax
import jax.numpy as jnp
import numpy as np
from jax import lax
from jax.experimental import pallas as pl
from jax.experimental.pallas import tpu as pltpu

F32 = jnp.float32
BF16 = jnp.bfloat16

D_MODEL = 1024
NORM_EPS = 1e-6
GLA_HEADS = 4
GLA_DK = 128
GLA_DV = 256
GLA_QK_W = GLA_HEADS * GLA_DK
GLA_V_W = GLA_HEADS * GLA_DV
GLA_GATE_RANK = 16
GLA_GATE_NORMALIZER = 16.0
GLA_LOG_GATE_MIN = -0.5
GLA_CHUNK = 128
GLA_UNROLL = 4
SWA_HEADS = 8
SWA_KV_HEADS = 2
SWA_GROUP = SWA_HEADS // SWA_KV_HEADS
SWA_HEAD_DIM = 128
SWA_Q_W = SWA_HEADS * SWA_HEAD_DIM
SWA_KV_W = SWA_KV_HEADS * SWA_HEAD_DIM
SWA_BLOCK = 128
ROPE_THETA = 10000.0
D_FF = 2816
IN_SPLITS = (GLA_QK_W, GLA_QK_W, GLA_V_W, GLA_V_W, GLA_GATE_RANK, GLA_GATE_RANK,
             SWA_Q_W, SWA_KV_W, SWA_KV_W, D_MODEL, D_MODEL)

LANES = 128
W_IN_OFFSETS = tuple(int(o) for o in np.cumsum((0,) + IN_SPLITS))
W_IN_CODES = W_IN_OFFSETS[4]
W_IN_TAIL = W_IN_OFFSETS[6]
MXU_COLS = 256
MASK_VALUE = -1e30
LOG2_E = math.log2(math.e)
VMEM_LIMIT = 56 * 1024 * 1024

IN_TILE_M = 512
SWA_TILE_Q = 512
SWA_UNROLL = 2
OUT_TILE_M = 512
FFN_CHUNK = D_FF // 2


def _const_spec(shape):
    zeros = (0,) * len(shape)
    return pl.BlockSpec(shape, lambda *_: zeros, pipeline_mode=pl.Buffered(1))


def _dot(a, b):
    return jnp.dot(a, b, preferred_element_type=F32)


def _dot_nt(a, b):
    return lax.dot_general(a, b, (((1,), (1,)), ((), ())), preferred_element_type=F32)


def _dot_tn(a, b):
    return lax.dot_general(a, b, (((0,), (0,)), ((), ())), preferred_element_type=F32)


def _rms(x, gain):
    ms = jnp.mean(x * x, axis=-1, keepdims=True)
    return x * lax.rsqrt(ms + NORM_EPS) * gain


def _inproj_kernel(x_ref, g_ref, cos_ref, sin_ref, qg_ref, kg_ref, w_ref, up_ref, gbias_ref,
                   qd_ref, ki_ref, kt_ref, tot_ref, v_ref, r_ref, sq_ref, sk_ref, sv_ref,
                   ga_ref, gb_ref, wtail_ref):
    tm = x_ref.shape[0]
    c = GLA_CHUNK

    @pl.when(pl.program_id(0) == 0)
    def _():
        wtail_ref[...] = w_ref[:, W_IN_TAIL:]

    wq_ref, wk_ref, wv_ref, wr_ref = [
        w_ref.at[:, a:b] for a, b in zip(W_IN_OFFSETS[:4], W_IN_OFFSETS[1:5])]
    wsq_ref, wsk_ref, wsv_ref, wga_ref, wgb_ref = [
        wtail_ref.at[:, a - W_IN_TAIL:b - W_IN_TAIL]
        for a, b in zip(W_IN_OFFSETS[6:-1], W_IN_OFFSETS[7:])]
    wlr_ref = w_ref.at[:, W_IN_CODES:W_IN_CODES + LANES]
    h = _rms(x_ref[...], g_ref[...]).astype(BF16)
    cos = cos_ref[...]
    sin = sin_ref[...]

    def norm_rope(t, gain, scale):
        y = _rms(t, gain)
        rot = pltpu.roll(y, SWA_HEAD_DIM // 2, axis=1)
        return ((y * cos + rot * sin) * scale).astype(BF16)

    hd = SWA_HEAD_DIM
    for w_ref, o_ref, gain_ref, scale in ((wsk_ref, sk_ref, kg_ref, 1.0),
                                          (wsq_ref, sq_ref, qg_ref, hd ** -0.5 * LOG2_E)):
        for c0 in range(0, o_ref.shape[1], MXU_COLS):
            t = _dot(h, w_ref[:, c0:c0 + MXU_COLS])
            for c1 in range(0, MXU_COLS, hd):
                o_ref[:, c0 + c1:c0 + c1 + hd] = norm_rope(t[:, c1:c1 + hd], gain_ref[...], scale)

    lr = _dot(h, wlr_ref[...]).astype(BF16)
    ga_ref[...] = jax.nn.sigmoid(_dot(h, wga_ref[...])).astype(BF16)
    z = _dot(lr, up_ref[...]) + gbias_ref[...]
    log2_g = jnp.maximum(
        (jnp.minimum(z, 0.0) * LOG2_E - jnp.log2(1.0 + jnp.exp2(jnp.abs(z) * -LOG2_E)))
        * (1.0 / GLA_GATE_NORMALIZER), GLA_LOG_GATE_MIN * LOG2_E)
    hi = log2_g.astype(BF16)
    lo = (log2_g - hi.astype(F32)).astype(BF16)

    gb_ref[...] = jax.nn.sigmoid(_dot(h, wgb_ref[...])).astype(BF16)
    q = _dot(h, wq_ref[...]) * (GLA_DK ** -0.5)
    k = _dot(h, wk_ref[...])

    ri = lax.broadcasted_iota(jnp.int32, (c, 2 * c), 0)
    ci = lax.broadcasted_iota(jnp.int32, (c, 2 * c), 1) % c
    tri = (jnp.where(ri >= ci, 1.0, 0.0).astype(BF16), jnp.where(ci >= ri, 1.0, 0.0).astype(BF16))
    total_row = (c - 1, 0)
    for t in range(tm // c):
        rows = slice(t * c, (t + 1) * c)
        totals = []
        for d in range(2):
            cols = slice(d * GLA_QK_W, (d + 1) * GLA_QK_W)
            b = _dot(tri[d], jnp.concatenate([hi[rows, cols], lo[rows, cols]], axis=0))
            decay = jnp.exp2(b)
            total = decay[total_row[d]:total_row[d] + 1, :]
            totals.append(total)
            q_dec = (q[rows, :] * decay).astype(BF16)
            k_inv = k[rows, :] * jnp.exp2(-b)
            k_tail = (k_inv * total).astype(BF16)
            k_inv = k_inv.astype(BF16)
            for hh in range(GLA_HEADS):
                src = slice(hh * GLA_DK, (hh + 1) * GLA_DK)
                dst = slice(d * GLA_DK, (d + 1) * GLA_DK)
                qd_ref[hh, rows, dst] = q_dec[:, src]
                ki_ref[hh, rows, dst] = k_inv[:, src]
                kt_ref[hh, rows, dst] = k_tail[:, src]
        tot_ref[t] = jnp.concatenate(
            [totals[d][:, hh * GLA_DK:(hh + 1) * GLA_DK]
             for hh in range(GLA_HEADS) for d in range(2)], axis=1)

    sv_ref[...] = _dot(h, wsv_ref[...]).astype(BF16)
    for w_ref, o_ref in ((wv_ref, v_ref), (wr_ref, r_ref)):
        for hh in range(GLA_HEADS):
            o_ref[hh] = _dot(h, w_ref[:, hh * GLA_DV:(hh + 1) * GLA_DV]).astype(BF16)


def _in_projection(x2, norm_g, cos, sin, q_g, k_g, weights, seq):
    m = x2.shape[0]
    tm = IN_TILE_M
    pos_blocks = seq // tm
    row = lambda w: pl.BlockSpec((tm, w), lambda i: (i, 0))
    pos = pl.BlockSpec((tm, SWA_HEAD_DIM), lambda i: (i % pos_blocks, 0))
    head_major = (pl.BlockSpec((GLA_HEADS, tm, GLA_DV), lambda i: (0, i, 0)),
                  jax.ShapeDtypeStruct((GLA_HEADS, m, GLA_DV), BF16))
    tot = (pl.BlockSpec((tm // GLA_CHUNK, 1, 2 * GLA_QK_W), lambda i: (i, 0, 0)),
           jax.ShapeDtypeStruct((m // GLA_CHUNK, 1, 2 * GLA_QK_W), F32))
    flat = lambda w: (row(w), jax.ShapeDtypeStruct((m, w), BF16))
    outs = (head_major, head_major, head_major, tot, head_major, head_major,
            flat(SWA_Q_W), flat(SWA_KV_W), flat(SWA_KV_W), flat(D_MODEL), flat(D_MODEL))
    return pl.pallas_call(
        _inproj_kernel,
        grid=(m // tm,),
        in_specs=[row(D_MODEL), _const_spec((1, D_MODEL)), pos, pos,
                  _const_spec((1, SWA_HEAD_DIM)), _const_spec((1, SWA_HEAD_DIM))]
                 + [_const_spec(w.shape) for w in weights],
        out_specs=[spec for spec, _ in outs],
        out_shape=[shape for _, shape in outs],
        scratch_shapes=[pltpu.VMEM((D_MODEL, W_IN_OFFSETS[-1] - W_IN_TAIL), BF16)],
        compiler_params=pltpu.CompilerParams(
            dimension_semantics=("arbitrary",), vmem_limit_bytes=VMEM_LIMIT),
        name="in_projection",
    )(x2, norm_g, cos, sin, q_g, k_g, *weights)


def _gla_kernel(qd_ref, ki_ref, kt_ref, tot_ref, v_ref, r_ref, og_ref,
                o_ref, sf_ref, kvb_ref, state_ref, oraw_ref):
    c = GLA_CHUNK
    dk = GLA_DK
    u = GLA_UNROLL
    n_chunks = v_ref.shape[0] // c
    lower_incl = (lax.broadcasted_iota(jnp.int32, (c, c), 0)
                  >= lax.broadcasted_iota(jnp.int32, (c, c), 1))
    rows_of = lambda n: pl.ds(n * c if isinstance(n, int) else pl.multiple_of(n * c, c), c)

    state_ref[...] = jnp.zeros_like(state_ref)

    def sweep_right(i, carry):
        chunks = [i * u + t for t in range(u)]
        kv_t = [_dot_tn(v_ref[rows_of(n), :], kt_ref[rows_of(n), :]) for n in chunks]
        s = state_ref[...]
        for n, kv in zip(chunks, kv_t):
            sf_ref[n] = s.astype(BF16)
            kvb_ref[n] = kv[:, dk:]
            s = s * tot_ref[n][:, :dk] + kv[:, :dk]
        state_ref[...] = s
        return carry

    lax.fori_loop(0, n_chunks // u, sweep_right, 0)

    state_ref[...] = jnp.zeros_like(state_ref)
    trips = n_chunks // u
    chunks_of = lambda i: [n_chunks - 1 - (i * u + t) for t in range(u)]

    def matmuls(i):
        chunks = chunks_of(i)
        rows = [rows_of(n) for n in chunks]
        qd = [qd_ref[r, :] for r in rows]
        ki = [ki_ref[r, :] for r in rows]
        a = [jnp.where(lower_incl, _dot_nt(q[:, :dk], k[:, :dk]),
                       _dot_nt(q[:, dk:], k[:, dk:])).astype(BF16) for q, k in zip(qd, ki)]
        s = state_ref[...]
        for j, n in enumerate(chunks):
            both = jnp.concatenate([sf_ref[n], s.astype(BF16)], axis=1)
            oraw_ref[i % 2, j] = _dot(a[j], v_ref[rows[j], :]) + _dot_nt(qd[j], both)
            s = s * tot_ref[n][:, dk:] + kvb_ref[n]
        state_ref[...] = s

    def tail(i):
        rows = [rows_of(n) for n in chunks_of(i)]
        finals = [(_rms(oraw_ref[i % 2, j], og_ref[...])
                   * jax.nn.silu(r_ref[r, :].astype(F32))).astype(BF16)
                  for j, r in enumerate(rows)]
        for r, o in zip(rows, finals):
            o_ref[r, :] = o

    def sweep_left(i, carry):
        tail(i - 1)
        matmuls(i)
        return carry

    matmuls(0)
    lax.fori_loop(1, trips, sweep_left, 0)
    tail(trips - 1)


def _gla_branch(qd, ki, kt, tot, v, r, out_g, batch, seq):
    n_chunks = seq // GLA_CHUNK
    assert 2 * GLA_DK == GLA_DV
    vv = pl.BlockSpec((None, seq, GLA_DV), lambda b, h: (h, b, 0))
    return pl.pallas_call(
        _gla_kernel,
        grid=(batch, GLA_HEADS),
        in_specs=[vv, vv, vv, pl.BlockSpec((n_chunks, 1, 2 * GLA_DK), lambda b, h: (b, 0, h)),
                  vv, vv, _const_spec((1, GLA_DV))],
        out_specs=vv,
        out_shape=jax.ShapeDtypeStruct((GLA_HEADS, batch * seq, GLA_DV), BF16),
        scratch_shapes=[pltpu.VMEM((n_chunks, GLA_DV, GLA_DK), BF16),
                        pltpu.VMEM((n_chunks, GLA_DV, GLA_DK), F32),
                        pltpu.VMEM((GLA_DV, GLA_DK), F32),
                        pltpu.VMEM((2, GLA_UNROLL, GLA_CHUNK, GLA_DV), F32)],
        compiler_params=pltpu.CompilerParams(
            dimension_semantics=("parallel", "parallel"), vmem_limit_bytes=VMEM_LIMIT),
        name="gla_branch",
    )(qd, ki, kt, tot, v, r, out_g)


def _swa_kernel(sink_ref, q_ref, k_ref, v_ref, o_ref):
    blk = SWA_BLOCK
    hd = SWA_HEAD_DIM
    tile_blocks = q_ref.shape[0] // blk
    seq_blocks = k_ref.shape[0] // blk
    tile = pl.program_id(1)
    rows_g = SWA_GROUP * blk
    qi = lax.broadcasted_iota(jnp.int32, (rows_g, blk), 0) % blk
    kj = lax.broadcasted_iota(jnp.int32, (rows_g, blk), 1)
    ones = jnp.ones((3 * blk, hd), BF16)
    head_cols = lambda hh: slice(hh * hd, (hh + 1) * hd)

    def step(i, carry):
        jobs = [(i * SWA_UNROLL + t, hk) for t in range(SWA_UNROLL) for hk in range(SWA_KV_HEADS)]
        q4, kv_rows, masks = [], [], []
        for t, hk in jobs:
            n = tile * tile_blocks + t
            q_rows = pl.ds(pl.multiple_of(t * blk, blk), blk)
            q4.append(jnp.concatenate(
                [q_ref[q_rows, head_cols(hk * SWA_GROUP + g)] for g in range(SWA_GROUP)], axis=0))
            kv_rows.append([pl.ds(pl.multiple_of(nn * blk, blk), blk) for nn in
                            (jnp.maximum(n - 1, 0), n, jnp.minimum(n + 1, seq_blocks - 1))])
            masks.append([kj >= qi + jnp.where(n > 0, 0, blk), None,
                          kj <= qi - jnp.where(n < seq_blocks - 1, 0, blk)])
        s = [[_dot_nt(q, k_ref[r, head_cols(hk)]) for r in rows]
             for q, rows, (_, hk) in zip(q4, kv_rows, jobs)]
        s = [[sj if mask is None else jnp.where(mask, sj, MASK_VALUE)
              for sj, mask in zip(sb, mb)] for sb, mb in zip(s, masks)]
        row_max = [jnp.broadcast_to(
            jnp.maximum(jnp.maximum(sb[0], sb[1]), sb[2]).max(axis=-1, keepdims=True),
            (rows_g, blk)) for sb in s]
        m, sink_w = [], []
        for rm, (_, hk) in zip(row_max, jobs):
            sinks = [sink_ref[hk * SWA_GROUP + g] * LOG2_E for g in range(SWA_GROUP)]
            mg = [jnp.maximum(rm[g * blk:(g + 1) * blk], sk) for g, sk in enumerate(sinks)]
            m.append(jnp.concatenate(mg, axis=0))
            sink_w.append(jnp.concatenate([jnp.exp2(sk - x) for sk, x in zip(sinks, mg)], axis=0))
        p = [jnp.concatenate([jnp.exp2(sj - mb).astype(BF16) for sj in sb], axis=1)
             for sb, mb in zip(s, m)]
        pv = [_dot(pj, jnp.concatenate(
                  [jnp.concatenate([v_ref[r, head_cols(hk)] for r in rows], axis=0), ones], axis=1))
              for pj, rows, (_, hk) in zip(p, kv_rows, jobs)]
        for acc, sw, (t, hk) in zip(pv, sink_w, jobs):
            o = acc[:, :hd] / (acc[:, hd:] + sw)
            q_rows = pl.ds(pl.multiple_of(t * blk, blk), blk)
            for g in range(SWA_GROUP):
                o_ref[q_rows, head_cols(hk * SWA_GROUP + g)] = (
                    o[g * blk:(g + 1) * blk, :].astype(BF16))
        return carry

    lax.fori_loop(0, tile_blocks // SWA_UNROLL, step, 0)


def _swa_branch(sq, sk, sv, sinks, batch, seq):
    tq = SWA_TILE_Q
    tiles = seq // tq
    return pl.pallas_call(
        _swa_kernel,
        grid=(batch, tiles),
        in_specs=[pl.BlockSpec(memory_space=pltpu.SMEM),
                  pl.BlockSpec((tq, SWA_Q_W), lambda b, t: (b * tiles + t, 0)),
                  pl.BlockSpec((seq, SWA_KV_W), lambda b, t: (b, 0)),
                  pl.BlockSpec((seq, SWA_KV_W), lambda b, t: (b, 0))],
        out_specs=pl.BlockSpec((tq, SWA_Q_W), lambda b, t: (b * tiles + t, 0)),
        out_shape=jax.ShapeDtypeStruct((batch * seq, SWA_Q_W), BF16),
        compiler_params=pltpu.CompilerParams(
            dimension_semantics=("parallel", "parallel"), vmem_limit_bytes=VMEM_LIMIT),
        name="swa_branch",
    )(sinks, sq, sk, sv)


def _out_ffn_kernel(x_ref, og_ref, os_ref, ga_ref, gb_ref, wog_ref, wos_ref, wout_ref, g2_ref,
                    wfi_ref, wfo_ref, o_ref):
    o_gla = jnp.concatenate([og_ref[hh] for hh in range(GLA_HEADS)], axis=1)
    y_gla = _dot(o_gla, wog_ref[...])
    y_swa = _dot(os_ref[...], wos_ref[...])
    merged = ga_ref[...].astype(F32) * y_gla + gb_ref[...].astype(F32) * y_swa
    x1 = x_ref[...] + _dot(merged.astype(BF16), wout_ref[...])
    h2 = _rms(x1, g2_ref[...]).astype(BF16)
    acc = x1
    for c0 in range(0, D_FF, FFN_CHUNK):
        cols = slice(c0, c0 + FFN_CHUNK)
        gate = _dot(h2, wfi_ref[:, cols])
        up = _dot(h2, wfi_ref[:, D_FF + c0:D_FF + c0 + FFN_CHUNK])
        act = (jax.nn.silu(gate) * up).astype(BF16)
        acc = acc + _dot(act, wfo_ref[cols, :])
    o_ref[...] = acc


def _out_ffn(x2, o_gla, o_swa, ga, gb, w_o_gla, w_o_swa, w_out, norm_g, w_fi, w_fo):
    m = x2.shape[0]
    tm = OUT_TILE_M
    row = pl.BlockSpec((tm, D_MODEL), lambda i: (i, 0))
    consts = (w_o_gla, w_o_swa, w_out, norm_g, w_fi, w_fo)
    return pl.pallas_call(
        _out_ffn_kernel,
        grid=(m // tm,),
        in_specs=[row, pl.BlockSpec((GLA_HEADS, tm, GLA_DV), lambda i: (0, i, 0))] + [row] * 3
                 + [_const_spec(w.shape) for w in consts],
        out_specs=row,
        out_shape=jax.ShapeDtypeStruct((m, D_MODEL), F32),
        compiler_params=pltpu.CompilerParams(
            dimension_semantics=("parallel",), vmem_limit_bytes=VMEM_LIMIT),
        name="out_ffn",
    )(x2, o_gla, o_swa, ga, gb, *consts)


def _rope_tables(seq):
    half = SWA_HEAD_DIM // 2
    inv_freq = ROPE_THETA ** (-np.arange(half, dtype=np.float64) / half)
    ang = np.arange(seq, dtype=np.float64)[:, None] * inv_freq[None, :]
    cos, sin = np.cos(ang), np.sin(ang)
    return (jnp.asarray(np.concatenate([cos, cos], axis=-1), F32),
            jnp.asarray(np.concatenate([-sin, sin], axis=-1), F32))


def _gate_up(up_f, up_b):
    zeros = jnp.zeros_like(up_f)
    top = jnp.concatenate([up_f, zeros], axis=1)
    bot = jnp.concatenate([zeros, up_b], axis=1)
    pad = jnp.zeros((LANES - 2 * GLA_GATE_RANK, 2 * GLA_QK_W), up_f.dtype)
    return jnp.concatenate([top, bot, pad], axis=0).astype(BF16)


def _layer(x2, batch, seq, norm_mix_g, w_in, up_f, bias_f, up_b, bias_b, gla_out_g, w_o_gla,
           q_g, k_g, sinks, w_o_swa, w_out, norm_ffn_g, w_ffn_in, w_ffn_out):
    w = w_in.astype(BF16)
    cos, sin = _rope_tables(seq)
    row = lambda t: t.reshape(1, -1)
    qd, ki, kt, tot, v, r, sq, sk, sv, ga, gb = _in_projection(
        x2, row(norm_mix_g), cos, sin, row(q_g), row(k_g),
        (w, _gate_up(up_f, up_b), row(jnp.concatenate([bias_f, bias_b]))), seq)
    o_gla = _gla_branch(qd, ki, kt, tot, v, r, row(gla_out_g), batch, seq)
    o_swa = _swa_branch(sq, sk, sv, sinks, batch, seq)
    return _out_ffn(x2, o_gla, o_swa, ga, gb, w_o_gla.astype(BF16), w_o_swa.astype(BF16),
                    w_out.astype(BF16), row(norm_ffn_g), w_ffn_in.astype(BF16),
                    w_ffn_out.astype(BF16))


def kernel(x, norm_mix_g, w_in, gla_gate_up_fwd, gla_gate_bias_fwd, gla_gate_up_bwd,
           gla_gate_bias_bwd, gla_out_norm_g, w_o_gla, swa_q_norm_g, swa_k_norm_g,
           swa_sinks, w_o_swa, w_out, norm_ffn_g, w_ffn_in, w_ffn_out):
    batch, seq, d = x.shape
    x2 = x.reshape(batch * seq, d)
    for l in range(w_in.shape[0]):
        x2 = _layer(x2, batch, seq, norm_mix_g[l], w_in[l], gla_gate_up_fwd[l],
                    gla_gate_bias_fwd[l], gla_gate_up_bwd[l], gla_gate_bias_bwd[l],
                    gla_out_norm_g[l], w_o_gla[l], swa_q_norm_g[l], swa_k_norm_g[l],
                    swa_sinks[l], w_o_swa[l], w_out[l], norm_ffn_g[l], w_ffn_in[l],
                    w_ffn_out[l])
    return x2.reshape(batch, seq, d)
```

```python
import math

import jax
import jax.numpy as jnp
import numpy as np
from jax import lax
from jax.experimental import pallas as pl
from jax.experimental.pallas import tpu as pltpu

F32 = jnp.float32
BF16 = jnp.bfloat16

D_MODEL = 1024
NORM_EPS = 1e-6
GLA_HEADS = 4
GLA_DK = 128
GLA_DV = 256
GLA_QK_W = GLA_HEADS * GLA_DK
GLA_V_W = GLA_HEADS * GLA_DV
GLA_GATE_RANK = 16
GLA_GATE_NORMALIZER = 16.0
GLA_LOG_GATE_MIN = -0.5
GLA_CHUNK = 128
GLA_STATE_UNROLL = 8
GLA_OUT_UNROLL = 4
SWA_HEADS = 8
SWA_KV_HEADS = 2
SWA_GROUP = SWA_HEADS // SWA_KV_HEADS
SWA_HEAD_DIM = 128
SWA_Q_W = SWA_HEADS * SWA_HEAD_DIM
SWA_KV_W = SWA_KV_HEADS * SWA_HEAD_DIM
SWA_BLOCK = 128
ROPE_THETA = 10000.0
D_FF = 2816
IN_SPLITS = (GLA_QK_W, GLA_QK_W, GLA_V_W, GLA_V_W, GLA_GATE_RANK, GLA_GATE_RANK,
             SWA_Q_W, SWA_KV_W, SWA_KV_W, D_MODEL, D_MODEL)

LANES = 128
W_IN_OFFSETS = tuple(int(o) for o in np.cumsum((0,) + IN_SPLITS))
W_IN_CODES = W_IN_OFFSETS[4]
W_IN_TAIL = W_IN_OFFSETS[6]
MXU_COLS = 256
MASK_VALUE = -1e30
LOG2_E = math.log2(math.e)
VMEM_LIMIT = 56 * 1024 * 1024

IN_TILE_M = 512
SWA_TILE_Q = 512
SWA_UNROLL = 2
OUT_TILE_M = 512
FFN_CHUNK = D_FF // 2


def _const_spec(shape):
    zeros = (0,) * len(shape)
    return pl.BlockSpec(shape, lambda *_: zeros, pipeline_mode=pl.Buffered(1))


def _dot(a, b):
    return jnp.dot(a, b, preferred_element_type=F32)


def _dot_nt(a, b):
    return lax.dot_general(a, b, (((1,), (1,)), ((), ())), preferred_element_type=F32)


def _dot_tn(a, b):
    return lax.dot_general(a, b, (((0,), (0,)), ((), ())), preferred_element_type=F32)


def _rms(x, gain):
    ms = jnp.mean(x * x, axis=-1, keepdims=True)
    return x * lax.rsqrt(ms + NORM_EPS) * gain


def _inproj_kernel(x_ref, g_ref, cos_ref, sin_ref, qg_ref, kg_ref, w_ref, up_ref, gbias_ref,
                   qd_ref, ki_ref, kt_ref, tot_ref, v_ref, r_ref, sq_ref, sk_ref, sv_ref,
                   ga_ref, gb_ref, wtail_ref):
    tm = x_ref.shape[0]
    c = GLA_CHUNK

    @pl.when(pl.program_id(0) == 0)
    def _():
        wtail_ref[...] = w_ref[:, W_IN_TAIL:]

    wq_ref, wk_ref, wv_ref, wr_ref = [
        w_ref.at[:, a:b] for a, b in zip(W_IN_OFFSETS[:4], W_IN_OFFSETS[1:5])]
    wsq_ref, wsk_ref, wsv_ref, wga_ref, wgb_ref = [
        wtail_ref.at[:, a - W_IN_TAIL:b - W_IN_TAIL]
        for a, b in zip(W_IN_OFFSETS[6:-1], W_IN_OFFSETS[7:])]
    wlr_ref = w_ref.at[:, W_IN_CODES:W_IN_CODES + LANES]
    h = _rms(x_ref[...], g_ref[...]).astype(BF16)
    cos = cos_ref[...]
    sin = sin_ref[...]

    def norm_rope(t, gain, scale):
        y = _rms(t, gain)
        rot = pltpu.roll(y, SWA_HEAD_DIM // 2, axis=1)
        return ((y * cos + rot * sin) * scale).astype(BF16)

    hd = SWA_HEAD_DIM
    for w_ref, o_ref, gain_ref, scale in ((wsk_ref, sk_ref, kg_ref, 1.0),
                                          (wsq_ref, sq_ref, qg_ref, hd ** -0.5 * LOG2_E)):
        for c0 in range(0, o_ref.shape[1], MXU_COLS):
            t = _dot(h, w_ref[:, c0:c0 + MXU_COLS])
            for c1 in range(0, MXU_COLS, hd):
                o_ref[:, c0 + c1:c0 + c1 + hd] = norm_rope(t[:, c1:c1 + hd], gain_ref[...], scale)

    lr = _dot(h, wlr_ref[...]).astype(BF16)
    ga_ref[...] = jax.nn.sigmoid(_dot(h, wga_ref[...])).astype(BF16)
    z = _dot(lr, up_ref[...]) + gbias_ref[...]
    log2_g = jnp.maximum(
        (jnp.minimum(z, 0.0) * LOG2_E - jnp.log2(1.0 + jnp.exp2(jnp.abs(z) * -LOG2_E)))
        * (1.0 / GLA_GATE_NORMALIZER), GLA_LOG_GATE_MIN * LOG2_E)
    hi = log2_g.astype(BF16)
    lo = (log2_g - hi.astype(F32)).astype(BF16)

    gb_ref[...] = jax.nn.sigmoid(_dot(h, wgb_ref[...])).astype(BF16)
    q = _dot(h, wq_ref[...]) * (GLA_DK ** -0.5)
    k = _dot(h, wk_ref[...])

    ri = lax.broadcasted_iota(jnp.int32, (c, 2 * c), 0)
    ci = lax.broadcasted_iota(jnp.int32, (c, 2 * c), 1) % c
    tri = (jnp.where(ri >= ci, 1.0, 0.0).astype(BF16), jnp.where(ci >= ri, 1.0, 0.0).astype(BF16))
    total_row = (c - 1, 0)
    for t in range(tm // c):
        rows = slice(t * c, (t + 1) * c)
        totals = []
        for d in range(2):
            cols = slice(d * GLA_QK_W, (d + 1) * GLA_QK_W)
            b = _dot(tri[d], jnp.concatenate([hi[rows, cols], lo[rows, cols]], axis=0))
            decay = jnp.exp2(b)
            total = decay[total_row[d]:total_row[d] + 1, :]
            totals.append(total)
            q_dec = (q[rows, :] * decay).astype(BF16)
            k_inv = k[rows, :] * jnp.exp2(-b)
            k_tail = (k_inv * total).astype(BF16)
            k_inv = k_inv.astype(BF16)
            for hh in range(GLA_HEADS):
                src = slice(hh * GLA_DK, (hh + 1) * GLA_DK)
                dst = slice(d * GLA_DK, (d + 1) * GLA_DK)
                qd_ref[hh, rows, dst] = q_dec[:, src]
                ki_ref[hh, rows, dst] = k_inv[:, src]
                kt_ref[hh, rows, dst] = k_tail[:, src]
        tot_ref[t] = jnp.concatenate(
            [totals[d][:, hh * GLA_DK:(hh + 1) * GLA_DK]
             for hh in range(GLA_HEADS) for d in range(2)], axis=1)

    sv_ref[...] = _dot(h, wsv_ref[...]).astype(BF16)
    for hh in range(GLA_HEADS):
        cols = slice(hh * GLA_DV, (hh + 1) * GLA_DV)
        r_ref[hh] = jax.nn.silu(_dot(h, wr_ref[:, cols])).astype(BF16)
    for hh in range(GLA_HEADS):
        cols = slice(hh * GLA_DV, (hh + 1) * GLA_DV)
        v_ref[hh] = _dot(h, wv_ref[:, cols]).astype(BF16)


def _in_projection(x2, norm_g, cos, sin, q_g, k_g, weights, seq):
    m = x2.shape[0]
    tm = IN_TILE_M
    pos_blocks = seq // tm
    row = lambda w: pl.BlockSpec((tm, w), lambda i: (i, 0))
    pos = pl.BlockSpec((tm, SWA_HEAD_DIM), lambda i: (i % pos_blocks, 0))
    head_major = (pl.BlockSpec((GLA_HEADS, tm, GLA_DV), lambda i: (0, i, 0)),
                  jax.ShapeDtypeStruct((GLA_HEADS, m, GLA_DV), BF16))
    tot = (pl.BlockSpec((tm // GLA_CHUNK, 1, 2 * GLA_QK_W), lambda i: (i, 0, 0)),
           jax.ShapeDtypeStruct((m // GLA_CHUNK, 1, 2 * GLA_QK_W), F32))
    flat = lambda w: (row(w), jax.ShapeDtypeStruct((m, w), BF16))
    outs = (head_major, head_major, head_major, tot, head_major, head_major,
            flat(SWA_Q_W), flat(SWA_KV_W), flat(SWA_KV_W), flat(D_MODEL), flat(D_MODEL))
    return pl.pallas_call(
        _inproj_kernel,
        grid=(m // tm,),
        in_specs=[row(D_MODEL), _const_spec((1, D_MODEL)), pos, pos,
                  _const_spec((1, SWA_HEAD_DIM)), _const_spec((1, SWA_HEAD_DIM))]
                 + [_const_spec(w.shape) for w in weights],
        out_specs=[spec for spec, _ in outs],
        out_shape=[shape for _, shape in outs],
        scratch_shapes=[pltpu.VMEM((D_MODEL, W_IN_OFFSETS[-1] - W_IN_TAIL), BF16)],
        compiler_params=pltpu.CompilerParams(
            dimension_semantics=("arbitrary",), vmem_limit_bytes=VMEM_LIMIT),
        name="in_projection",
    )(x2, norm_g, cos, sin, q_g, k_g, *weights)


def _gla_kernel(qd_ref, ki_ref, kt_ref, tot_ref, v_ref, r_ref, og_ref,
                o_ref, sf_ref, kvb_ref, state_ref, oraw_ref):
    c = GLA_CHUNK
    dk = GLA_DK
    n_chunks = v_ref.shape[0] // c
    lower_incl = (lax.broadcasted_iota(jnp.int32, (c, c), 0)
                  >= lax.broadcasted_iota(jnp.int32, (c, c), 1))
    rows_of = lambda n: pl.ds(n * c if isinstance(n, int) else pl.multiple_of(n * c, c), c)

    state_ref[...] = jnp.zeros_like(state_ref)

    def sweep_right(i, carry):
        chunks = [i * GLA_STATE_UNROLL + t for t in range(GLA_STATE_UNROLL)]
        kv_t = [_dot_tn(v_ref[rows_of(n), :], kt_ref[rows_of(n), :]) for n in chunks]
        s = state_ref[...]
        for n, kv in zip(chunks, kv_t):
            sf_ref[n] = s.astype(BF16)
            kvb_ref[n] = kv[:, dk:]
            s = s * tot_ref[n][:, :dk] + kv[:, :dk]
        state_ref[...] = s
        return carry

    lax.fori_loop(0, n_chunks // GLA_STATE_UNROLL, sweep_right, 0)

    state_ref[...] = jnp.zeros_like(state_ref)
    u = GLA_OUT_UNROLL
    trips = n_chunks // u
    chunks_of = lambda i: [n_chunks - 1 - (i * u + t) for t in range(u)]

    def matmuls(i):
        chunks = chunks_of(i)
        rows = [rows_of(n) for n in chunks]
        qd = [qd_ref[r, :] for r in rows]
        ki = [ki_ref[r, :] for r in rows]
        a = [jnp.where(lower_incl, _dot_nt(q[:, :dk], k[:, :dk]),
                       _dot_nt(q[:, dk:], k[:, dk:])).astype(BF16) for q, k in zip(qd, ki)]
        s = state_ref[...]
        for j, n in enumerate(chunks):
            both = jnp.concatenate([sf_ref[n], s.astype(BF16)], axis=1)
            oraw_ref[i % 2, j] = _dot(a[j], v_ref[rows[j], :]) + _dot_nt(qd[j], both)
            s = s * tot_ref[n][:, dk:] + kvb_ref[n]
        state_ref[...] = s

    def tail(i):
        rows = [rows_of(n) for n in chunks_of(i)]
        finals = [(_rms(oraw_ref[i % 2, j], og_ref[...])
                   * r_ref[r, :].astype(F32)).astype(BF16)
                  for j, r in enumerate(rows)]
        for r, o in zip(rows, finals):
            o_ref[r, :] = o

    def sweep_left(i, carry):
        tail(i - 1)
        matmuls(i)
        return carry

    matmuls(0)
    lax.fori_loop(1, trips, sweep_left, 0)
    tail(trips - 1)


def _gla_branch(qd, ki, kt, tot, v, r, out_g, batch, seq):
    n_chunks = seq // GLA_CHUNK
    assert 2 * GLA_DK == GLA_DV
    vv = pl.BlockSpec((None, seq, GLA_DV), lambda b, h: (h, b, 0))
    return pl.pallas_call(
        _gla_kernel,
        grid=(batch, GLA_HEADS),
        in_specs=[vv, vv, vv, pl.BlockSpec((n_chunks, 1, 2 * GLA_DK), lambda b, h: (b, 0, h)),
                  vv, vv, _const_spec((1, GLA_DV))],
        out_specs=vv,
        out_shape=jax.ShapeDtypeStruct((GLA_HEADS, batch * seq, GLA_DV), BF16),
        scratch_shapes=[pltpu.VMEM((n_chunks, GLA_DV, GLA_DK), BF16),
                        pltpu.VMEM((n_chunks, GLA_DV, GLA_DK), F32),
                        pltpu.VMEM((GLA_DV, GLA_DK), F32),
                        pltpu.VMEM((2, GLA_OUT_UNROLL, GLA_CHUNK, GLA_DV), F32)],
        compiler_params=pltpu.CompilerParams(
            dimension_semantics=("parallel", "parallel"), vmem_limit_bytes=VMEM_LIMIT),
        name="gla_branch",
    )(qd, ki, kt, tot, v, r, out_g)


def _swa_kernel(sink_ref, q_ref, k_ref, v_ref, o_ref):
    blk = SWA_BLOCK
    hd = SWA_HEAD_DIM
    tile_blocks = q_ref.shape[0] // blk
    seq_blocks = k_ref.shape[0] // blk
    tile = pl.program_id(1)
    rows_g = SWA_GROUP * blk
    qi = lax.broadcasted_iota(jnp.int32, (rows_g, blk), 0) % blk
    kj = lax.broadcasted_iota(jnp.int32, (rows_g, blk), 1)
    ones = jnp.ones((3 * blk, hd), BF16)
    head_cols = lambda hh: slice(hh * hd, (hh + 1) * hd)

    def step(i, carry):
        jobs = [(i * SWA_UNROLL + t, hk) for t in range(SWA_UNROLL) for hk in range(SWA_KV_HEADS)]
        q4, kv_rows, masks = [], [], []
        for t, hk in jobs:
            n = tile * tile_blocks + t
            q_rows = pl.ds(pl.multiple_of(t * blk, blk), blk)
            q4.append(jnp.concatenate(
                [q_ref[q_rows, head_cols(hk * SWA_GROUP + g)] for g in range(SWA_GROUP)], axis=0))
            kv_rows.append([pl.ds(pl.multiple_of(nn * blk, blk), blk) for nn in
                            (jnp.maximum(n - 1, 0), n, jnp.minimum(n + 1, seq_blocks - 1))])
            masks.append([kj >= qi + jnp.where(n > 0, 0, blk), None,
                          kj <= qi - jnp.where(n < seq_blocks - 1, 0, blk)])
        s = [[_dot_nt(q, k_ref[r, head_cols(hk)]) for r in rows]
             for q, rows, (_, hk) in zip(q4, kv_rows, jobs)]
        s = [[sj if mask is None else jnp.where(mask, sj, MASK_VALUE)
              for sj, mask in zip(sb, mb)] for sb, mb in zip(s, masks)]
        row_max = [jnp.broadcast_to(
            jnp.maximum(jnp.maximum(sb[0], sb[1]), sb[2]).max(axis=-1, keepdims=True),
            (rows_g, blk)) for sb in s]
        m, sink_w = [], []
        for rm, (_, hk) in zip(row_max, jobs):
            sinks = [sink_ref[hk * SWA_GROUP + g] * LOG2_E for g in range(SWA_GROUP)]
            mg = [jnp.maximum(rm[g * blk:(g + 1) * blk], sk) for g, sk in enumerate(sinks)]
            m.append(jnp.concatenate(mg, axis=0))
            sink_w.append(jnp.concatenate([jnp.exp2(sk - x) for sk, x in zip(sinks, mg)], axis=0))
        p = [jnp.concatenate([jnp.exp2(sj - mb).astype(BF16) for sj in sb], axis=1)
             for sb, mb in zip(s, m)]
        pv = [_dot(pj, jnp.concatenate(
                  [jnp.concatenate([v_ref[r, head_cols(hk)] for r in rows], axis=0), ones], axis=1))
              for pj, rows, (_, hk) in zip(p, kv_rows, jobs)]
        for acc, sw, (t, hk) in zip(pv, sink_w, jobs):
            o = acc[:, :hd] / (acc[:, hd:] + sw)
            q_rows = pl.ds(pl.multiple_of(t * blk, blk), blk)
            for g in range(SWA_GROUP):
                o_ref[q_rows, head_cols(hk * SWA_GROUP + g)] = (
                    o[g * blk:(g + 1) * blk, :].astype(BF16))
        return carry

    lax.fori_loop(0, tile_blocks // SWA_UNROLL, step, 0)


def _swa_branch(sq, sk, sv, sinks, batch, seq):
    tq = SWA_TILE_Q
    tiles = seq // tq
    return pl.pallas_call(
        _swa_kernel,
        grid=(batch, tiles),
        in_specs=[pl.BlockSpec(memory_space=pltpu.SMEM),
                  pl.BlockSpec((tq, SWA_Q_W), lambda b, t: (b * tiles + t, 0)),
                  pl.BlockSpec((seq, SWA_KV_W), lambda b, t: (b, 0)),
                  pl.BlockSpec((seq, SWA_KV_W), lambda b, t: (b, 0))],
        out_specs=pl.BlockSpec((tq, SWA_Q_W), lambda b, t: (b * tiles + t, 0)),
        out_shape=jax.ShapeDtypeStruct((batch * seq, SWA_Q_W), BF16),
        compiler_params=pltpu.CompilerParams(
            dimension_semantics=("parallel", "parallel"), vmem_limit_bytes=VMEM_LIMIT),
        name="swa_branch",
    )(sinks, sq, sk, sv)


def _out_ffn_kernel(x_ref, og_ref, os_ref, ga_ref, gb_ref, wog_ref, wos_ref, wout_ref, g2_ref,
                    wfi_ref, wfo_ref, o_ref):
    o_gla = jnp.concatenate([og_ref[hh] for hh in range(GLA_HEADS)], axis=1)
    y_gla = _dot(o_gla, wog_ref[...])
    y_swa = _dot(os_ref[...], wos_ref[...])
    merged = ga_ref[...].astype(F32) * y_gla + gb_ref[...].astype(F32) * y_swa
    x1 = x_ref[...] + _dot(merged.astype(BF16), wout_ref[...])
    h2 = _rms(x1, g2_ref[...]).astype(BF16)
    acc = x1
    for c0 in range(0, D_FF, FFN_CHUNK):
        cols = slice(c0, c0 + FFN_CHUNK)
        gate = _dot(h2, wfi_ref[:, cols])
        up = _dot(h2, wfi_ref[:, D_FF + c0:D_FF + c0 + FFN_CHUNK])
        act = (jax.nn.silu(gate) * up).astype(BF16)
        acc = acc + _dot(act, wfo_ref[cols, :])
    o_ref[...] = acc


def _out_ffn(x2, o_gla, o_swa, ga, gb, w_o_gla, w_o_swa, w_out, norm_g, w_fi, w_fo):
    m = x2.shape[0]
    tm = OUT_TILE_M
    row = pl.BlockSpec((tm, D_MODEL), lambda i: (i, 0))
    consts = (w_o_gla, w_o_swa, w_out, norm_g, w_fi, w_fo)
    return pl.pallas_call(
        _out_ffn_kernel,
        grid=(m // tm,),
        in_specs=[row, pl.BlockSpec((GLA_HEADS, tm, GLA_DV), lambda i: (0, i, 0))] + [row] * 3
                 + [_const_spec(w.shape) for w in consts],
        out_specs=row,
        out_shape=jax.ShapeDtypeStruct((m, D_MODEL), F32),
        compiler_params=pltpu.CompilerParams(
            dimension_semantics=("parallel",), vmem_limit_bytes=VMEM_LIMIT),
        name="out_ffn",
    )(x2, o_gla, o_swa, ga, gb, *consts)


def _rope_tables(seq):
    half = SWA_HEAD_DIM // 2
    inv_freq = ROPE_THETA ** (-np.arange(half, dtype=np.float64) / half)
    ang = np.arange(seq, dtype=np.float64)[:, None] * inv_freq[None, :]
    cos, sin = np.cos(ang), np.sin(ang)
    return (jnp.asarray(np.concatenate([cos, cos], axis=-1), F32),
            jnp.asarray(np.concatenate([-sin, sin], axis=-1), F32))


def _gate_up(up_f, up_b):
    zeros = jnp.zeros_like(up_f)
    top = jnp.concatenate([up_f, zeros], axis=1)
    bot = jnp.concatenate([zeros, up_b], axis=1)
    pad = jnp.zeros((LANES - 2 * GLA_GATE_RANK, 2 * GLA_QK_W), up_f.dtype)
    return jnp.concatenate([top, bot, pad], axis=0).astype(BF16)


def _layer(x2, batch, seq, norm_mix_g, w_in, up_f, bias_f, up_b, bias_b, gla_out_g, w_o_gla,
           q_g, k_g, sinks, w_o_swa, w_out, norm_ffn_g, w_ffn_in, w_ffn_out):
    w = w_in.astype(BF16)
    cos, sin = _rope_tables(seq)
    row = lambda t: t.reshape(1, -1)
    qd, ki, kt, tot, v, r, sq, sk, sv, ga, gb = _in_projection(
        x2, row(norm_mix_g), cos, sin, row(q_g), row(k_g),
        (w, _gate_up(up_f, up_b), row(jnp.concatenate([bias_f, bias_b]))), seq)
    o_gla = _gla_branch(qd, ki, kt, tot, v, r, row(gla_out_g), batch, seq)
    o_swa = _swa_branch(sq, sk, sv, sinks, batch, seq)
    return _out_ffn(x2, o_gla, o_swa, ga, gb, w_o_gla.astype(BF16), w_o_swa.astype(BF16),
                    w_out.astype(BF16), row(norm_ffn_g), w_ffn_in.astype(BF16),
                    w_ffn_out.astype(BF16))


def kernel(x, norm_mix_g, w_in, gla_gate_up_fwd, gla_gate_bias_fwd, gla_gate_up_bwd,
           gla_gate_bias_bwd, gla_out_norm_g, w_o_gla, swa_q_norm_g, swa_k_norm_g,
           swa_sinks, w_o_swa, w_out, norm_ffn_g, w_ffn_in, w_ffn_out):
    batch, seq, d = x.shape
    x2 = x.reshape(batch * seq, d)
    for l in range(w_in.shape[0]):
        x2 = _layer(x2, batch, seq, norm_mix_g[l], w_in[l], gla_gate_up_fwd[l],
                    gla_gate_bias_fwd[l], gla_gate_up_bwd[l], gla_gate_bias_bwd[l],
                    gla_out_norm_g[l], w_o_gla[l], swa_q_norm_g[l], swa_k_norm_g[l],
                    swa_sinks[l], w_o_swa[l], w_out[l], norm_ffn_g[l], w_ffn_in[l],
                    w_ffn_out[l])
    return x2.reshape(batch, seq, d)
```

```python
import math

import jax
import jax.numpy as jnp
import numpy as np
from jax import lax
from jax.experimental import pallas as pl
from jax.experimental.pallas import tpu as pltpu

F32 = jnp.float32
BF16 = jnp.bfloat16

D_MODEL = 1024
NORM_EPS = 1e-6
GLA_HEADS = 4
GLA_DK = 128
GLA_DV = 256
GLA_QK_W = GLA_HEADS * GLA_DK
GLA_V_W = GLA_HEADS * GLA_DV
GLA_GATE_RANK = 16
GLA_GATE_NORMALIZER = 16.0
GLA_LOG_GATE_MIN = -0.5
GLA_CHUNK = 128
GLA_STATE_UNROLL = 8
GLA_OUT_UNROLL = 4
SWA_HEADS = 8
SWA_KV_HEADS = 2
SWA_GROUP = SWA_HEADS // SWA_KV_HEADS
SWA_HEAD_DIM = 128
SWA_Q_W = SWA_HEADS * SWA_HEAD_DIM
SWA_KV_W = SWA_KV_HEADS * SWA_HEAD_DIM
SWA_BLOCK = 128
ROPE_THETA = 10000.0
D_FF = 2816
IN_SPLITS = (GLA_QK_W, GLA_QK_W, GLA_V_W, GLA_V_W, GLA_GATE_RANK, GLA_GATE_RANK,
             SWA_Q_W, SWA_KV_W, SWA_KV_W, D_MODEL, D_MODEL)

LANES = 128
W_IN_OFFSETS = tuple(int(o) for o in np.cumsum((0,) + IN_SPLITS))
W_IN_CODES = W_IN_OFFSETS[4]
W_IN_TAIL = W_IN_OFFSETS[6]
MXU_COLS = 256
MASK_VALUE = -1e30
LOG2_E = math.log2(math.e)
VMEM_LIMIT = 56 * 1024 * 1024

IN_TILE_M = 512
SWA_TILE_Q = 512
SWA_UNROLL = 4
OUT_TILE_M = 512
FFN_CHUNK = D_FF // 2
STAGE_ROWS = 64


def _const_spec(shape):
    zeros = (0,) * len(shape)
    return pl.BlockSpec(shape, lambda *_: zeros, pipeline_mode=pl.Buffered(1))


def _dot(a, b):
    return jnp.dot(a, b, preferred_element_type=F32)


def _dot_nt(a, b):
    return lax.dot_general(a, b, (((1,), (1,)), ((), ())), preferred_element_type=F32)


def _dot_tn(a, b):
    return lax.dot_general(a, b, (((0,), (0,)), ((), ())), preferred_element_type=F32)


def _rms(x, gain):
    ms = jnp.mean(x * x, axis=-1, keepdims=True)
    return x * lax.rsqrt(ms + NORM_EPS) * gain


def _stream_rows(src_hbm, row0, chunk_rows, n_chunks, stage_ref, sem_ref, consume):
    n_cols = src_hbm.shape[1]

    def copy(c, slot):
        return pltpu.make_async_copy(src_hbm.at[pl.ds(row0 + c * chunk_rows, chunk_rows), :],
                                     stage_ref.at[slot, :chunk_rows, :n_cols], sem_ref.at[slot])

    copy(0, 0).start()

    def body(c, carry):
        slot = c % 2

        @pl.when(c + 1 < n_chunks)
        def _():
            copy(c + 1, 1 - slot).start()

        copy(c, slot).wait()
        consume(c, stage_ref.at[slot, :chunk_rows, :n_cols])
        return carry

    lax.fori_loop(0, n_chunks, body, 0)


def _inproj_kernel(x_ref, g_ref, cos_ref, sin_ref, qg_ref, kg_ref, w_ref, up_ref, gbias_ref,
                   qd_ref, ki_ref, kt_ref, tot_ref, v_ref, r_ref, sq_ref, sk_ref, sv_ref,
                   ga_ref, gb_ref, whead_ref, wtail_ref, stage_ref, sem_ref):
    tm = x_ref.shape[0]
    c = GLA_CHUNK

    @pl.when(pl.program_id(0) == 0)
    def _():
        for dst_ref, row0 in ((whead_ref, 0), (wtail_ref, W_IN_TAIL)):
            def cast(j, chunk_ref, dst_ref=dst_ref):
                cols = pl.ds(pl.multiple_of(j * LANES, LANES), LANES)
                dst_ref[:, cols] = chunk_ref[...].T.astype(BF16)
            _stream_rows(w_ref, row0, LANES, dst_ref.shape[1] // LANES, stage_ref, sem_ref, cast)

    w_ref = whead_ref
    wq_ref, wk_ref, wv_ref, wr_ref = [
        w_ref.at[:, a:b] for a, b in zip(W_IN_OFFSETS[:4], W_IN_OFFSETS[1:5])]
    wsq_ref, wsk_ref, wsv_ref, wga_ref, wgb_ref = [
        wtail_ref.at[:, a - W_IN_TAIL:b - W_IN_TAIL]
        for a, b in zip(W_IN_OFFSETS[6:-1], W_IN_OFFSETS[7:])]
    wlr_ref = w_ref.at[:, W_IN_CODES:W_IN_CODES + LANES]
    h = _rms(x_ref[...], g_ref[...]).astype(BF16)
    cos = cos_ref[...]
    sin = sin_ref[...]

    def norm_rope(t, gain, scale):
        y = _rms(t, gain)
        rot = pltpu.roll(y, SWA_HEAD_DIM // 2, axis=1)
        return ((y * cos + rot * sin) * scale).astype(BF16)

    hd = SWA_HEAD_DIM
    for w_ref, o_ref, gain_ref, scale in ((wsk_ref, sk_ref, kg_ref, 1.0),
                                          (wsq_ref, sq_ref, qg_ref, hd ** -0.5 * LOG2_E)):
        for c0 in range(0, o_ref.shape[1], MXU_COLS):
            t = _dot(h, w_ref[:, c0:c0 + MXU_COLS])
            for c1 in range(0, MXU_COLS, hd):
                o_ref[:, c0 + c1:c0 + c1 + hd] = norm_rope(t[:, c1:c1 + hd], gain_ref[...], scale)

    lr = _dot(h, wlr_ref[...]).astype(BF16)
    ga_ref[...] = jax.nn.sigmoid(_dot(h, wga_ref[...])).astype(BF16)
    z = _dot(lr, up_ref[...]) + gbias_ref[...]
    log2_g = jnp.maximum(
        (jnp.minimum(z, 0.0) * LOG2_E - jnp.log2(1.0 + jnp.exp2(jnp.abs(z) * -LOG2_E)))
        * (1.0 / GLA_GATE_NORMALIZER), GLA_LOG_GATE_MIN * LOG2_E)
    hi = log2_g.astype(BF16)
    lo = (log2_g - hi.astype(F32)).astype(BF16)

    gb_ref[...] = jax.nn.sigmoid(_dot(h, wgb_ref[...])).astype(BF16)
    q = _dot(h, wq_ref[...]) * (GLA_DK ** -0.5)
    k = _dot(h, wk_ref[...])

    ri = lax.broadcasted_iota(jnp.int32, (c, 2 * c), 0)
    ci = lax.broadcasted_iota(jnp.int32, (c, 2 * c), 1) % c
    tri = (jnp.where(ri >= ci, 1.0, 0.0).astype(BF16), jnp.where(ci >= ri, 1.0, 0.0).astype(BF16))
    total_row = (c - 1, 0)
    for t in range(tm // c):
        rows = slice(t * c, (t + 1) * c)
        totals = []
        for d in range(2):
            cols = slice(d * GLA_QK_W, (d + 1) * GLA_QK_W)
            b = _dot(tri[d], jnp.concatenate([hi[rows, cols], lo[rows, cols]], axis=0))
            decay = jnp.exp2(b)
            total = decay[total_row[d]:total_row[d] + 1, :]
            totals.append(total)
            q_dec = (q[rows, :] * decay).astype(BF16)
            k_inv = k[rows, :] * jnp.exp2(-b)
            k_tail = (k_inv * total).astype(BF16)
            k_inv = k_inv.astype(BF16)
            for hh in range(GLA_HEADS):
                src = slice(hh * GLA_DK, (hh + 1) * GLA_DK)
                dst = slice(d * GLA_DK, (d + 1) * GLA_DK)
                qd_ref[hh, rows, dst] = q_dec[:, src]
                ki_ref[hh, rows, dst] = k_inv[:, src]
                kt_ref[hh, rows, dst] = k_tail[:, src]
        tot_ref[t] = jnp.concatenate(
            [totals[d][:, hh * GLA_DK:(hh + 1) * GLA_DK]
             for hh in range(GLA_HEADS) for d in range(2)], axis=1)

    sv_ref[...] = _dot(h, wsv_ref[...]).astype(BF16)
    for hh in range(GLA_HEADS):
        cols = slice(hh * GLA_DV, (hh + 1) * GLA_DV)
        r_ref[hh] = jax.nn.silu(_dot(h, wr_ref[:, cols])).astype(BF16)
    for hh in range(GLA_HEADS):
        cols = slice(hh * GLA_DV, (hh + 1) * GLA_DV)
        v_ref[hh] = _dot(h, wv_ref[:, cols]).astype(BF16)


def _in_projection(x2, norm_g, cos, sin, q_g, k_g, weights, seq):
    m = x2.shape[0]
    tm = IN_TILE_M
    pos_blocks = seq // tm
    row = lambda w: pl.BlockSpec((tm, w), lambda i: (i, 0))
    pos = pl.BlockSpec((tm, SWA_HEAD_DIM), lambda i: (i % pos_blocks, 0))
    head_major = (pl.BlockSpec((GLA_HEADS, tm, GLA_DV), lambda i: (0, i, 0)),
                  jax.ShapeDtypeStruct((GLA_HEADS, m, GLA_DV), BF16))
    tot = (pl.BlockSpec((tm // GLA_CHUNK, 1, 2 * GLA_QK_W), lambda i: (i, 0, 0)),
           jax.ShapeDtypeStruct((m // GLA_CHUNK, 1, 2 * GLA_QK_W), F32))
    flat = lambda w: (row(w), jax.ShapeDtypeStruct((m, w), BF16))
    outs = (head_major, head_major, head_major, tot, head_major, head_major,
            flat(SWA_Q_W), flat(SWA_KV_W), flat(SWA_KV_W), flat(D_MODEL), flat(D_MODEL))
    return pl.pallas_call(
        _inproj_kernel,
        grid=(m // tm,),
        in_specs=[row(D_MODEL), _const_spec((1, D_MODEL)), pos, pos,
                  _const_spec((1, SWA_HEAD_DIM)), _const_spec((1, SWA_HEAD_DIM)),
                  pl.BlockSpec(memory_space=pl.ANY)]
                 + [_const_spec(w.shape) for w in weights[1:]],
        out_specs=[spec for spec, _ in outs],
        out_shape=[shape for _, shape in outs],
        scratch_shapes=[pltpu.VMEM((D_MODEL, W_IN_CODES + LANES), BF16),
                        pltpu.VMEM((D_MODEL, W_IN_OFFSETS[-1] - W_IN_TAIL), BF16),
                        pltpu.VMEM((2, LANES, D_MODEL), F32),
                        pltpu.SemaphoreType.DMA((2,))],
        compiler_params=pltpu.CompilerParams(
            dimension_semantics=("arbitrary",), vmem_limit_bytes=VMEM_LIMIT),
        name="in_projection",
    )(x2, norm_g, cos, sin, q_g, k_g, *weights)


def _gla_kernel(qd_ref, ki_ref, kt_ref, tot_ref, v_ref, r_ref, og_ref,
                o_ref, sf_ref, kvb_ref, state_ref, oraw_ref):
    c = GLA_CHUNK
    dk = GLA_DK
    n_chunks = v_ref.shape[0] // c
    lower_incl = (lax.broadcasted_iota(jnp.int32, (c, c), 0)
                  >= lax.broadcasted_iota(jnp.int32, (c, c), 1))
    rows_of = lambda n: pl.ds(n * c if isinstance(n, int) else pl.multiple_of(n * c, c), c)

    state_ref[...] = jnp.zeros_like(state_ref)

    def sweep_right(i, carry):
        chunks = [i * GLA_STATE_UNROLL + t for t in range(GLA_STATE_UNROLL)]
        kv_t = [_dot_tn(v_ref[rows_of(n), :], kt_ref[rows_of(n), :]) for n in chunks]
        s = state_ref[...]
        for n, kv in zip(chunks, kv_t):
            sf_ref[n] = s.astype(BF16)
            kvb_ref[n] = kv[:, dk:]
            s = s * tot_ref[n][:, :dk] + kv[:, :dk]
        state_ref[...] = s
        return carry

    lax.fori_loop(0, n_chunks // GLA_STATE_UNROLL, sweep_right, 0)

    state_ref[...] = jnp.zeros_like(state_ref)
    u = GLA_OUT_UNROLL
    trips = n_chunks // u
    chunks_of = lambda i: [n_chunks - 1 - (i * u + t) for t in range(u)]

    def matmuls(i):
        chunks = chunks_of(i)
        rows = [rows_of(n) for n in chunks]
        qd = [qd_ref[r, :] for r in rows]
        ki = [ki_ref[r, :] for r in rows]
        a = [jnp.where(lower_incl, _dot_nt(q[:, :dk], k[:, :dk]),
                       _dot_nt(q[:, dk:], k[:, dk:])).astype(BF16) for q, k in zip(qd, ki)]
        s = state_ref[...]
        for j, n in enumerate(chunks):
            both = jnp.concatenate([sf_ref[n], s.astype(BF16)], axis=1)
            oraw_ref[i % 2, j] = _dot(a[j], v_ref[rows[j], :]) + _dot_nt(qd[j], both)
            s = s * tot_ref[n][:, dk:] + kvb_ref[n]
        state_ref[...] = s

    def tail(i):
        rows = [rows_of(n) for n in chunks_of(i)]
        finals = [(_rms(oraw_ref[i % 2, j], og_ref[...])
                   * r_ref[r, :].astype(F32)).astype(BF16)
                  for j, r in enumerate(rows)]
        for r, o in zip(rows, finals):
            o_ref[r, :] = o

    def sweep_left(i, carry):
        tail(i - 1)
        matmuls(i)
        return carry

    matmuls(0)
    lax.fori_loop(1, trips, sweep_left, 0)
    tail(trips - 1)


def _gla_branch(qd, ki, kt, tot, v, r, out_g, batch, seq):
    n_chunks = seq // GLA_CHUNK
    assert 2 * GLA_DK == GLA_DV
    vv = pl.BlockSpec((None, seq, GLA_DV), lambda b, h: (h, b, 0))
    return pl.pallas_call(
        _gla_kernel,
        grid=(batch, GLA_HEADS),
        in_specs=[vv, vv, vv, pl.BlockSpec((n_chunks, 1, 2 * GLA_DK), lambda b, h: (b, 0, h)),
                  vv, vv, _const_spec((1, GLA_DV))],
        out_specs=vv,
        out_shape=jax.ShapeDtypeStruct((GLA_HEADS, batch * seq, GLA_DV), BF16),
        scratch_shapes=[pltpu.VMEM((n_chunks, GLA_DV, GLA_DK), BF16),
                        pltpu.VMEM((n_chunks, GLA_DV, GLA_DK), F32),
                        pltpu.VMEM((GLA_DV, GLA_DK), F32),
                        pltpu.VMEM((2, GLA_OUT_UNROLL, GLA_CHUNK, GLA_DV), F32)],
        compiler_params=pltpu.CompilerParams(
            dimension_semantics=("parallel", "parallel"), vmem_limit_bytes=VMEM_LIMIT),
        name="gla_branch",
    )(qd, ki, kt, tot, v, r, out_g)


def _swa_kernel(sink_ref, q_ref, k_ref, v_ref, o_ref):
    blk = SWA_BLOCK
    hd = SWA_HEAD_DIM
    tile_blocks = q_ref.shape[0] // blk
    seq_blocks = k_ref.shape[0] // blk
    tile = pl.program_id(1)
    rows_g = SWA_GROUP * blk
    qi = lax.broadcasted_iota(jnp.int32, (rows_g, blk), 0) % blk
    kj = lax.broadcasted_iota(jnp.int32, (rows_g, blk), 1)
    ones = jnp.ones((3 * blk, hd), BF16)
    head_cols = lambda hh: slice(hh * hd, (hh + 1) * hd)

    def step(i, carry):
        jobs = [(i * SWA_UNROLL + t, hk) for t in range(SWA_UNROLL) for hk in range(SWA_KV_HEADS)]
        q4, kv_rows, masks = [], [], []
        for t, hk in jobs:
            n = tile * tile_blocks + t
            q_rows = pl.ds(pl.multiple_of(t * blk, blk), blk)
            q4.append(jnp.concatenate(
                [q_ref[q_rows, head_cols(hk * SWA_GROUP + g)] for g in range(SWA_GROUP)], axis=0))
            kv_rows.append([pl.ds(pl.multiple_of(nn * blk, blk), blk) for nn in
                            (jnp.maximum(n - 1, 0), n, jnp.minimum(n + 1, seq_blocks - 1))])
            masks.append([kj >= qi + jnp.where(n > 0, 0, blk), None,
                          kj <= qi - jnp.where(n < seq_blocks - 1, 0, blk)])
        s = [[_dot_nt(q, k_ref[r, head_cols(hk)]) for r in rows]
             for q, rows, (_, hk) in zip(q4, kv_rows, jobs)]
        s = [[sj if mask is None else jnp.where(mask, sj, MASK_VALUE)
              for sj, mask in zip(sb, mb)] for sb, mb in zip(s, masks)]
        row_max = [jnp.broadcast_to(
            jnp.maximum(jnp.maximum(sb[0], sb[1]), sb[2]).max(axis=-1, keepdims=True),
            (rows_g, blk)) for sb in s]
        m, sink_w = [], []
        for rm, (_, hk) in zip(row_max, jobs):
            sinks = [sink_ref[hk * SWA_GROUP + g] * LOG2_E for g in range(SWA_GROUP)]
            mg = [jnp.maximum(rm[g * blk:(g + 1) * blk], sk) for g, sk in enumerate(sinks)]
            m.append(jnp.concatenate(mg, axis=0))
            sink_w.append(jnp.concatenate([jnp.exp2(sk - x) for sk, x in zip(sinks, mg)], axis=0))
        p = [jnp.concatenate([jnp.exp2(sj - mb).astype(BF16) for sj in sb], axis=1)
             for sb, mb in zip(s, m)]
        pv = [_dot(pj, jnp.concatenate(
                  [jnp.concatenate([v_ref[r, head_cols(hk)] for r in rows], axis=0), ones], axis=1))
              for pj, rows, (_, hk) in zip(p, kv_rows, jobs)]
        for acc, sw, (t, hk) in zip(pv, sink_w, jobs):
            o = acc[:, :hd] / (acc[:, hd:] + sw)
            q_rows = pl.ds(pl.multiple_of(t * blk, blk), blk)
            for g in range(SWA_GROUP):
                o_ref[q_rows, head_cols(hk * SWA_GROUP + g)] = (
                    o[g * blk:(g + 1) * blk, :].astype(BF16))
        return carry

    lax.fori_loop(0, tile_blocks // SWA_UNROLL, step, 0)


def _swa_branch(sq, sk, sv, sinks, batch, seq):
    tq = SWA_TILE_Q
    tiles = seq // tq
    return pl.pallas_call(
        _swa_kernel,
        grid=(batch, tiles),
        in_specs=[pl.BlockSpec(memory_space=pltpu.SMEM),
                  pl.BlockSpec((tq, SWA_Q_W), lambda b, t: (b * tiles + t, 0)),
                  pl.BlockSpec((seq, SWA_KV_W), lambda b, t: (b, 0)),
                  pl.BlockSpec((seq, SWA_KV_W), lambda b, t: (b, 0))],
        out_specs=pl.BlockSpec((tq, SWA_Q_W), lambda b, t: (b * tiles + t, 0)),
        out_shape=jax.ShapeDtypeStruct((batch * seq, SWA_Q_W), BF16),
        compiler_params=pltpu.CompilerParams(
            dimension_semantics=("parallel", "parallel"), vmem_limit_bytes=VMEM_LIMIT),
        name="swa_branch",
    )(sinks, sq, sk, sv)


def _out_ffn_kernel(x_ref, og_ref, os_ref, ga_ref, gb_ref, g2_ref,
                    wog_hbm, wos_hbm, wout_hbm, wfi_hbm, wfo_hbm, o_ref,
                    wog_ref, wos_ref, wout_ref, wfi_ref, wfo_ref, stage_ref, sem_ref):
    @pl.when(pl.program_id(0) == 0)
    def _():
        for src, dst in ((wog_hbm, wog_ref), (wos_hbm, wos_ref), (wout_hbm, wout_ref),
                         (wfi_hbm, wfi_ref), (wfo_hbm, wfo_ref)):
            def cast(j, chunk_ref, dst=dst):
                rows = pl.ds(pl.multiple_of(j * STAGE_ROWS, STAGE_ROWS), STAGE_ROWS)
                dst[rows, :] = chunk_ref[...].astype(BF16)
            _stream_rows(src, 0, STAGE_ROWS, src.shape[0] // STAGE_ROWS, stage_ref, sem_ref, cast)

    o_gla = jnp.concatenate([og_ref[hh] for hh in range(GLA_HEADS)], axis=1)
    y_gla = _dot(o_gla, wog_ref[...])
    y_swa = _dot(os_ref[...], wos_ref[...])
    merged = ga_ref[...].astype(F32) * y_gla + gb_ref[...].astype(F32) * y_swa
    x1 = x_ref[...] + _dot(merged.astype(BF16), wout_ref[...])
    h2 = _rms(x1, g2_ref[...]).astype(BF16)
    acc = x1
    for c0 in range(0, D_FF, FFN_CHUNK):
        cols = slice(c0, c0 + FFN_CHUNK)
        gate = _dot(h2, wfi_ref[:, cols])
        up = _dot(h2, wfi_ref[:, D_FF + c0:D_FF + c0 + FFN_CHUNK])
        act = (jax.nn.silu(gate) * up).astype(BF16)
        acc = acc + _dot(act, wfo_ref[cols, :])
    o_ref[...] = acc


def _out_ffn(x2, o_gla, o_swa, ga, gb, norm_g, *weights):
    m = x2.shape[0]
    tm = OUT_TILE_M
    row = pl.BlockSpec((tm, D_MODEL), lambda i: (i, 0))
    return pl.pallas_call(
        _out_ffn_kernel,
        grid=(m // tm,),
        in_specs=[row, pl.BlockSpec((GLA_HEADS, tm, GLA_DV), lambda i: (0, i, 0))] + [row] * 3
                 + [_const_spec(norm_g.shape)]
                 + [pl.BlockSpec(memory_space=pl.ANY)] * len(weights),
        out_specs=row,
        out_shape=jax.ShapeDtypeStruct((m, D_MODEL), F32),
        scratch_shapes=[pltpu.VMEM(w.shape, BF16) for w in weights]
                       + [pltpu.VMEM((2, STAGE_ROWS, max(w.shape[1] for w in weights)), F32),
                          pltpu.SemaphoreType.DMA((2,))],
        compiler_params=pltpu.CompilerParams(
            dimension_semantics=("arbitrary",), vmem_limit_bytes=VMEM_LIMIT),
        name="out_ffn",
    )(x2, o_gla, o_swa, ga, gb, norm_g, *weights)


def _rope_tables(seq):
    half = SWA_HEAD_DIM // 2
    inv_freq = ROPE_THETA ** (-np.arange(half, dtype=np.float64) / half)
    ang = np.arange(seq, dtype=np.float64)[:, None] * inv_freq[None, :]
    cos, sin = np.cos(ang), np.sin(ang)
    return (jnp.asarray(np.concatenate([cos, cos], axis=-1), F32),
            jnp.asarray(np.concatenate([-sin, sin], axis=-1), F32))


def _gate_up(up_f, up_b):
    zeros = jnp.zeros_like(up_f)
    top = jnp.concatenate([up_f, zeros], axis=1)
    bot = jnp.concatenate([zeros, up_b], axis=1)
    pad = jnp.zeros((LANES - 2 * GLA_GATE_RANK, 2 * GLA_QK_W), up_f.dtype)
    return jnp.concatenate([top, bot, pad], axis=0).astype(BF16)


def _layer(x2, batch, seq, norm_mix_g, w_in, up_f, bias_f, up_b, bias_b, gla_out_g, w_o_gla,
           q_g, k_g, sinks, w_o_swa, w_out, norm_ffn_g, w_ffn_in, w_ffn_out):
    cos, sin = _rope_tables(seq)
    row = lambda t: t.reshape(1, -1)
    qd, ki, kt, tot, v, r, sq, sk, sv, ga, gb = _in_projection(
        x2, row(norm_mix_g), cos, sin, row(q_g), row(k_g),
        (w_in.T, _gate_up(up_f, up_b), row(jnp.concatenate([bias_f, bias_b]))), seq)
    o_gla = _gla_branch(qd, ki, kt, tot, v, r, row(gla_out_g), batch, seq)
    o_swa = _swa_branch(sq, sk, sv, sinks, batch, seq)
    return _out_ffn(x2, o_gla, o_swa, ga, gb, row(norm_ffn_g),
                    w_o_gla, w_o_swa, w_out, w_ffn_in, w_ffn_out)


def kernel(x, norm_mix_g, w_in, gla_gate_up_fwd, gla_gate_bias_fwd, gla_gate_up_bwd,
           gla_gate_bias_bwd, gla_out_norm_g, w_o_gla, swa_q_norm_g, swa_k_norm_g,
           swa_sinks, w_o_swa, w_out, norm_ffn_g, w_ffn_in, w_ffn_out):
    batch, seq, d = x.shape
    x2 = x.reshape(batch * seq, d)
    for l in range(w_in.shape[0]):
        x2 = _layer(x2, batch, seq, norm_mix_g[l], w_in[l], gla_gate_up_fwd[l],
                    gla_gate_bias_fwd[l], gla_gate_up_bwd[l], gla_gate_bias_bwd[l],
                    gla_out_norm_g[l], w_o_gla[l], swa_q_norm_g[l], swa_k_norm_g[l],
                    swa_sinks[l], w_o_swa[l], w_out[l], norm_ffn_g[l], w_ffn_in[l],
                    w_ffn_out[l])
    return x2.reshape(batch, seq, d)
```

```python
import math

import jax
import jax.numpy as jnp
import numpy as np
from jax import lax
from jax.experimental import pallas as pl
from jax.experimental.pallas import tpu as pltpu

F32 = jnp.float32
BF16 = jnp.bfloat16

D_MODEL = 1024
NORM_EPS = 1e-6
GLA_HEADS = 4
GLA_DK = 128
GLA_DV = 256
GLA_QK_W = GLA_HEADS * GLA_DK
GLA_V_W = GLA_HEADS * GLA_DV
GLA_GATE_RANK = 16
GLA_GATE_NORMALIZER = 16.0
GLA_LOG_GATE_MIN = -0.5
GLA_CHUNK = 128
GLA_STATE_UNROLL = 8
GLA_OUT_UNROLL = 4
SWA_HEADS = 8
SWA_KV_HEADS = 2
SWA_GROUP = SWA_HEADS // SWA_KV_HEADS
SWA_HEAD_DIM = 128
SWA_Q_W = SWA_HEADS * SWA_HEAD_DIM
SWA_KV_W = SWA_KV_HEADS * SWA_HEAD_DIM
SWA_BLOCK = 128
ROPE_THETA = 10000.0
D_FF = 2816
IN_SPLITS = (GLA_QK_W, GLA_QK_W, GLA_V_W, GLA_V_W, GLA_GATE_RANK, GLA_GATE_RANK,
             SWA_Q_W, SWA_KV_W, SWA_KV_W, D_MODEL, D_MODEL)

LANES = 128
W_IN_OFFSETS = tuple(int(o) for o in np.cumsum((0,) + IN_SPLITS))
W_IN_CODES = W_IN_OFFSETS[4]
W_IN_TAIL = W_IN_OFFSETS[6]
MXU_COLS = 256
MASK_VALUE = -1e30
LOG2_E = math.log2(math.e)
VMEM_LIMIT = 56 * 1024 * 1024

IN_TILE_M = 512
SWA_TILE_Q = 512
SWA_UNROLL = 4
OUT_TILE_M = 512
FFN_CHUNK = D_FF // 2
STAGE_SLOTS = 8
OUT_STAGE_TILE = (256, 512)


def _const_spec(shape):
    zeros = (0,) * len(shape)
    return pl.BlockSpec(shape, lambda *_: zeros, pipeline_mode=pl.Buffered(1))


def _dot(a, b):
    return jnp.dot(a, b, preferred_element_type=F32)


def _dot_nt(a, b):
    return lax.dot_general(a, b, (((1,), (1,)), ((), ())), preferred_element_type=F32)


def _dot_tn(a, b):
    return lax.dot_general(a, b, (((0,), (0,)), ((), ())), preferred_element_type=F32)


def _rms(x, gain):
    ms = jnp.mean(x * x, axis=-1, keepdims=True)
    return x * lax.rsqrt(ms + NORM_EPS) * gain


def _stream_tiles(tiles, stage_ref, sem_ref):
    copies = [pltpu.make_async_copy(src, stage_ref.at[t % STAGE_SLOTS], sem_ref.at[t % STAGE_SLOTS])
              for t, (src, _) in enumerate(tiles)]
    depth = STAGE_SLOTS - 1
    for cp in copies[:depth]:
        cp.start()
    for t, (_, consume) in enumerate(tiles):
        if t + depth < len(tiles):
            copies[t + depth].start()
        copies[t].wait()
        consume(stage_ref.at[t % STAGE_SLOTS])


def _inproj_kernel(x_ref, g_ref, cos_ref, sin_ref, qg_ref, kg_ref, w_ref, up_ref, gbias_ref,
                   qd_ref, ki_ref, kt_ref, tot_ref, v_ref, r_ref, sq_ref, sk_ref, sv_ref,
                   ga_ref, gb_ref, whead_ref, wtail_ref, stage_ref, sem_ref):
    tm = x_ref.shape[0]
    c = GLA_CHUNK

    @pl.when(pl.program_id(0) == 0)
    def _():
        tiles = []
        for dst_ref, row0 in ((whead_ref, 0), (wtail_ref, W_IN_TAIL)):
            for c0 in range(0, dst_ref.shape[1], LANES):
                def cast(tile_ref, dst_ref=dst_ref, c0=c0):
                    dst_ref[:, c0:c0 + LANES] = tile_ref[...].T.astype(BF16)
                tiles.append((w_ref.at[row0 + c0:row0 + c0 + LANES, :], cast))
        _stream_tiles(tiles, stage_ref, sem_ref)

    w_ref = whead_ref
    wq_ref, wk_ref, wv_ref, wr_ref = [
        w_ref.at[:, a:b] for a, b in zip(W_IN_OFFSETS[:4], W_IN_OFFSETS[1:5])]
    wsq_ref, wsk_ref, wsv_ref, wga_ref, wgb_ref = [
        wtail_ref.at[:, a - W_IN_TAIL:b - W_IN_TAIL]
        for a, b in zip(W_IN_OFFSETS[6:-1], W_IN_OFFSETS[7:])]
    wlr_ref = w_ref.at[:, W_IN_CODES:W_IN_CODES + LANES]
    h = _rms(x_ref[...], g_ref[...]).astype(BF16)
    cos = cos_ref[...]
    sin = sin_ref[...]

    def norm_rope(t, gain, scale):
        y = _rms(t, gain)
        rot = pltpu.roll(y, SWA_HEAD_DIM // 2, axis=1)
        return ((y * cos + rot * sin) * scale).astype(BF16)

    hd = SWA_HEAD_DIM
    for w_ref, o_ref, gain_ref, scale in ((wsk_ref, sk_ref, kg_ref, 1.0),
                                          (wsq_ref, sq_ref, qg_ref, hd ** -0.5 * LOG2_E)):
        for c0 in range(0, o_ref.shape[1], MXU_COLS):
            t = _dot(h, w_ref[:, c0:c0 + MXU_COLS])
            for c1 in range(0, MXU_COLS, hd):
                o_ref[:, c0 + c1:c0 + c1 + hd] = norm_rope(t[:, c1:c1 + hd], gain_ref[...], scale)

    lr = _dot(h, wlr_ref[...]).astype(BF16)
    ga_ref[...] = jax.nn.sigmoid(_dot(h, wga_ref[...])).astype(BF16)
    z = _dot(lr, up_ref[...]) + gbias_ref[...]
    log2_g = jnp.maximum(
        (jnp.minimum(z, 0.0) * LOG2_E - jnp.log2(1.0 + jnp.exp2(jnp.abs(z) * -LOG2_E)))
        * (1.0 / GLA_GATE_NORMALIZER), GLA_LOG_GATE_MIN * LOG2_E)
    hi = log2_g.astype(BF16)
    lo = (log2_g - hi.astype(F32)).astype(BF16)

    gb_ref[...] = jax.nn.sigmoid(_dot(h, wgb_ref[...])).astype(BF16)
    q = _dot(h, wq_ref[...]) * (GLA_DK ** -0.5)
    k = _dot(h, wk_ref[...])

    ri = lax.broadcasted_iota(jnp.int32, (c, 2 * c), 0)
    ci = lax.broadcasted_iota(jnp.int32, (c, 2 * c), 1) % c
    tri = (jnp.where(ri >= ci, 1.0, 0.0).astype(BF16), jnp.where(ci >= ri, 1.0, 0.0).astype(BF16))
    total_row = (c - 1, 0)
    for t in range(tm // c):
        rows = slice(t * c, (t + 1) * c)
        totals = []
        for d in range(2):
            cols = slice(d * GLA_QK_W, (d + 1) * GLA_QK_W)
            b = _dot(tri[d], jnp.concatenate([hi[rows, cols], lo[rows, cols]], axis=0))
            decay = jnp.exp2(b)
            total = decay[total_row[d]:total_row[d] + 1, :]
            totals.append(total)
            q_dec = (q[rows, :] * decay).astype(BF16)
            k_inv = k[rows, :] * jnp.exp2(-b)
            k_tail = (k_inv * total).astype(BF16)
            k_inv = k_inv.astype(BF16)
            for hh in range(GLA_HEADS):
                src = slice(hh * GLA_DK, (hh + 1) * GLA_DK)
                dst = slice(d * GLA_DK, (d + 1) * GLA_DK)
                qd_ref[hh, rows, dst] = q_dec[:, src]
                ki_ref[hh, rows, dst] = k_inv[:, src]
                kt_ref[hh, rows, dst] = k_tail[:, src]
        tot_ref[t] = jnp.concatenate(
            [totals[d][:, hh * GLA_DK:(hh + 1) * GLA_DK]
             for hh in range(GLA_HEADS) for d in range(2)], axis=1)

    sv_ref[...] = _dot(h, wsv_ref[...]).astype(BF16)
    for hh in range(GLA_HEADS):
        cols = slice(hh * GLA_DV, (hh + 1) * GLA_DV)
        r_ref[hh] = jax.nn.silu(_dot(h, wr_ref[:, cols])).astype(BF16)
    for hh in range(GLA_HEADS):
        cols = slice(hh * GLA_DV, (hh + 1) * GLA_DV)
        v_ref[hh] = _dot(h, wv_ref[:, cols]).astype(BF16)


def _in_projection(x2, norm_g, cos, sin, q_g, k_g, weights, seq):
    m = x2.shape[0]
    tm = IN_TILE_M
    pos_blocks = seq // tm
    row = lambda w: pl.BlockSpec((tm, w), lambda i: (i, 0))
    pos = pl.BlockSpec((tm, SWA_HEAD_DIM), lambda i: (i % pos_blocks, 0))
    head_major = (pl.BlockSpec((GLA_HEADS, tm, GLA_DV), lambda i: (0, i, 0)),
                  jax.ShapeDtypeStruct((GLA_HEADS, m, GLA_DV), BF16))
    tot = (pl.BlockSpec((tm // GLA_CHUNK, 1, 2 * GLA_QK_W), lambda i: (i, 0, 0)),
           jax.ShapeDtypeStruct((m // GLA_CHUNK, 1, 2 * GLA_QK_W), F32))
    flat = lambda w: (row(w), jax.ShapeDtypeStruct((m, w), BF16))
    outs = (head_major, head_major, head_major, tot, head_major, head_major,
            flat(SWA_Q_W), flat(SWA_KV_W), flat(SWA_KV_W), flat(D_MODEL), flat(D_MODEL))
    return pl.pallas_call(
        _inproj_kernel,
        grid=(m // tm,),
        in_specs=[row(D_MODEL), _const_spec((1, D_MODEL)), pos, pos,
                  _const_spec((1, SWA_HEAD_DIM)), _const_spec((1, SWA_HEAD_DIM)),
                  pl.BlockSpec(memory_space=pl.ANY)]
                 + [_const_spec(w.shape) for w in weights[1:]],
        out_specs=[spec for spec, _ in outs],
        out_shape=[shape for _, shape in outs],
        scratch_shapes=[pltpu.VMEM((D_MODEL, W_IN_CODES + LANES), BF16),
                        pltpu.VMEM((D_MODEL, W_IN_OFFSETS[-1] - W_IN_TAIL), BF16),
                        pltpu.VMEM((STAGE_SLOTS, LANES, D_MODEL), F32),
                        pltpu.SemaphoreType.DMA((STAGE_SLOTS,))],
        compiler_params=pltpu.CompilerParams(
            dimension_semantics=("arbitrary",), vmem_limit_bytes=VMEM_LIMIT),
        name="in_projection",
    )(x2, norm_g, cos, sin, q_g, k_g, *weights)


def _gla_kernel(qd_ref, ki_ref, kt_ref, tot_ref, v_ref, r_ref, og_ref,
                o_ref, sf_ref, kvb_ref, state_ref, oraw_ref):
    c = GLA_CHUNK
    dk = GLA_DK
    n_chunks = v_ref.shape[0] // c
    lower_incl = (lax.broadcasted_iota(jnp.int32, (c, c), 0)
                  >= lax.broadcasted_iota(jnp.int32, (c, c), 1))
    rows_of = lambda n: pl.ds(n * c if isinstance(n, int) else pl.multiple_of(n * c, c), c)

    state_ref[...] = jnp.zeros_like(state_ref)

    def sweep_right(i, carry):
        chunks = [i * GLA_STATE_UNROLL + t for t in range(GLA_STATE_UNROLL)]
        kv_t = [_dot_tn(v_ref[rows_of(n), :], kt_ref[rows_of(n), :]) for n in chunks]
        s = state_ref[...]
        for n, kv in zip(chunks, kv_t):
            sf_ref[n] = s.astype(BF16)
            kvb_ref[n] = kv[:, dk:]
            s = s * tot_ref[n][:, :dk] + kv[:, :dk]
        state_ref[...] = s
        return carry

    lax.fori_loop(0, n_chunks // GLA_STATE_UNROLL, sweep_right, 0)

    state_ref[...] = jnp.zeros_like(state_ref)
    u = GLA_OUT_UNROLL
    trips = n_chunks // u
    chunks_of = lambda i: [n_chunks - 1 - (i * u + t) for t in range(u)]

    def matmuls(i):
        chunks = chunks_of(i)
        rows = [rows_of(n) for n in chunks]
        qd = [qd_ref[r, :] for r in rows]
        ki = [ki_ref[r, :] for r in rows]
        a = [jnp.where(lower_incl, _dot_nt(q[:, :dk], k[:, :dk]),
                       _dot_nt(q[:, dk:], k[:, dk:])).astype(BF16) for q, k in zip(qd, ki)]
        s = state_ref[...]
        for j, n in enumerate(chunks):
            both = jnp.concatenate([sf_ref[n], s.astype(BF16)], axis=1)
            oraw_ref[i % 2, j] = _dot(a[j], v_ref[rows[j], :]) + _dot_nt(qd[j], both)
            s = s * tot_ref[n][:, dk:] + kvb_ref[n]
        state_ref[...] = s

    def tail(i):
        rows = [rows_of(n) for n in chunks_of(i)]
        finals = [(_rms(oraw_ref[i % 2, j], og_ref[...])
                   * r_ref[r, :].astype(F32)).astype(BF16)
                  for j, r in enumerate(rows)]
        for r, o in zip(rows, finals):
            o_ref[r, :] = o

    def sweep_left(i, carry):
        tail(i - 1)
        matmuls(i)
        return carry

    matmuls(0)
    lax.fori_loop(1, trips, sweep_left, 0)
    tail(trips - 1)


def _gla_branch(qd, ki, kt, tot, v, r, out_g, batch, seq):
    n_chunks = seq // GLA_CHUNK
    assert 2 * GLA_DK == GLA_DV
    vv = pl.BlockSpec((None, seq, GLA_DV), lambda b, h: (h, b, 0))
    return pl.pallas_call(
        _gla_kernel,
        grid=(batch, GLA_HEADS),
        in_specs=[vv, vv, vv, pl.BlockSpec((n_chunks, 1, 2 * GLA_DK), lambda b, h: (b, 0, h)),
                  vv, vv, _const_spec((1, GLA_DV))],
        out_specs=vv,
        out_shape=jax.ShapeDtypeStruct((GLA_HEADS, batch * seq, GLA_DV), BF16),
        scratch_shapes=[pltpu.VMEM((n_chunks, GLA_DV, GLA_DK), BF16),
                        pltpu.VMEM((n_chunks, GLA_DV, GLA_DK), F32),
                        pltpu.VMEM((GLA_DV, GLA_DK), F32),
                        pltpu.VMEM((2, GLA_OUT_UNROLL, GLA_CHUNK, GLA_DV), F32)],
        compiler_params=pltpu.CompilerParams(
            dimension_semantics=("parallel", "parallel"), vmem_limit_bytes=VMEM_LIMIT),
        name="gla_branch",
    )(qd, ki, kt, tot, v, r, out_g)


def _swa_kernel(sink_ref, q_ref, k_ref, v_ref, o_ref):
    blk = SWA_BLOCK
    hd = SWA_HEAD_DIM
    tile_blocks = q_ref.shape[0] // blk
    seq_blocks = k_ref.shape[0] // blk
    tile = pl.program_id(1)
    rows_g = SWA_GROUP * blk
    qi = lax.broadcasted_iota(jnp.int32, (rows_g, blk), 0) % blk
    kj = lax.broadcasted_iota(jnp.int32, (rows_g, blk), 1)
    ones = jnp.ones((3 * blk, hd), BF16)
    head_cols = lambda hh: slice(hh * hd, (hh + 1) * hd)

    def step(i, carry):
        jobs = [(i * SWA_UNROLL + t, hk) for t in range(SWA_UNROLL) for hk in range(SWA_KV_HEADS)]
        q4, kv_rows, masks = [], [], []
        for t, hk in jobs:
            n = tile * tile_blocks + t
            q_rows = pl.ds(pl.multiple_of(t * blk, blk), blk)
            q4.append(jnp.concatenate(
                [q_ref[q_rows, head_cols(hk * SWA_GROUP + g)] for g in range(SWA_GROUP)], axis=0))
            kv_rows.append([pl.ds(pl.multiple_of(nn * blk, blk), blk) for nn in
                            (jnp.maximum(n - 1, 0), n, jnp.minimum(n + 1, seq_blocks - 1))])
            masks.append([kj >= qi + jnp.where(n > 0, 0, blk), None,
                          kj <= qi - jnp.where(n < seq_blocks - 1, 0, blk)])
        s = [[_dot_nt(q, k_ref[r, head_cols(hk)]) for r in rows]
             for q, rows, (_, hk) in zip(q4, kv_rows, jobs)]
        s = [[sj if mask is None else jnp.where(mask, sj, MASK_VALUE)
              for sj, mask in zip(sb, mb)] for sb, mb in zip(s, masks)]
        row_max = [jnp.broadcast_to(
            jnp.maximum(jnp.maximum(sb[0], sb[1]), sb[2]).max(axis=-1, keepdims=True),
            (rows_g, blk)) for sb in s]
        m, sink_w = [], []
        for rm, (_, hk) in zip(row_max, jobs):
            sinks = [sink_ref[hk * SWA_GROUP + g] * LOG2_E for g in range(SWA_GROUP)]
            mg = [jnp.maximum(rm[g * blk:(g + 1) * blk], sk) for g, sk in enumerate(sinks)]
            m.append(jnp.concatenate(mg, axis=0))
            sink_w.append(jnp.concatenate([jnp.exp2(sk - x) for sk, x in zip(sinks, mg)], axis=0))
        p = [jnp.concatenate([jnp.exp2(sj - mb).astype(BF16) for sj in sb], axis=1)
             for sb, mb in zip(s, m)]
        pv = [_dot(pj, jnp.concatenate(
                  [jnp.concatenate([v_ref[r, head_cols(hk)] for r in rows], axis=0), ones], axis=1))
              for pj, rows, (_, hk) in zip(p, kv_rows, jobs)]
        for acc, sw, (t, hk) in zip(pv, sink_w, jobs):
            o = acc[:, :hd] / (acc[:, hd:] + sw)
            q_rows = pl.ds(pl.multiple_of(t * blk, blk), blk)
            for g in range(SWA_GROUP):
                o_ref[q_rows, head_cols(hk * SWA_GROUP + g)] = (
                    o[g * blk:(g + 1) * blk, :].astype(BF16))
        return carry

    lax.fori_loop(0, tile_blocks // SWA_UNROLL, step, 0)


def _swa_branch(sq, sk, sv, sinks, batch, seq):
    tq = SWA_TILE_Q
    tiles = seq // tq
    return pl.pallas_call(
        _swa_kernel,
        grid=(batch, tiles),
        in_specs=[pl.BlockSpec(memory_space=pltpu.SMEM),
                  pl.BlockSpec((tq, SWA_Q_W), lambda b, t: (b * tiles + t, 0)),
                  pl.BlockSpec((seq, SWA_KV_W), lambda b, t: (b, 0)),
                  pl.BlockSpec((seq, SWA_KV_W), lambda b, t: (b, 0))],
        out_specs=pl.BlockSpec((tq, SWA_Q_W), lambda b, t: (b * tiles + t, 0)),
        out_shape=jax.ShapeDtypeStruct((batch * seq, SWA_Q_W), BF16),
        compiler_params=pltpu.CompilerParams(
            dimension_semantics=("parallel", "parallel"), vmem_limit_bytes=VMEM_LIMIT),
        name="swa_branch",
    )(sinks, sq, sk, sv)


def _out_ffn_kernel(x_ref, og_ref, os_ref, ga_ref, gb_ref, g2_ref,
                    wog_hbm, wos_hbm, wout_hbm, wfi_hbm, wfo_hbm, o_ref,
                    wog_ref, wos_ref, wout_ref, wfi_ref, wfo_ref, stage_ref, sem_ref):
    @pl.when(pl.program_id(0) == 0)
    def _():
        tr, tc = stage_ref.shape[1:]
        tiles = []
        for src, dst in ((wog_hbm, wog_ref), (wos_hbm, wos_ref), (wout_hbm, wout_ref),
                         (wfi_hbm, wfi_ref), (wfo_hbm, wfo_ref)):
            for r0 in range(0, src.shape[0], tr):
                for c0 in range(0, src.shape[1], tc):
                    def cast(tile_ref, dst=dst, r0=r0, c0=c0):
                        dst[r0:r0 + tr, c0:c0 + tc] = tile_ref[...].astype(BF16)
                    tiles.append((src.at[r0:r0 + tr, c0:c0 + tc], cast))
        _stream_tiles(tiles, stage_ref, sem_ref)

    o_gla = jnp.concatenate([og_ref[hh] for hh in range(GLA_HEADS)], axis=1)
    y_gla = _dot(o_gla, wog_ref[...])
    y_swa = _dot(os_ref[...], wos_ref[...])
    merged = ga_ref[...].astype(F32) * y_gla + gb_ref[...].astype(F32) * y_swa
    x1 = x_ref[...] + _dot(merged.astype(BF16), wout_ref[...])
    h2 = _rms(x1, g2_ref[...]).astype(BF16)
    acc = x1
    for c0 in range(0, D_FF, FFN_CHUNK):
        cols = slice(c0, c0 + FFN_CHUNK)
        gate = _dot(h2, wfi_ref[:, cols])
        up = _dot(h2, wfi_ref[:, D_FF + c0:D_FF + c0 + FFN_CHUNK])
        act = (jax.nn.silu(gate) * up).astype(BF16)
        acc = acc + _dot(act, wfo_ref[cols, :])
    o_ref[...] = acc


def _out_ffn(x2, o_gla, o_swa, ga, gb, norm_g, *weights):
    m = x2.shape[0]
    tm = OUT_TILE_M
    row = pl.BlockSpec((tm, D_MODEL), lambda i: (i, 0))
    return pl.pallas_call(
        _out_ffn_kernel,
        grid=(m // tm,),
        in_specs=[row, pl.BlockSpec((GLA_HEADS, tm, GLA_DV), lambda i: (0, i, 0))] + [row] * 3
                 + [_const_spec(norm_g.shape)]
                 + [pl.BlockSpec(memory_space=pl.ANY)] * len(weights),
        out_specs=row,
        out_shape=jax.ShapeDtypeStruct((m, D_MODEL), F32),
        scratch_shapes=[pltpu.VMEM(w.shape, BF16) for w in weights]
                       + [pltpu.VMEM((STAGE_SLOTS,) + OUT_STAGE_TILE, F32),
                          pltpu.SemaphoreType.DMA((STAGE_SLOTS,))],
        compiler_params=pltpu.CompilerParams(
            dimension_semantics=("arbitrary",), vmem_limit_bytes=VMEM_LIMIT),
        name="out_ffn",
    )(x2, o_gla, o_swa, ga, gb, norm_g, *weights)


def _rope_tables(seq):
    half = SWA_HEAD_DIM // 2
    inv_freq = ROPE_THETA ** (-np.arange(half, dtype=np.float64) / half)
    ang = np.arange(seq, dtype=np.float64)[:, None] * inv_freq[None, :]
    cos, sin = np.cos(ang), np.sin(ang)
    return (jnp.asarray(np.concatenate([cos, cos], axis=-1), F32),
            jnp.asarray(np.concatenate([-sin, sin], axis=-1), F32))


def _gate_up(up_f, up_b):
    zeros = jnp.zeros_like(up_f)
    top = jnp.concatenate([up_f, zeros], axis=1)
    bot = jnp.concatenate([zeros, up_b], axis=1)
    pad = jnp.zeros((LANES - 2 * GLA_GATE_RANK, 2 * GLA_QK_W), up_f.dtype)
    return jnp.concatenate([top, bot, pad], axis=0).astype(BF16)


def _layer(x2, batch, seq, norm_mix_g, w_in, up_f, bias_f, up_b, bias_b, gla_out_g, w_o_gla,
           q_g, k_g, sinks, w_o_swa, w_out, norm_ffn_g, w_ffn_in, w_ffn_out):
    cos, sin = _rope_tables(seq)
    row = lambda t: t.reshape(1, -1)
    qd, ki, kt, tot, v, r, sq, sk, sv, ga, gb = _in_projection(
        x2, row(norm_mix_g), cos, sin, row(q_g), row(k_g),
        (w_in.T, _gate_up(up_f, up_b), row(jnp.concatenate([bias_f, bias_b]))), seq)
    o_gla = _gla_branch(qd, ki, kt, tot, v, r, row(gla_out_g), batch, seq)
    o_swa = _swa_branch(sq, sk, sv, sinks, batch, seq)
    return _out_ffn(x2, o_gla, o_swa, ga, gb, row(norm_ffn_g),
                    w_o_gla, w_o_swa, w_out, w_ffn_in, w_ffn_out)


def kernel(x, norm_mix_g, w_in, gla_gate_up_fwd, gla_gate_bias_fwd, gla_gate_up_bwd,
           gla_gate_bias_bwd, gla_out_norm_g, w_o_gla, swa_q_norm_g, swa_k_norm_g,
           swa_sinks, w_o_swa, w_out, norm_ffn_g, w_ffn_in, w_ffn_out):
    batch, seq, d = x.shape
    x2 = x.reshape(batch * seq, d)
    for l in range(w_in.shape[0]):
        x2 = _layer(x2, batch, seq, norm_mix_g[l], w_in[l], gla_gate_up_fwd[l],
                    gla_gate_bias_fwd[l], gla_gate_up_bwd[l], gla_gate_bias_bwd[l],
                    gla_out_norm_g[l], w_o_gla[l], swa_q_norm_g[l], swa_k_norm_g[l],
                    swa_sinks[l], w_o_swa[l], w_out[l], norm_ffn_g[l], w_ffn_in[l],
                    w_ffn_out[l])
    return x2.reshape(batch, seq, d)
```

```python
import math

import jax
import jax.numpy as jnp
import numpy as np
from jax import lax
from jax.experimental import pallas as pl
from jax.experimental.pallas import tpu as pltpu

F32 = jnp.float32
BF16 = jnp.bfloat16

D_MODEL = 1024
NORM_EPS = 1e-6
GLA_HEADS = 4
GLA_DK = 128
GLA_DV = 256
GLA_QK_W = GLA_HEADS * GLA_DK
GLA_V_W = GLA_HEADS * GLA_DV
GLA_GATE_RANK = 16
GLA_GATE_NORMALIZER = 16.0
GLA_LOG_GATE_MIN = -0.5
GLA_CHUNK = 128
GLA_STATE_UNROLL = 8
GLA_OUT_UNROLL = 4
SWA_HEADS = 8
SWA_KV_HEADS = 2
SWA_GROUP = SWA_HEADS // SWA_KV_HEADS
SWA_HEAD_DIM = 128
SWA_Q_W = SWA_HEADS * SWA_HEAD_DIM
SWA_KV_W = SWA_KV_HEADS * SWA_HEAD_DIM
SWA_BLOCK = 128
ROPE_THETA = 10000.0
D_FF = 2816
IN_SPLITS = (GLA_QK_W, GLA_QK_W, GLA_V_W, GLA_V_W, GLA_GATE_RANK, GLA_GATE_RANK,
             SWA_Q_W, SWA_KV_W, SWA_KV_W, D_MODEL, D_MODEL)

LANES = 128
W_IN_OFFSETS = tuple(int(o) for o in np.cumsum((0,) + IN_SPLITS))
W_IN_CODES = W_IN_OFFSETS[4]
W_IN_TAIL = W_IN_OFFSETS[6]
MXU_COLS = 256
MASK_VALUE = -1e30
LOG2_E = math.log2(math.e)
VMEM_LIMIT = 56 * 1024 * 1024

IN_TILE_M = 512
SWA_TILE_Q = 512
SWA_UNROLL = 4
OUT_TILE_M = 512
FFN_CHUNK = D_FF // 2
STAGE_SLOTS = 8
OUT_STAGE_TILE = (256, 512)


def _const_spec(shape):
    zeros = (0,) * len(shape)
    return pl.BlockSpec(shape, lambda *_: zeros, pipeline_mode=pl.Buffered(1))


def _dot(a, b):
    return jnp.dot(a, b, preferred_element_type=F32)


def _dot_nt(a, b):
    return lax.dot_general(a, b, (((1,), (1,)), ((), ())), preferred_element_type=F32)


def _dot_tn(a, b):
    return lax.dot_general(a, b, (((0,), (0,)), ((), ())), preferred_element_type=F32)


def _rms(x, gain):
    ms = jnp.mean(x * x, axis=-1, keepdims=True)
    return x * lax.rsqrt(ms + NORM_EPS) * gain


def _stream_tiles(tiles, stage_ref, sem_ref):
    copies = [pltpu.make_async_copy(src, stage_ref.at[t % STAGE_SLOTS], sem_ref.at[t % STAGE_SLOTS])
              for t, (src, _) in enumerate(tiles)]
    depth = STAGE_SLOTS - 1
    for cp in copies[:depth]:
        cp.start()
    for t, (_, consume) in enumerate(tiles):
        if t + depth < len(tiles):
            copies[t + depth].start()
        copies[t].wait()
        consume(stage_ref.at[t % STAGE_SLOTS])


def _inproj_kernel(x_ref, g_ref, cos_ref, sin_ref, qg_ref, kg_ref, w_ref, up_ref, gbias_ref,
                   qd_ref, ki_ref, kt_ref, tot_ref, v_ref, r_ref, sq_ref, sk_ref, sv_ref,
                   ga_ref, gb_ref, whead_ref, wtail_ref, stage_ref, sem_ref):
    tm = x_ref.shape[0]
    c = GLA_CHUNK

    @pl.when(pl.program_id(0) == 0)
    def _():
        tiles = []
        for dst_ref, row0 in ((whead_ref, 0), (wtail_ref, W_IN_TAIL)):
            for c0 in range(0, dst_ref.shape[1], LANES):
                def cast(tile_ref, dst_ref=dst_ref, c0=c0):
                    dst_ref[:, c0:c0 + LANES] = tile_ref[...].T.astype(BF16)
                tiles.append((w_ref.at[row0 + c0:row0 + c0 + LANES, :], cast))
        _stream_tiles(tiles, stage_ref, sem_ref)

    w_ref = whead_ref
    wq_ref, wk_ref, wv_ref, wr_ref = [
        w_ref.at[:, a:b] for a, b in zip(W_IN_OFFSETS[:4], W_IN_OFFSETS[1:5])]
    wsq_ref, wsk_ref, wsv_ref, wga_ref, wgb_ref = [
        wtail_ref.at[:, a - W_IN_TAIL:b - W_IN_TAIL]
        for a, b in zip(W_IN_OFFSETS[6:-1], W_IN_OFFSETS[7:])]
    wlr_ref = w_ref.at[:, W_IN_CODES:W_IN_CODES + LANES]
    h = _rms(x_ref[...], g_ref[...]).astype(BF16)
    cos = cos_ref[...]
    sin = sin_ref[...]

    def norm_rope(t, gain, scale):
        y = _rms(t, gain)
        rot = pltpu.roll(y, SWA_HEAD_DIM // 2, axis=1)
        return ((y * cos + rot * sin) * scale).astype(BF16)

    hd = SWA_HEAD_DIM

    def rope_pair(w_ref, o_ref, gain_ref, scale, c0):
        t = _dot(h, w_ref[:, c0:c0 + MXU_COLS])
        for c1 in range(0, MXU_COLS, hd):
            o_ref[:, c0 + c1:c0 + c1 + hd] = norm_rope(t[:, c1:c1 + hd], gain_ref[...], scale)

    def v_head(hh):
        v_ref[hh] = _dot(h, wv_ref[:, hh * GLA_DV:(hh + 1) * GLA_DV]).astype(BF16)

    def r_head(hh):
        r_ref[hh] = jax.nn.silu(_dot(h, wr_ref[:, hh * GLA_DV:(hh + 1) * GLA_DV])).astype(BF16)

    rope_pair(wsk_ref, sk_ref, kg_ref, 1.0, 0)
    for pair in range(SWA_Q_W // MXU_COLS):
        rope_pair(wsq_ref, sq_ref, qg_ref, hd ** -0.5 * LOG2_E, pair * MXU_COLS)

    lr = _dot(h, wlr_ref[...]).astype(BF16)
    ga_ref[...] = jax.nn.sigmoid(_dot(h, wga_ref[...])).astype(BF16)
    z = _dot(lr, up_ref[...]) + gbias_ref[...]
    log2_g = jnp.maximum(
        (jnp.minimum(z, 0.0) * LOG2_E - jnp.log2(1.0 + jnp.exp2(jnp.abs(z) * -LOG2_E)))
        * (1.0 / GLA_GATE_NORMALIZER), GLA_LOG_GATE_MIN * LOG2_E)
    hi = log2_g.astype(BF16)
    lo = (log2_g - hi.astype(F32)).astype(BF16)

    v_head(0)
    v_head(1)
    gb_ref[...] = jax.nn.sigmoid(_dot(h, wgb_ref[...])).astype(BF16)
    q = _dot(h, wq_ref[...]) * (GLA_DK ** -0.5)
    k = _dot(h, wk_ref[...])
    v_head(2)
    v_head(3)

    ri = lax.broadcasted_iota(jnp.int32, (c, 2 * c), 0)
    ci = lax.broadcasted_iota(jnp.int32, (c, 2 * c), 1) % c
    tri = (jnp.where(ri >= ci, 1.0, 0.0).astype(BF16), jnp.where(ci >= ri, 1.0, 0.0).astype(BF16))
    total_row = (c - 1, 0)
    for t in range(tm // c):
        rows = slice(t * c, (t + 1) * c)
        totals = []
        for d in range(2):
            cols = slice(d * GLA_QK_W, (d + 1) * GLA_QK_W)
            b = _dot(tri[d], jnp.concatenate([hi[rows, cols], lo[rows, cols]], axis=0))
            decay = jnp.exp2(b)
            total = decay[total_row[d]:total_row[d] + 1, :]
            totals.append(total)
            q_dec = (q[rows, :] * decay).astype(BF16)
            k_inv = k[rows, :] * jnp.exp2(-b)
            k_tail = (k_inv * total).astype(BF16)
            k_inv = k_inv.astype(BF16)
            for hh in range(GLA_HEADS):
                src = slice(hh * GLA_DK, (hh + 1) * GLA_DK)
                dst = slice(d * GLA_DK, (d + 1) * GLA_DK)
                qd_ref[hh, rows, dst] = q_dec[:, src]
                ki_ref[hh, dst, rows] = k_inv[:, src].T
                kt_ref[hh, rows, dst] = k_tail[:, src]
        tot_ref[t] = jnp.concatenate(
            [totals[d][:, hh * GLA_DK:(hh + 1) * GLA_DK]
             for hh in range(GLA_HEADS) for d in range(2)], axis=1)
        if t < GLA_HEADS:
            r_head(t)

    for hh in range(tm // c, GLA_HEADS):
        r_head(hh)
    sv_ref[...] = _dot(h, wsv_ref[...]).astype(BF16)


def _in_projection(x2, norm_g, cos, sin, q_g, k_g, weights, seq):
    m = x2.shape[0]
    tm = IN_TILE_M
    pos_blocks = seq // tm
    row = lambda w: pl.BlockSpec((tm, w), lambda i: (i, 0))
    pos = pl.BlockSpec((tm, SWA_HEAD_DIM), lambda i: (i % pos_blocks, 0))
    head_major = (pl.BlockSpec((GLA_HEADS, tm, GLA_DV), lambda i: (0, i, 0)),
                  jax.ShapeDtypeStruct((GLA_HEADS, m, GLA_DV), BF16))
    tot = (pl.BlockSpec((tm // GLA_CHUNK, 1, 2 * GLA_QK_W), lambda i: (i, 0, 0)),
           jax.ShapeDtypeStruct((m // GLA_CHUNK, 1, 2 * GLA_QK_W), F32))
    flat = lambda w: (row(w), jax.ShapeDtypeStruct((m, w), BF16))
    ki_t = (pl.BlockSpec((GLA_HEADS, 2 * GLA_DK, tm), lambda i: (0, 0, i)),
            jax.ShapeDtypeStruct((GLA_HEADS, 2 * GLA_DK, m), BF16))
    outs = (head_major, ki_t, head_major, tot, head_major, head_major,
            flat(SWA_Q_W), flat(SWA_KV_W), flat(SWA_KV_W), flat(D_MODEL), flat(D_MODEL))
    return pl.pallas_call(
        _inproj_kernel,
        grid=(m // tm,),
        in_specs=[row(D_MODEL), _const_spec((1, D_MODEL)), pos, pos,
                  _const_spec((1, SWA_HEAD_DIM)), _const_spec((1, SWA_HEAD_DIM)),
                  pl.BlockSpec(memory_space=pl.ANY)]
                 + [_const_spec(w.shape) for w in weights[1:]],
        out_specs=[spec for spec, _ in outs],
        out_shape=[shape for _, shape in outs],
        scratch_shapes=[pltpu.VMEM((D_MODEL, W_IN_CODES + LANES), BF16),
                        pltpu.VMEM((D_MODEL, W_IN_OFFSETS[-1] - W_IN_TAIL), BF16),
                        pltpu.VMEM((STAGE_SLOTS, LANES, D_MODEL), F32),
                        pltpu.SemaphoreType.DMA((STAGE_SLOTS,))],
        compiler_params=pltpu.CompilerParams(
            dimension_semantics=("arbitrary",), vmem_limit_bytes=VMEM_LIMIT),
        name="in_projection",
    )(x2, norm_g, cos, sin, q_g, k_g, *weights)


def _gla_kernel(qd_ref, ki_ref, kt_ref, tot_ref, v_ref, r_ref, og_ref,
                o_ref, sf_ref, kvb_ref, state_ref, oraw_ref):
    c = GLA_CHUNK
    dk = GLA_DK
    n_chunks = v_ref.shape[0] // c
    lower_incl = (lax.broadcasted_iota(jnp.int32, (c, c), 0)
                  >= lax.broadcasted_iota(jnp.int32, (c, c), 1))
    rows_of = lambda n: pl.ds(n * c if isinstance(n, int) else pl.multiple_of(n * c, c), c)

    state_ref[...] = jnp.zeros_like(state_ref)

    def sweep_right(i, carry):
        chunks = [i * GLA_STATE_UNROLL + t for t in range(GLA_STATE_UNROLL)]
        kv_t = [_dot_tn(v_ref[rows_of(n), :], kt_ref[rows_of(n), :]) for n in chunks]
        s = state_ref[...]
        for n, kv in zip(chunks, kv_t):
            sf_ref[n] = s.astype(BF16).T
            kvb_ref[n] = kv[:, dk:]
            s = s * tot_ref[n][:, :dk] + kv[:, :dk]
        state_ref[...] = s
        return carry

    lax.fori_loop(0, n_chunks // GLA_STATE_UNROLL, sweep_right, 0)

    state_ref[...] = jnp.zeros_like(state_ref)
    u = GLA_OUT_UNROLL
    trips = n_chunks // u
    chunks_of = lambda i: [n_chunks - 1 - (i * u + t) for t in range(u)]

    def matmuls(i):
        chunks = chunks_of(i)
        rows = [rows_of(n) for n in chunks]
        qd = [qd_ref[r, :] for r in rows]
        ki_t = [ki_ref[:, r] for r in rows]
        a = [jnp.where(lower_incl, _dot(q[:, :dk], k[:dk, :]),
                       _dot(q[:, dk:], k[dk:, :])).astype(BF16) for q, k in zip(qd, ki_t)]
        s = state_ref[...]
        for j, n in enumerate(chunks):
            both = jnp.concatenate([sf_ref[n], s.astype(BF16).T], axis=0)
            oraw_ref[i % 2, j] = _dot(a[j], v_ref[rows[j], :]) + _dot(qd[j], both)
            s = s * tot_ref[n][:, dk:] + kvb_ref[n]
        state_ref[...] = s

    def tail(i):
        rows = [rows_of(n) for n in chunks_of(i)]
        finals = [(_rms(oraw_ref[i % 2, j], og_ref[...])
                   * r_ref[r, :].astype(F32)).astype(BF16)
                  for j, r in enumerate(rows)]
        for r, o in zip(rows, finals):
            o_ref[r, :] = o

    def sweep_left(i, carry):
        tail(i - 1)
        matmuls(i)
        return carry

    matmuls(0)
    lax.fori_loop(1, trips, sweep_left, 0)
    tail(trips - 1)


def _gla_branch(qd, ki, kt, tot, v, r, out_g, batch, seq):
    n_chunks = seq // GLA_CHUNK
    assert 2 * GLA_DK == GLA_DV
    vv = pl.BlockSpec((None, seq, GLA_DV), lambda b, h: (h, b, 0))
    return pl.pallas_call(
        _gla_kernel,
        grid=(batch, GLA_HEADS),
        in_specs=[vv, pl.BlockSpec((None, 2 * GLA_DK, seq), lambda b, h: (h, 0, b)), vv,
                  pl.BlockSpec((n_chunks, 1, 2 * GLA_DK), lambda b, h: (b, 0, h)),
                  vv, vv, _const_spec((1, GLA_DV))],
        out_specs=vv,
        out_shape=jax.ShapeDtypeStruct((GLA_HEADS, batch * seq, GLA_DV), BF16),
        scratch_shapes=[pltpu.VMEM((n_chunks, GLA_DK, GLA_DV), BF16),
                        pltpu.VMEM((n_chunks, GLA_DV, GLA_DK), F32),
                        pltpu.VMEM((GLA_DV, GLA_DK), F32),
                        pltpu.VMEM((2, GLA_OUT_UNROLL, GLA_CHUNK, GLA_DV), F32)],
        compiler_params=pltpu.CompilerParams(
            dimension_semantics=("parallel", "parallel"), vmem_limit_bytes=VMEM_LIMIT),
        name="gla_branch",
    )(qd, ki, kt, tot, v, r, out_g)


def _swa_kernel(sink_ref, q_ref, k_ref, v_ref, o_ref):
    blk = SWA_BLOCK
    hd = SWA_HEAD_DIM
    tile_blocks = q_ref.shape[0] // blk
    seq_blocks = k_ref.shape[0] // blk
    tile = pl.program_id(1)
    rows_g = SWA_GROUP * blk
    qi = lax.broadcasted_iota(jnp.int32, (rows_g, blk), 0) % blk
    kj = lax.broadcasted_iota(jnp.int32, (rows_g, blk), 1)
    ones = jnp.ones((3 * blk, hd), BF16)
    head_cols = lambda hh: slice(hh * hd, (hh + 1) * hd)

    def step(i, carry):
        jobs = [(i * SWA_UNROLL + t, hk) for t in range(SWA_UNROLL) for hk in range(SWA_KV_HEADS)]
        q4, kv_rows, masks = [], [], []
        for t, hk in jobs:
            n = tile * tile_blocks + t
            q_rows = pl.ds(pl.multiple_of(t * blk, blk), blk)
            q4.append(jnp.concatenate(
                [q_ref[q_rows, head_cols(hk * SWA_GROUP + g)] for g in range(SWA_GROUP)], axis=0))
            kv_rows.append([pl.ds(pl.multiple_of(nn * blk, blk), blk) for nn in
                            (jnp.maximum(n - 1, 0), n, jnp.minimum(n + 1, seq_blocks - 1))])
            masks.append([kj >= qi + jnp.where(n > 0, 0, blk), None,
                          kj <= qi - jnp.where(n < seq_blocks - 1, 0, blk)])
        s = [[_dot_nt(q, k_ref[r, head_cols(hk)]) for r in rows]
             for q, rows, (_, hk) in zip(q4, kv_rows, jobs)]
        s = [[sj if mask is None else jnp.where(mask, sj, MASK_VALUE)
              for sj, mask in zip(sb, mb)] for sb, mb in zip(s, masks)]
        row_max = [jnp.broadcast_to(
            jnp.maximum(jnp.maximum(sb[0], sb[1]), sb[2]).max(axis=-1, keepdims=True),
            (rows_g, blk)) for sb in s]
        m, sink_w = [], []
        for rm, (_, hk) in zip(row_max, jobs):
            sinks = [sink_ref[hk * SWA_GROUP + g] * LOG2_E for g in range(SWA_GROUP)]
            mg = [jnp.maximum(rm[g * blk:(g + 1) * blk], sk) for g, sk in enumerate(sinks)]
            m.append(jnp.concatenate(mg, axis=0))
            sink_w.append(jnp.concatenate([jnp.exp2(sk - x) for sk, x in zip(sinks, mg)], axis=0))
        p = [jnp.concatenate([jnp.exp2(sj - mb).astype(BF16) for sj in sb], axis=1)
             for sb, mb in zip(s, m)]
        pv = [_dot(pj, jnp.concatenate(
                  [jnp.concatenate([v_ref[r, head_cols(hk)] for r in rows], axis=0), ones], axis=1))
              for pj, rows, (_, hk) in zip(p, kv_rows, jobs)]
        for acc, sw, (t, hk) in zip(pv, sink_w, jobs):
            o = acc[:, :hd] / (acc[:, hd:] + sw)
            q_rows = pl.ds(pl.multiple_of(t * blk, blk), blk)
            for g in range(SWA_GROUP):
                o_ref[q_rows, head_cols(hk * SWA_GROUP + g)] = (
                    o[g * blk:(g + 1) * blk, :].astype(BF16))
        return carry

    lax.fori_loop(0, tile_blocks // SWA_UNROLL, step, 0)


def _swa_branch(sq, sk, sv, sinks, batch, seq):
    tq = SWA_TILE_Q
    tiles = seq // tq
    return pl.pallas_call(
        _swa_kernel,
        grid=(batch, tiles),
        in_specs=[pl.BlockSpec(memory_space=pltpu.SMEM),
                  pl.BlockSpec((tq, SWA_Q_W), lambda b, t: (b * tiles + t, 0)),
                  pl.BlockSpec((seq, SWA_KV_W), lambda b, t: (b, 0)),
                  pl.BlockSpec((seq, SWA_KV_W), lambda b, t: (b, 0))],
        out_specs=pl.BlockSpec((tq, SWA_Q_W), lambda b, t: (b * tiles + t, 0)),
        out_shape=jax.ShapeDtypeStruct((batch * seq, SWA_Q_W), BF16),
        compiler_params=pltpu.CompilerParams(
            dimension_semantics=("parallel", "parallel"), vmem_limit_bytes=VMEM_LIMIT),
        name="swa_branch",
    )(sinks, sq, sk, sv)


def _out_ffn_kernel(x_ref, og_ref, os_ref, ga_ref, gb_ref, g2_ref,
                    wog_hbm, wos_hbm, wout_hbm, wfi_hbm, wfo_hbm, o_ref,
                    wog_ref, wos_ref, wout_ref, wfi_ref, wfo_ref, stage_ref, sem_ref):
    @pl.when(pl.program_id(0) == 0)
    def _():
        tr, tc = stage_ref.shape[1:]
        tiles = []
        for src, dst in ((wog_hbm, wog_ref), (wos_hbm, wos_ref), (wout_hbm, wout_ref),
                         (wfi_hbm, wfi_ref), (wfo_hbm, wfo_ref)):
            for r0 in range(0, src.shape[0], tr):
                for c0 in range(0, src.shape[1], tc):
                    def cast(tile_ref, dst=dst, r0=r0, c0=c0):
                        dst[r0:r0 + tr, c0:c0 + tc] = tile_ref[...].astype(BF16)
                    tiles.append((src.at[r0:r0 + tr, c0:c0 + tc], cast))
        _stream_tiles(tiles, stage_ref, sem_ref)

    o_gla = jnp.concatenate([og_ref[hh] for hh in range(GLA_HEADS)], axis=1)
    y_gla = _dot(o_gla, wog_ref[...])
    y_swa = _dot(os_ref[...], wos_ref[...])
    merged = ga_ref[...].astype(F32) * y_gla + gb_ref[...].astype(F32) * y_swa
    x1 = x_ref[...] + _dot(merged.astype(BF16), wout_ref[...])
    h2 = _rms(x1, g2_ref[...]).astype(BF16)
    acc = x1
    for c0 in range(0, D_FF, FFN_CHUNK):
        cols = slice(c0, c0 + FFN_CHUNK)
        gate = _dot(h2, wfi_ref[:, cols])
        up = _dot(h2, wfi_ref[:, D_FF + c0:D_FF + c0 + FFN_CHUNK])
        act = (jax.nn.silu(gate) * up).astype(BF16)
        acc = acc + _dot(act, wfo_ref[cols, :])
    o_ref[...] = acc


def _out_ffn(x2, o_gla, o_swa, ga, gb, norm_g, *weights):
    m = x2.shape[0]
    tm = OUT_TILE_M
    row = pl.BlockSpec((tm, D_MODEL), lambda i: (i, 0))
    return pl.pallas_call(
        _out_ffn_kernel,
        grid=(m // tm,),
        in_specs=[row, pl.BlockSpec((GLA_HEADS, tm, GLA_DV), lambda i: (0, i, 0))] + [row] * 3
                 + [_const_spec(norm_g.shape)]
                 + [pl.BlockSpec(memory_space=pl.ANY)] * len(weights),
        out_specs=row,
        out_shape=jax.ShapeDtypeStruct((m, D_MODEL), F32),
        scratch_shapes=[pltpu.VMEM(w.shape, BF16) for w in weights]
                       + [pltpu.VMEM((STAGE_SLOTS,) + OUT_STAGE_TILE, F32),
                          pltpu.SemaphoreType.DMA((STAGE_SLOTS,))],
        compiler_params=pltpu.CompilerParams(
            dimension_semantics=("arbitrary",), vmem_limit_bytes=VMEM_LIMIT),
        name="out_ffn",
    )(x2, o_gla, o_swa, ga, gb, norm_g, *weights)


def _rope_tables(seq):
    half = SWA_HEAD_DIM // 2
    inv_freq = ROPE_THETA ** (-np.arange(half, dtype=np.float64) / half)
    ang = np.arange(seq, dtype=np.float64)[:, None] * inv_freq[None, :]
    cos, sin = np.cos(ang), np.sin(ang)
    return (jnp.asarray(np.concatenate([cos, cos], axis=-1), F32),
            jnp.asarray(np.concatenate([-sin, sin], axis=-1), F32))


def _gate_up(up_f, up_b):
    zeros = jnp.zeros_like(up_f)
    top = jnp.concatenate([up_f, zeros], axis=1)
    bot = jnp.concatenate([zeros, up_b], axis=1)
    pad = jnp.zeros((LANES - 2 * GLA_GATE_RANK, 2 * GLA_QK_W), up_f.dtype)
    return jnp.concatenate([top, bot, pad], axis=0).astype(BF16)


def _layer(x2, batch, seq, norm_mix_g, w_in, up_f, bias_f, up_b, bias_b, gla_out_g, w_o_gla,
           q_g, k_g, sinks, w_o_swa, w_out, norm_ffn_g, w_ffn_in, w_ffn_out):
    cos, sin = _rope_tables(seq)
    row = lambda t: t.reshape(1, -1)
    qd, ki, kt, tot, v, r, sq, sk, sv, ga, gb = _in_projection(
        x2, row(norm_mix_g), cos, sin, row(q_g), row(k_g),
        (w_in.T, _gate_up(up_f, up_b), row(jnp.concatenate([bias_f, bias_b]))), seq)
    o_gla = _gla_branch(qd, ki, kt, tot, v, r, row(gla_out_g), batch, seq)
    o_swa = _swa_branch(sq, sk, sv, sinks, batch, seq)
    return _out_ffn(x2, o_gla, o_swa, ga, gb, row(norm_ffn_g),
                    w_o_gla, w_o_swa, w_out, w_ffn_in, w_ffn_out)


def kernel(x, norm_mix_g, w_in, gla_gate_up_fwd, gla_gate_bias_fwd, gla_gate_up_bwd,
           gla_gate_bias_bwd, gla_out_norm_g, w_o_gla, swa_q_norm_g, swa_k_norm_g,
           swa_sinks, w_o_swa, w_out, norm_ffn_g, w_ffn_in, w_ffn_out):
    batch, seq, d = x.shape
    x2 = x.reshape(batch * seq, d)
    for l in range(w_in.shape[0]):
        x2 = _layer(x2, batch, seq, norm_mix_g[l], w_in[l], gla_gate_up_fwd[l],
                    gla_gate_bias_fwd[l], gla_gate_up_bwd[l], gla_gate_bias_bwd[l],
                    gla_out_norm_g[l], w_o_gla[l], swa_q_norm_g[l], swa_k_norm_g[l],
                    swa_sinks[l], w_o_swa[l], w_out[l], norm_ffn_g[l], w_ffn_in[l],
                    w_ffn_out[l])
    return x2.reshape(batch, seq, d)
```

```python
import math

import jax
import jax.numpy as jnp
import numpy as np
from jax import lax
from jax.experimental import pallas as pl
from jax.experimental.pallas import tpu as pltpu

F32 = jnp.float32
BF16 = jnp.bfloat16

D_MODEL = 1024
NORM_EPS = 1e-6
GLA_HEADS = 4
GLA_DK = 128
GLA_DV = 256
GLA_QK_W = GLA_HEADS * GLA_DK
GLA_V_W = GLA_HEADS * GLA_DV
GLA_GATE_RANK = 16
GLA_GATE_NORMALIZER = 16.0
GLA_LOG_GATE_MIN = -0.5
GLA_CHUNK = 128
GLA_STATE_UNROLL = 8
GLA_OUT_UNROLL = 8
SWA_HEADS = 8
SWA_KV_HEADS = 2
SWA_GROUP = SWA_HEADS // SWA_KV_HEADS
SWA_HEAD_DIM = 128
SWA_Q_W = SWA_HEADS * SWA_HEAD_DIM
SWA_KV_W = SWA_KV_HEADS * SWA_HEAD_DIM
SWA_BLOCK = 128
ROPE_THETA = 10000.0
D_FF = 2816
IN_SPLITS = (GLA_QK_W, GLA_QK_W, GLA_V_W, GLA_V_W, GLA_GATE_RANK, GLA_GATE_RANK,
             SWA_Q_W, SWA_KV_W, SWA_KV_W, D_MODEL, D_MODEL)

LANES = 128
W_IN_OFFSETS = tuple(int(o) for o in np.cumsum((0,) + IN_SPLITS))
W_IN_CODES = W_IN_OFFSETS[4]
W_IN_TAIL = W_IN_OFFSETS[6]
MXU_COLS = 256
MASK_VALUE = -1e30
LOG2_E = math.log2(math.e)
VMEM_LIMIT = 56 * 1024 * 1024

IN_TILE_M = 512
SWA_TILE_Q = 1024
SWA_UNROLL = 4
OUT_TILE_M = 512
FFN_CHUNK = D_FF // 2
STAGE_SLOTS = 8
OUT_STAGE_TILE = (256, 512)


def _const_spec(shape):
    zeros = (0,) * len(shape)
    return pl.BlockSpec(shape, lambda *_: zeros, pipeline_mode=pl.Buffered(1))


def _dot(a, b):
    return jnp.dot(a, b, preferred_element_type=F32)


def _dot_nt(a, b):
    return lax.dot_general(a, b, (((1,), (1,)), ((), ())), preferred_element_type=F32)


def _dot_tn(a, b):
    return lax.dot_general(a, b, (((0,), (0,)), ((), ())), preferred_element_type=F32)


def _rms(x, gain):
    ms = jnp.mean(x * x, axis=-1, keepdims=True)
    return x * lax.rsqrt(ms + NORM_EPS) * gain


def _stream_tiles(tiles, stage_ref, sem_ref):
    copies = [pltpu.make_async_copy(src, stage_ref.at[t % STAGE_SLOTS], sem_ref.at[t % STAGE_SLOTS])
              for t, (src, _) in enumerate(tiles)]
    depth = STAGE_SLOTS - 1
    for cp in copies[:depth]:
        cp.start()
    for t, (_, consume) in enumerate(tiles):
        if t + depth < len(tiles):
            copies[t + depth].start()
        copies[t].wait()
        consume(stage_ref.at[t % STAGE_SLOTS])


def _inproj_kernel(x_ref, g_ref, cos_ref, sin_ref, qg_ref, kg_ref, w_ref, up_ref, gbias_ref,
                   qd_ref, ki_ref, kt_ref, tot_ref, v_ref, r_ref, sq_ref, sk_ref, sv_ref,
                   ga_ref, gb_ref, whead_ref, wtail_ref, stage_ref, sem_ref):
    tm = x_ref.shape[0]
    c = GLA_CHUNK

    @pl.when(pl.program_id(0) == 0)
    def _():
        tiles = []
        for dst_ref, row0 in ((whead_ref, 0), (wtail_ref, W_IN_TAIL)):
            for c0 in range(0, dst_ref.shape[1], LANES):
                def cast(tile_ref, dst_ref=dst_ref, c0=c0):
                    dst_ref[:, c0:c0 + LANES] = tile_ref[...].T.astype(BF16)
                tiles.append((w_ref.at[row0 + c0:row0 + c0 + LANES, :], cast))
        _stream_tiles(tiles, stage_ref, sem_ref)

    w_ref = whead_ref
    wq_ref, wk_ref, wv_ref, wr_ref = [
        w_ref.at[:, a:b] for a, b in zip(W_IN_OFFSETS[:4], W_IN_OFFSETS[1:5])]
    wsq_ref, wsk_ref, wsv_ref, wga_ref, wgb_ref = [
        wtail_ref.at[:, a - W_IN_TAIL:b - W_IN_TAIL]
        for a, b in zip(W_IN_OFFSETS[6:-1], W_IN_OFFSETS[7:])]
    wlr_ref = w_ref.at[:, W_IN_CODES:W_IN_CODES + LANES]
    h = _rms(x_ref[...], g_ref[...]).astype(BF16)
    cos = cos_ref[...]
    sin = sin_ref[...]

    def norm_rope(t, gain, scale):
        y = _rms(t, gain)
        rot = pltpu.roll(y, SWA_HEAD_DIM // 2, axis=1)
        return ((y * cos + rot * sin) * scale).astype(BF16)

    hd = SWA_HEAD_DIM

    def rope_pair(w_ref, o_ref, gain_ref, scale, c0):
        t = _dot(h, w_ref[:, c0:c0 + MXU_COLS])
        for c1 in range(0, MXU_COLS, hd):
            o_ref[:, c0 + c1:c0 + c1 + hd] = norm_rope(t[:, c1:c1 + hd], gain_ref[...], scale)

    def v_head(hh):
        v_ref[hh] = _dot(h, wv_ref[:, hh * GLA_DV:(hh + 1) * GLA_DV]).astype(BF16)

    def r_head(hh):
        r_ref[hh] = jax.nn.silu(_dot(h, wr_ref[:, hh * GLA_DV:(hh + 1) * GLA_DV])).astype(BF16)

    rope_pair(wsk_ref, sk_ref, kg_ref, 1.0, 0)
    for pair in range(SWA_Q_W // MXU_COLS):
        rope_pair(wsq_ref, sq_ref, qg_ref, hd ** -0.5 * LOG2_E, pair * MXU_COLS)

    lr = _dot(h, wlr_ref[...]).astype(BF16)
    ga_ref[...] = jax.nn.sigmoid(_dot(h, wga_ref[...])).astype(BF16)
    z = _dot(lr, up_ref[...]) + gbias_ref[...]
    log2_g = jnp.maximum(
        (jnp.minimum(z, 0.0) * LOG2_E - jnp.log2(1.0 + jnp.exp2(jnp.abs(z) * -LOG2_E)))
        * (1.0 / GLA_GATE_NORMALIZER), GLA_LOG_GATE_MIN * LOG2_E)
    hi = log2_g.astype(BF16)
    lo = (log2_g - hi.astype(F32)).astype(BF16)

    v_head(0)
    v_head(1)
    gb_ref[...] = jax.nn.sigmoid(_dot(h, wgb_ref[...])).astype(BF16)
    q = _dot(h, wq_ref[...]) * (GLA_DK ** -0.5)
    k = _dot(h, wk_ref[...])
    v_head(2)
    v_head(3)

    ri = lax.broadcasted_iota(jnp.int32, (c, 2 * c), 0)
    ci = lax.broadcasted_iota(jnp.int32, (c, 2 * c), 1) % c
    tri = (jnp.where(ri >= ci, 1.0, 0.0).astype(BF16), jnp.where(ci >= ri, 1.0, 0.0).astype(BF16))
    total_row = (c - 1, 0)
    for t in range(tm // c):
        rows = slice(t * c, (t + 1) * c)
        totals = []
        for d in range(2):
            cols = slice(d * GLA_QK_W, (d + 1) * GLA_QK_W)
            b = _dot(tri[d], jnp.concatenate([hi[rows, cols], lo[rows, cols]], axis=0))
            decay = jnp.exp2(b)
            total = decay[total_row[d]:total_row[d] + 1, :]
            totals.append(total)
            q_dec = (q[rows, :] * decay).astype(BF16)
            k_inv = k[rows, :] * jnp.exp2(-b)
            k_tail = (k_inv * total).astype(BF16)
            k_inv = k_inv.astype(BF16)
            for hh in range(GLA_HEADS):
                src = slice(hh * GLA_DK, (hh + 1) * GLA_DK)
                dst = slice(d * GLA_DK, (d + 1) * GLA_DK)
                qd_ref[hh, rows, dst] = q_dec[:, src]
                ki_ref[hh, dst, rows] = k_inv[:, src].T
                kt_ref[hh, rows, dst] = k_tail[:, src]
        tot_ref[t] = jnp.concatenate(
            [totals[d][:, hh * GLA_DK:(hh + 1) * GLA_DK]
             for hh in range(GLA_HEADS) for d in range(2)], axis=1)
        if t < GLA_HEADS:
            r_head(t)

    for hh in range(tm // c, GLA_HEADS):
        r_head(hh)
    sv_ref[...] = _dot(h, wsv_ref[...]).astype(BF16)


def _in_projection(x2, norm_g, cos, sin, q_g, k_g, weights, seq):
    m = x2.shape[0]
    tm = IN_TILE_M
    pos_blocks = seq // tm
    row = lambda w: pl.BlockSpec((tm, w), lambda i: (i, 0))
    pos = pl.BlockSpec((tm, SWA_HEAD_DIM), lambda i: (i % pos_blocks, 0))
    head_major = (pl.BlockSpec((GLA_HEADS, tm, GLA_DV), lambda i: (0, i, 0)),
                  jax.ShapeDtypeStruct((GLA_HEADS, m, GLA_DV), BF16))
    tot = (pl.BlockSpec((tm // GLA_CHUNK, 1, 2 * GLA_QK_W), lambda i: (i, 0, 0)),
           jax.ShapeDtypeStruct((m // GLA_CHUNK, 1, 2 * GLA_QK_W), F32))
    flat = lambda w: (row(w), jax.ShapeDtypeStruct((m, w), BF16))
    ki_t = (pl.BlockSpec((GLA_HEADS, 2 * GLA_DK, tm), lambda i: (0, 0, i)),
            jax.ShapeDtypeStruct((GLA_HEADS, 2 * GLA_DK, m), BF16))
    outs = (head_major, ki_t, head_major, tot, head_major, head_major,
            flat(SWA_Q_W), flat(SWA_KV_W), flat(SWA_KV_W), flat(D_MODEL), flat(D_MODEL))
    return pl.pallas_call(
        _inproj_kernel,
        grid=(m // tm,),
        in_specs=[row(D_MODEL), _const_spec((1, D_MODEL)), pos, pos,
                  _const_spec((1, SWA_HEAD_DIM)), _const_spec((1, SWA_HEAD_DIM)),
                  pl.BlockSpec(memory_space=pl.ANY)]
                 + [_const_spec(w.shape) for w in weights[1:]],
        out_specs=[spec for spec, _ in outs],
        out_shape=[shape for _, shape in outs],
        scratch_shapes=[pltpu.VMEM((D_MODEL, W_IN_CODES + LANES), BF16),
                        pltpu.VMEM((D_MODEL, W_IN_OFFSETS[-1] - W_IN_TAIL), BF16),
                        pltpu.VMEM((STAGE_SLOTS, LANES, D_MODEL), F32),
                        pltpu.SemaphoreType.DMA((STAGE_SLOTS,))],
        compiler_params=pltpu.CompilerParams(
            dimension_semantics=("arbitrary",), vmem_limit_bytes=VMEM_LIMIT),
        name="in_projection",
    )(x2, norm_g, cos, sin, q_g, k_g, *weights)


def _gla_kernel(qd_ref, ki_ref, kt_ref, tot_ref, v_ref, r_ref, og_ref,
                o_ref, sf_ref, kvb_ref, state_ref, oraw_ref):
    c = GLA_CHUNK
    dk = GLA_DK
    n_chunks = v_ref.shape[0] // c
    lower_incl = (lax.broadcasted_iota(jnp.int32, (c, c), 0)
                  >= lax.broadcasted_iota(jnp.int32, (c, c), 1))
    rows_of = lambda n: pl.ds(n * c if isinstance(n, int) else pl.multiple_of(n * c, c), c)

    state_ref[...] = jnp.zeros_like(state_ref)

    def sweep_right(i, carry):
        chunks = [i * GLA_STATE_UNROLL + t for t in range(GLA_STATE_UNROLL)]
        kv_t = [_dot_tn(v_ref[rows_of(n), :], kt_ref[rows_of(n), :]) for n in chunks]
        s = state_ref[...]
        for n, kv in zip(chunks, kv_t):
            sf_ref[n] = s.astype(BF16).T
            kvb_ref[n] = kv[:, dk:]
            s = s * tot_ref[n][:, :dk] + kv[:, :dk]
        state_ref[...] = s
        return carry

    lax.fori_loop(0, n_chunks // GLA_STATE_UNROLL, sweep_right, 0)

    state_ref[...] = jnp.zeros_like(state_ref)
    u = GLA_OUT_UNROLL
    trips = n_chunks // u
    chunks_of = lambda i: [n_chunks - 1 - (i * u + t) for t in range(u)]

    def matmuls(i):
        chunks = chunks_of(i)
        rows = [rows_of(n) for n in chunks]
        qd = [qd_ref[r, :] for r in rows]
        ki_t = [ki_ref[:, r] for r in rows]
        a = [jnp.where(lower_incl, _dot(q[:, :dk], k[:dk, :]),
                       _dot(q[:, dk:], k[dk:, :])).astype(BF16) for q, k in zip(qd, ki_t)]
        s = state_ref[...]
        for j, n in enumerate(chunks):
            both = jnp.concatenate([sf_ref[n], s.astype(BF16).T], axis=0)
            oraw_ref[i % 2, j] = _dot(a[j], v_ref[rows[j], :]) + _dot(qd[j], both)
            s = s * tot_ref[n][:, dk:] + kvb_ref[n]
        state_ref[...] = s

    def tail(i):
        rows = [rows_of(n) for n in chunks_of(i)]
        finals = [(_rms(oraw_ref[i % 2, j], og_ref[...])
                   * r_ref[r, :].astype(F32)).astype(BF16)
                  for j, r in enumerate(rows)]
        for r, o in zip(rows, finals):
            o_ref[r, :] = o

    def sweep_left(i, carry):
        tail(i - 1)
        matmuls(i)
        return carry

    matmuls(0)
    lax.fori_loop(1, trips, sweep_left, 0)
    tail(trips - 1)


def _gla_branch(qd, ki, kt, tot, v, r, out_g, batch, seq):
    n_chunks = seq // GLA_CHUNK
    assert 2 * GLA_DK == GLA_DV
    vv = pl.BlockSpec((None, seq, GLA_DV), lambda b, h: (h, b, 0))
    return pl.pallas_call(
        _gla_kernel,
        grid=(batch, GLA_HEADS),
        in_specs=[vv, pl.BlockSpec((None, 2 * GLA_DK, seq), lambda b, h: (h, 0, b)), vv,
                  pl.BlockSpec((n_chunks, 1, 2 * GLA_DK), lambda b, h: (b, 0, h)),
                  vv, vv, _const_spec((1, GLA_DV))],
        out_specs=vv,
        out_shape=jax.ShapeDtypeStruct((GLA_HEADS, batch * seq, GLA_DV), BF16),
        scratch_shapes=[pltpu.VMEM((n_chunks, GLA_DK, GLA_DV), BF16),
                        pltpu.VMEM((n_chunks, GLA_DV, GLA_DK), F32),
                        pltpu.VMEM((GLA_DV, GLA_DK), F32),
                        pltpu.VMEM((2, GLA_OUT_UNROLL, GLA_CHUNK, GLA_DV), F32)],
        compiler_params=pltpu.CompilerParams(
            dimension_semantics=("parallel", "parallel"), vmem_limit_bytes=VMEM_LIMIT),
        name="gla_branch",
    )(qd, ki, kt, tot, v, r, out_g)


def _swa_kernel(sink_ref, q_ref, k_ref, v_ref, o_ref):
    blk = SWA_BLOCK
    hd = SWA_HEAD_DIM
    tile_blocks = q_ref.shape[0] // blk
    seq_blocks = k_ref.shape[0] // blk
    tile = pl.program_id(1)
    rows_g = SWA_GROUP * blk
    qi = lax.broadcasted_iota(jnp.int32, (rows_g, blk), 0) % blk
    kj = lax.broadcasted_iota(jnp.int32, (rows_g, blk), 1)
    ones = jnp.ones((3 * blk, hd), BF16)
    head_cols = lambda hh: slice(hh * hd, (hh + 1) * hd)

    def step(i, carry):
        jobs = [(i * SWA_UNROLL + t, hk) for t in range(SWA_UNROLL) for hk in range(SWA_KV_HEADS)]
        q4, kv_rows, masks = [], [], []
        for t, hk in jobs:
            n = tile * tile_blocks + t
            q_rows = pl.ds(pl.multiple_of(t * blk, blk), blk)
            q4.append(jnp.concatenate(
                [q_ref[q_rows, head_cols(hk * SWA_GROUP + g)] for g in range(SWA_GROUP)], axis=0))
            kv_rows.append([pl.ds(pl.multiple_of(nn * blk, blk), blk) for nn in
                            (jnp.maximum(n - 1, 0), n, jnp.minimum(n + 1, seq_blocks - 1))])
            masks.append([kj >= qi + jnp.where(n > 0, 0, blk), None,
                          kj <= qi - jnp.where(n < seq_blocks - 1, 0, blk)])
        s = [[_dot_nt(q, k_ref[r, head_cols(hk)]) for r in rows]
             for q, rows, (_, hk) in zip(q4, kv_rows, jobs)]
        s = [[sj if mask is None else jnp.where(mask, sj, MASK_VALUE)
              for sj, mask in zip(sb, mb)] for sb, mb in zip(s, masks)]
        row_max = [jnp.broadcast_to(
            jnp.maximum(jnp.maximum(sb[0], sb[1]), sb[2]).max(axis=-1, keepdims=True),
            (rows_g, blk)) for sb in s]
        m, sink_w = [], []
        for rm, (_, hk) in zip(row_max, jobs):
            sinks = [sink_ref[hk * SWA_GROUP + g] * LOG2_E for g in range(SWA_GROUP)]
            mg = [jnp.maximum(rm[g * blk:(g + 1) * blk], sk) for g, sk in enumerate(sinks)]
            m.append(jnp.concatenate(mg, axis=0))
            sink_w.append(jnp.concatenate([jnp.exp2(sk - x) for sk, x in zip(sinks, mg)], axis=0))
        p = [jnp.concatenate([jnp.exp2(sj - mb).astype(BF16) for sj in sb], axis=1)
             for sb, mb in zip(s, m)]
        pv = [_dot(pj, jnp.concatenate(
                  [jnp.concatenate([v_ref[r, head_cols(hk)] for r in rows], axis=0), ones], axis=1))
              for pj, rows, (_, hk) in zip(p, kv_rows, jobs)]
        for acc, sw, (t, hk) in zip(pv, sink_w, jobs):
            o = acc[:, :hd] / (acc[:, hd:] + sw)
            q_rows = pl.ds(pl.multiple_of(t * blk, blk), blk)
            for g in range(SWA_GROUP):
                o_ref[q_rows, head_cols(hk * SWA_GROUP + g)] = (
                    o[g * blk:(g + 1) * blk, :].astype(BF16))
        return carry

    lax.fori_loop(0, tile_blocks // SWA_UNROLL, step, 0)


def _swa_branch(sq, sk, sv, sinks, batch, seq):
    tq = SWA_TILE_Q
    tiles = seq // tq
    return pl.pallas_call(
        _swa_kernel,
        grid=(batch, tiles),
        in_specs=[pl.BlockSpec(memory_space=pltpu.SMEM),
                  pl.BlockSpec((tq, SWA_Q_W), lambda b, t: (b * tiles + t, 0)),
                  pl.BlockSpec((seq, SWA_KV_W), lambda b, t: (b, 0)),
                  pl.BlockSpec((seq, SWA_KV_W), lambda b, t: (b, 0))],
        out_specs=pl.BlockSpec((tq, SWA_Q_W), lambda b, t: (b * tiles + t, 0)),
        out_shape=jax.ShapeDtypeStruct((batch * seq, SWA_Q_W), BF16),
        compiler_params=pltpu.CompilerParams(
            dimension_semantics=("parallel", "parallel"), vmem_limit_bytes=VMEM_LIMIT),
        name="swa_branch",
    )(sinks, sq, sk, sv)


def _out_ffn_kernel(x_ref, og_ref, os_ref, ga_ref, gb_ref, g2_ref,
                    wog_hbm, wos_hbm, wout_hbm, wfi_hbm, wfo_hbm, o_ref,
                    wog_ref, wos_ref, wout_ref, wfi_ref, wfo_ref, stage_ref, sem_ref):
    @pl.when(pl.program_id(0) == 0)
    def _():
        tr, tc = stage_ref.shape[1:]
        tiles = []
        for src, dst in ((wog_hbm, wog_ref), (wos_hbm, wos_ref), (wout_hbm, wout_ref),
                         (wfi_hbm, wfi_ref), (wfo_hbm, wfo_ref)):
            for r0 in range(0, src.shape[0], tr):
                for c0 in range(0, src.shape[1], tc):
                    def cast(tile_ref, dst=dst, r0=r0, c0=c0):
                        dst[r0:r0 + tr, c0:c0 + tc] = tile_ref[...].astype(BF16)
                    tiles.append((src.at[r0:r0 + tr, c0:c0 + tc], cast))
        _stream_tiles(tiles, stage_ref, sem_ref)

    o_gla = jnp.concatenate([og_ref[hh] for hh in range(GLA_HEADS)], axis=1)
    y_gla = _dot(o_gla, wog_ref[...])
    y_swa = _dot(os_ref[...], wos_ref[...])
    merged = ga_ref[...].astype(F32) * y_gla + gb_ref[...].astype(F32) * y_swa
    x1 = x_ref[...] + _dot(merged.astype(BF16), wout_ref[...])
    h2 = _rms(x1, g2_ref[...]).astype(BF16)
    acc = x1
    for c0 in range(0, D_FF, FFN_CHUNK):
        cols = slice(c0, c0 + FFN_CHUNK)
        gate = _dot(h2, wfi_ref[:, cols])
        up = _dot(h2, wfi_ref[:, D_FF + c0:D_FF + c0 + FFN_CHUNK])
        act = (jax.nn.silu(gate) * up).astype(BF16)
        acc = acc + _dot(act, wfo_ref[cols, :])
    o_ref[...] = acc


def _out_ffn(x2, o_gla, o_swa, ga, gb, norm_g, *weights):
    m = x2.shape[0]
    tm = OUT_TILE_M
    row = pl.BlockSpec((tm, D_MODEL), lambda i: (i, 0))
    return pl.pallas_call(
        _out_ffn_kernel,
        grid=(m // tm,),
        in_specs=[row, pl.BlockSpec((GLA_HEADS, tm, GLA_DV), lambda i: (0, i, 0))] + [row] * 3
                 + [_const_spec(norm_g.shape)]
                 + [pl.BlockSpec(memory_space=pl.ANY)] * len(weights),
        out_specs=row,
        out_shape=jax.ShapeDtypeStruct((m, D_MODEL), F32),
        scratch_shapes=[pltpu.VMEM(w.shape, BF16) for w in weights]
                       + [pltpu.VMEM((STAGE_SLOTS,) + OUT_STAGE_TILE, F32),
                          pltpu.SemaphoreType.DMA((STAGE_SLOTS,))],
        compiler_params=pltpu.CompilerParams(
            dimension_semantics=("arbitrary",), vmem_limit_bytes=VMEM_LIMIT),
        name="out_ffn",
    )(x2, o_gla, o_swa, ga, gb, norm_g, *weights)


def _rope_tables(seq):
    half = SWA_HEAD_DIM // 2
    inv_freq = ROPE_THETA ** (-np.arange(half, dtype=np.float64) / half)
    ang = np.arange(seq, dtype=np.float64)[:, None] * inv_freq[None, :]
    cos, sin = np.cos(ang), np.sin(ang)
    return (jnp.asarray(np.concatenate([cos, cos], axis=-1), F32),
            jnp.asarray(np.concatenate([-sin, sin], axis=-1), F32))


def _gate_up(up_f, up_b):
    zeros = jnp.zeros_like(up_f)
    top = jnp.concatenate([up_f, zeros], axis=1)
    bot = jnp.concatenate([zeros, up_b], axis=1)
    pad = jnp.zeros((LANES - 2 * GLA_GATE_RANK, 2 * GLA_QK_W), up_f.dtype)
    return jnp.concatenate([top, bot, pad], axis=0).astype(BF16)


def _layer(x2, batch, seq, norm_mix_g, w_in, up_f, bias_f, up_b, bias_b, gla_out_g, w_o_gla,
           q_g, k_g, sinks, w_o_swa, w_out, norm_ffn_g, w_ffn_in, w_ffn_out):
    cos, sin = _rope_tables(seq)
    row = lambda t: t.reshape(1, -1)
    qd, ki, kt, tot, v, r, sq, sk, sv, ga, gb = _in_projection(
        x2, row(norm_mix_g), cos, sin, row(q_g), row(k_g),
        (w_in.T, _gate_up(up_f, up_b), row(jnp.concatenate([bias_f, bias_b]))), seq)
    o_gla = _gla_branch(qd, ki, kt, tot, v, r, row(gla_out_g), batch, seq)
    o_swa = _swa_branch(sq, sk, sv, sinks, batch, seq)
    return _out_ffn(x2, o_gla, o_swa, ga, gb, row(norm_ffn_g),
                    w_o_gla, w_o_swa, w_out, w_ffn_in, w_ffn_out)


def kernel(x, norm_mix_g, w_in, gla_gate_up_fwd, gla_gate_bias_fwd, gla_gate_up_bwd,
           gla_gate_bias_bwd, gla_out_norm_g, w_o_gla, swa_q_norm_g, swa_k_norm_g,
           swa_sinks, w_o_swa, w_out, norm_ffn_g, w_ffn_in, w_ffn_out):
    batch, seq, d = x.shape
    x2 = x.reshape(batch * seq, d)
    for l in range(w_in.shape[0]):
        x2 = _layer(x2, batch, seq, norm_mix_g[l], w_in[l], gla_gate_up_fwd[l],
                    gla_gate_bias_fwd[l], gla_gate_up_bwd[l], gla_gate_bias_bwd[l],
                    gla_out_norm_g[l], w_o_gla[l], swa_q_norm_g[l], swa_k_norm_g[l],
                    swa_sinks[l], w_o_swa[l], w_out[l], norm_ffn_g[l], w_ffn_in[l],
                    w_ffn_out[l])
    return x2.reshape(batch, seq, d)
```

```python
import math

import jax
import jax.numpy as jnp
import numpy as np
from jax import lax
from jax.experimental import pallas as pl
from jax.experimental.pallas import tpu as pltpu

F32 = jnp.float32
BF16 = jnp.bfloat16

D_MODEL = 1024
NORM_EPS = 1e-6
GLA_HEADS = 4
GLA_DK = 128
GLA_DV = 256
GLA_QK_W = GLA_HEADS * GLA_DK
GLA_V_W = GLA_HEADS * GLA_DV
GLA_GATE_RANK = 16
GLA_GATE_NORMALIZER = 16.0
GLA_LOG_GATE_MIN = -0.5
GLA_CHUNK = 128
GLA_STATE_UNROLL = 32
GLA_OUT_UNROLL = 32
SWA_HEADS = 8
SWA_KV_HEADS = 2
SWA_GROUP = SWA_HEADS // SWA_KV_HEADS
SWA_HEAD_DIM = 128
SWA_Q_W = SWA_HEADS * SWA_HEAD_DIM
SWA_KV_W = SWA_KV_HEADS * SWA_HEAD_DIM
SWA_BLOCK = 128
ROPE_THETA = 10000.0
D_FF = 2816
IN_SPLITS = (GLA_QK_W, GLA_QK_W, GLA_V_W, GLA_V_W, GLA_GATE_RANK, GLA_GATE_RANK,
             SWA_Q_W, SWA_KV_W, SWA_KV_W, D_MODEL, D_MODEL)

LANES = 128
W_IN_OFFSETS = tuple(int(o) for o in np.cumsum((0,) + IN_SPLITS))
W_IN_CODES = W_IN_OFFSETS[4]
W_IN_TAIL = W_IN_OFFSETS[6]
MXU_COLS = 256
MASK_VALUE = -1e30
LOG2_E = math.log2(math.e)
VMEM_LIMIT = 56 * 1024 * 1024

IN_TILE_M = 512
SWA_TILE_Q = 1024
SWA_UNROLL = 4
OUT_TILE_M = 512
FFN_CHUNK = D_FF // 2
STAGE_SLOTS = 8
OUT_STAGE_TILE = (256, 512)


def _const_spec(shape):
    zeros = (0,) * len(shape)
    return pl.BlockSpec(shape, lambda *_: zeros, pipeline_mode=pl.Buffered(1))


def _dot(a, b):
    return jnp.dot(a, b, preferred_element_type=F32)


def _dot_nt(a, b):
    return lax.dot_general(a, b, (((1,), (1,)), ((), ())), preferred_element_type=F32)


def _dot_tn(a, b):
    return lax.dot_general(a, b, (((0,), (0,)), ((), ())), preferred_element_type=F32)


def _rms(x, gain):
    ms = jnp.mean(x * x, axis=-1, keepdims=True)
    return x * lax.rsqrt(ms + NORM_EPS) * gain


def _stream_tiles(tiles, stage_ref, sem_ref):
    copies = [pltpu.make_async_copy(src, stage_ref.at[t % STAGE_SLOTS], sem_ref.at[t % STAGE_SLOTS])
              for t, (src, _) in enumerate(tiles)]
    depth = STAGE_SLOTS - 1
    for cp in copies[:depth]:
        cp.start()
    for t, (_, consume) in enumerate(tiles):
        if t + depth < len(tiles):
            copies[t + depth].start()
        copies[t].wait()
        consume(stage_ref.at[t % STAGE_SLOTS])


def _inproj_kernel(x_ref, g_ref, cos_ref, sin_ref, qg_ref, kg_ref, w_ref, up_ref, gbias_ref,
                   qd_ref, ki_ref, kt_ref, tot_ref, v_ref, r_ref, sq_ref, sk_ref, sv_ref,
                   ga_ref, gb_ref, whead_ref, wtail_ref, stage_ref, sem_ref):
    tm = x_ref.shape[0]
    c = GLA_CHUNK

    @pl.when(pl.program_id(0) == 0)
    def _():
        tiles = []
        for dst_ref, row0 in ((whead_ref, 0), (wtail_ref, W_IN_TAIL)):
            for c0 in range(0, dst_ref.shape[1], LANES):
                def cast(tile_ref, dst_ref=dst_ref, c0=c0):
                    dst_ref[:, c0:c0 + LANES] = tile_ref[...].T.astype(BF16)
                tiles.append((w_ref.at[row0 + c0:row0 + c0 + LANES, :], cast))
        _stream_tiles(tiles, stage_ref, sem_ref)

    w_ref = whead_ref
    wq_ref, wk_ref, wv_ref, wr_ref = [
        w_ref.at[:, a:b] for a, b in zip(W_IN_OFFSETS[:4], W_IN_OFFSETS[1:5])]
    wsq_ref, wsk_ref, wsv_ref, wga_ref, wgb_ref = [
        wtail_ref.at[:, a - W_IN_TAIL:b - W_IN_TAIL]
        for a, b in zip(W_IN_OFFSETS[6:-1], W_IN_OFFSETS[7:])]
    wlr_ref = w_ref.at[:, W_IN_CODES:W_IN_CODES + LANES]
    h = _rms(x_ref[...], g_ref[...]).astype(BF16)
    cos = cos_ref[...]
    sin = sin_ref[...]

    def norm_rope(t, gain, scale):
        y = _rms(t, gain)
        rot = pltpu.roll(y, SWA_HEAD_DIM // 2, axis=1)
        return ((y * cos + rot * sin) * scale).astype(BF16)

    hd = SWA_HEAD_DIM

    def rope_pair(w_ref, o_ref, gain_ref, scale, c0):
        t = _dot(h, w_ref[:, c0:c0 + MXU_COLS])
        for c1 in range(0, MXU_COLS, hd):
            o_ref[:, c0 + c1:c0 + c1 + hd] = norm_rope(t[:, c1:c1 + hd], gain_ref[...], scale)

    def v_head(hh):
        v_ref[hh] = _dot(h, wv_ref[:, hh * GLA_DV:(hh + 1) * GLA_DV]).astype(BF16)

    def r_head(hh):
        r_ref[hh] = jax.nn.silu(_dot(h, wr_ref[:, hh * GLA_DV:(hh + 1) * GLA_DV])).astype(BF16)

    rope_pair(wsk_ref, sk_ref, kg_ref, 1.0, 0)
    for pair in range(SWA_Q_W // MXU_COLS):
        rope_pair(wsq_ref, sq_ref, qg_ref, hd ** -0.5 * LOG2_E, pair * MXU_COLS)

    lr = _dot(h, wlr_ref[...]).astype(BF16)
    ga_ref[...] = jax.nn.sigmoid(_dot(h, wga_ref[...])).astype(BF16)
    z = _dot(lr, up_ref[...]) + gbias_ref[...]
    log2_g = jnp.maximum(
        (jnp.minimum(z, 0.0) * LOG2_E - jnp.log2(1.0 + jnp.exp2(jnp.abs(z) * -LOG2_E)))
        * (1.0 / GLA_GATE_NORMALIZER), GLA_LOG_GATE_MIN * LOG2_E)
    hi = log2_g.astype(BF16)
    lo = (log2_g - hi.astype(F32)).astype(BF16)

    v_head(0)
    v_head(1)
    gb_ref[...] = jax.nn.sigmoid(_dot(h, wgb_ref[...])).astype(BF16)
    q = _dot(h, wq_ref[...]) * (GLA_DK ** -0.5)
    k = _dot(h, wk_ref[...])
    v_head(2)
    v_head(3)

    ri = lax.broadcasted_iota(jnp.int32, (c, 2 * c), 0)
    ci = lax.broadcasted_iota(jnp.int32, (c, 2 * c), 1) % c
    tri = (jnp.where(ri >= ci, 1.0, 0.0).astype(BF16), jnp.where(ci >= ri, 1.0, 0.0).astype(BF16))
    total_row = (c - 1, 0)
    for t in range(tm // c):
        rows = slice(t * c, (t + 1) * c)
        totals = []
        for d in range(2):
            cols = slice(d * GLA_QK_W, (d + 1) * GLA_QK_W)
            b = _dot(tri[d], jnp.concatenate([hi[rows, cols], lo[rows, cols]], axis=0))
            decay = jnp.exp2(b)
            total = decay[total_row[d]:total_row[d] + 1, :]
            totals.append(total)
            q_dec = (q[rows, :] * decay).astype(BF16)
            k_inv = k[rows, :] * jnp.exp2(-b)
            k_tail = (k_inv * total).astype(BF16)
            k_inv = k_inv.astype(BF16)
            for hh in range(GLA_HEADS):
                src = slice(hh * GLA_DK, (hh + 1) * GLA_DK)
                dst = slice(d * GLA_DK, (d + 1) * GLA_DK)
                qd_ref[hh, rows, dst] = q_dec[:, src]
                ki_ref[hh, dst, rows] = k_inv[:, src].T
                kt_ref[hh, rows, dst] = k_tail[:, src]
        tot_ref[t] = jnp.concatenate(
            [totals[d][:, hh * GLA_DK:(hh + 1) * GLA_DK]
             for hh in range(GLA_HEADS) for d in range(2)], axis=1)
        if t < GLA_HEADS:
            r_head(t)

    for hh in range(tm // c, GLA_HEADS):
        r_head(hh)
    sv_ref[...] = _dot(h, wsv_ref[...]).astype(BF16)


def _in_projection(x2, norm_g, cos, sin, q_g, k_g, weights, seq):
    m = x2.shape[0]
    tm = IN_TILE_M
    pos_blocks = seq // tm
    row = lambda w: pl.BlockSpec((tm, w), lambda i: (i, 0))
    pos = pl.BlockSpec((tm, SWA_HEAD_DIM), lambda i: (i % pos_blocks, 0))
    head_major = (pl.BlockSpec((GLA_HEADS, tm, GLA_DV), lambda i: (0, i, 0)),
                  jax.ShapeDtypeStruct((GLA_HEADS, m, GLA_DV), BF16))
    tot = (pl.BlockSpec((tm // GLA_CHUNK, 1, 2 * GLA_QK_W), lambda i: (i, 0, 0)),
           jax.ShapeDtypeStruct((m // GLA_CHUNK, 1, 2 * GLA_QK_W), F32))
    flat = lambda w: (row(w), jax.ShapeDtypeStruct((m, w), BF16))
    ki_t = (pl.BlockSpec((GLA_HEADS, 2 * GLA_DK, tm), lambda i: (0, 0, i)),
            jax.ShapeDtypeStruct((GLA_HEADS, 2 * GLA_DK, m), BF16))
    outs = (head_major, ki_t, head_major, tot, head_major, head_major,
            flat(SWA_Q_W), flat(SWA_KV_W), flat(SWA_KV_W), flat(D_MODEL), flat(D_MODEL))
    return pl.pallas_call(
        _inproj_kernel,
        grid=(m // tm,),
        in_specs=[row(D_MODEL), _const_spec((1, D_MODEL)), pos, pos,
                  _const_spec((1, SWA_HEAD_DIM)), _const_spec((1, SWA_HEAD_DIM)),
                  pl.BlockSpec(memory_space=pl.ANY)]
                 + [_const_spec(w.shape) for w in weights[1:]],
        out_specs=[spec for spec, _ in outs],
        out_shape=[shape for _, shape in outs],
        scratch_shapes=[pltpu.VMEM((D_MODEL, W_IN_CODES + LANES), BF16),
                        pltpu.VMEM((D_MODEL, W_IN_OFFSETS[-1] - W_IN_TAIL), BF16),
                        pltpu.VMEM((STAGE_SLOTS, LANES, D_MODEL), F32),
                        pltpu.SemaphoreType.DMA((STAGE_SLOTS,))],
        compiler_params=pltpu.CompilerParams(
            dimension_semantics=("arbitrary",), vmem_limit_bytes=VMEM_LIMIT),
        name="in_projection",
    )(x2, norm_g, cos, sin, q_g, k_g, *weights)


def _gla_kernel(qd_ref, ki_ref, kt_ref, tot_ref, v_ref, r_ref, og_ref,
                o_ref, sf_ref, kvb_ref, state_ref, oraw_ref):
    c = GLA_CHUNK
    dk = GLA_DK
    n_chunks = v_ref.shape[0] // c
    lower_incl = (lax.broadcasted_iota(jnp.int32, (c, c), 0)
                  >= lax.broadcasted_iota(jnp.int32, (c, c), 1))
    rows_of = lambda n: pl.ds(n * c if isinstance(n, int) else pl.multiple_of(n * c, c), c)

    state_ref[...] = jnp.zeros_like(state_ref)

    def sweep_right(i, carry):
        chunks = [i * GLA_STATE_UNROLL + t for t in range(GLA_STATE_UNROLL)]
        kv_t = [_dot_tn(v_ref[rows_of(n), :], kt_ref[rows_of(n), :]) for n in chunks]
        s = state_ref[...]
        for n, kv in zip(chunks, kv_t):
            sf_ref[n] = s.astype(BF16).T
            kvb_ref[n] = kv[:, dk:]
            s = s * tot_ref[n][:, :dk] + kv[:, :dk]
        state_ref[...] = s
        return carry

    lax.fori_loop(0, n_chunks // GLA_STATE_UNROLL, sweep_right, 0)

    state_ref[...] = jnp.zeros_like(state_ref)
    u = GLA_OUT_UNROLL
    trips = n_chunks // u
    chunks_of = lambda i: [n_chunks - 1 - (i * u + t) for t in range(u)]

    def matmuls(i):
        chunks = chunks_of(i)
        rows = [rows_of(n) for n in chunks]
        qd = [qd_ref[r, :] for r in rows]
        ki_t = [ki_ref[:, r] for r in rows]
        a = [jnp.where(lower_incl, _dot(q[:, :dk], k[:dk, :]),
                       _dot(q[:, dk:], k[dk:, :])).astype(BF16) for q, k in zip(qd, ki_t)]
        s = state_ref[...]
        for j, n in enumerate(chunks):
            both = jnp.concatenate([sf_ref[n], s.astype(BF16).T], axis=0)
            oraw_ref[i % 2, j] = _dot(a[j], v_ref[rows[j], :]) + _dot(qd[j], both)
            s = s * tot_ref[n][:, dk:] + kvb_ref[n]
        state_ref[...] = s

    def tail(i):
        rows = [rows_of(n) for n in chunks_of(i)]
        finals = [(_rms(oraw_ref[i % 2, j], og_ref[...])
                   * r_ref[r, :].astype(F32)).astype(BF16)
                  for j, r in enumerate(rows)]
        for r, o in zip(rows, finals):
            o_ref[r, :] = o

    def sweep_left(i, carry):
        tail(i - 1)
        matmuls(i)
        return carry

    matmuls(0)
    lax.fori_loop(1, trips, sweep_left, 0)
    tail(trips - 1)


def _gla_branch(qd, ki, kt, tot, v, r, out_g, batch, seq):
    n_chunks = seq // GLA_CHUNK
    assert 2 * GLA_DK == GLA_DV
    vv = pl.BlockSpec((None, seq, GLA_DV), lambda b, h: (h, b, 0))
    return pl.pallas_call(
        _gla_kernel,
        grid=(batch, GLA_HEADS),
        in_specs=[vv, pl.BlockSpec((None, 2 * GLA_DK, seq), lambda b, h: (h, 0, b)), vv,
                  pl.BlockSpec((n_chunks, 1, 2 * GLA_DK), lambda b, h: (b, 0, h)),
                  vv, vv, _const_spec((1, GLA_DV))],
        out_specs=vv,
        out_shape=jax.ShapeDtypeStruct((GLA_HEADS, batch * seq, GLA_DV), BF16),
        scratch_shapes=[pltpu.VMEM((n_chunks, GLA_DK, GLA_DV), BF16),
                        pltpu.VMEM((n_chunks, GLA_DV, GLA_DK), F32),
                        pltpu.VMEM((GLA_DV, GLA_DK), F32),
                        pltpu.VMEM((2, GLA_OUT_UNROLL, GLA_CHUNK, GLA_DV), F32)],
        compiler_params=pltpu.CompilerParams(
            dimension_semantics=("parallel", "parallel"), vmem_limit_bytes=VMEM_LIMIT),
        name="gla_branch",
    )(qd, ki, kt, tot, v, r, out_g)


def _swa_kernel(sink_ref, q_ref, k_ref, v_ref, o_ref):
    blk = SWA_BLOCK
    hd = SWA_HEAD_DIM
    tile_blocks = q_ref.shape[0] // blk
    seq_blocks = k_ref.shape[0] // blk
    tile = pl.program_id(1)
    rows_g = SWA_GROUP * blk
    qi = lax.broadcasted_iota(jnp.int32, (rows_g, blk), 0) % blk
    kj = lax.broadcasted_iota(jnp.int32, (rows_g, blk), 1)
    ones = jnp.ones((3 * blk, hd), BF16)
    head_cols = lambda hh: slice(hh * hd, (hh + 1) * hd)

    def step(i, carry):
        jobs = [(i * SWA_UNROLL + t, hk) for t in range(SWA_UNROLL) for hk in range(SWA_KV_HEADS)]
        q4, kv_rows, masks = [], [], []
        for t, hk in jobs:
            n = tile * tile_blocks + t
            q_rows = pl.ds(pl.multiple_of(t * blk, blk), blk)
            q4.append(jnp.concatenate(
                [q_ref[q_rows, head_cols(hk * SWA_GROUP + g)] for g in range(SWA_GROUP)], axis=0))
            kv_rows.append([pl.ds(pl.multiple_of(nn * blk, blk), blk) for nn in
                            (jnp.maximum(n - 1, 0), n, jnp.minimum(n + 1, seq_blocks - 1))])
            masks.append([kj >= qi + jnp.where(n > 0, 0, blk), None,
                          kj <= qi - jnp.where(n < seq_blocks - 1, 0, blk)])
        s = [[_dot_nt(q, k_ref[r, head_cols(hk)]) for r in rows]
             for q, rows, (_, hk) in zip(q4, kv_rows, jobs)]
        s = [[sj if mask is None else jnp.where(mask, sj, MASK_VALUE)
              for sj, mask in zip(sb, mb)] for sb, mb in zip(s, masks)]
        row_max = [jnp.broadcast_to(
            jnp.maximum(jnp.maximum(sb[0], sb[1]), sb[2]).max(axis=-1, keepdims=True),
            (rows_g, blk)) for sb in s]
        m, sink_w = [], []
        for rm, (_, hk) in zip(row_max, jobs):
            sinks = [sink_ref[hk * SWA_GROUP + g] * LOG2_E for g in range(SWA_GROUP)]
            mg = [jnp.maximum(rm[g * blk:(g + 1) * blk], sk) for g, sk in enumerate(sinks)]
            m.append(jnp.concatenate(mg, axis=0))
            sink_w.append(jnp.concatenate([jnp.exp2(sk - x) for sk, x in zip(sinks, mg)], axis=0))
        p = [jnp.concatenate([jnp.exp2(sj - mb).astype(BF16) for sj in sb], axis=1)
             for sb, mb in zip(s, m)]
        pv = [_dot(pj, jnp.concatenate(
                  [jnp.concatenate([v_ref[r, head_cols(hk)] for r in rows], axis=0), ones], axis=1))
              for pj, rows, (_, hk) in zip(p, kv_rows, jobs)]
        for acc, sw, (t, hk) in zip(pv, sink_w, jobs):
            o = acc[:, :hd] / (acc[:, hd:] + sw)
            q_rows = pl.ds(pl.multiple_of(t * blk, blk), blk)
            for g in range(SWA_GROUP):
                o_ref[q_rows, head_cols(hk * SWA_GROUP + g)] = (
                    o[g * blk:(g + 1) * blk, :].astype(BF16))
        return carry

    lax.fori_loop(0, tile_blocks // SWA_UNROLL, step, 0)


def _swa_branch(sq, sk, sv, sinks, batch, seq):
    tq = SWA_TILE_Q
    tiles = seq // tq
    return pl.pallas_call(
        _swa_kernel,
        grid=(batch, tiles),
        in_specs=[pl.BlockSpec(memory_space=pltpu.SMEM),
                  pl.BlockSpec((tq, SWA_Q_W), lambda b, t: (b * tiles + t, 0)),
                  pl.BlockSpec((seq, SWA_KV_W), lambda b, t: (b, 0)),
                  pl.BlockSpec((seq, SWA_KV_W), lambda b, t: (b, 0))],
        out_specs=pl.BlockSpec((tq, SWA_Q_W), lambda b, t: (b * tiles + t, 0)),
        out_shape=jax.ShapeDtypeStruct((batch * seq, SWA_Q_W), BF16),
        compiler_params=pltpu.CompilerParams(
            dimension_semantics=("parallel", "parallel"), vmem_limit_bytes=VMEM_LIMIT),
        name="swa_branch",
    )(sinks, sq, sk, sv)


def _out_ffn_kernel(x_ref, og_ref, os_ref, ga_ref, gb_ref, g2_ref,
                    wog_hbm, wos_hbm, wout_hbm, wfi_hbm, wfo_hbm, o_ref,
                    wog_ref, wos_ref, wout_ref, wfi_ref, wfo_ref, stage_ref, sem_ref):
    @pl.when(pl.program_id(0) == 0)
    def _():
        tr, tc = stage_ref.shape[1:]
        tiles = []
        for src, dst in ((wog_hbm, wog_ref), (wos_hbm, wos_ref), (wout_hbm, wout_ref),
                         (wfi_hbm, wfi_ref), (wfo_hbm, wfo_ref)):
            for r0 in range(0, src.shape[0], tr):
                for c0 in range(0, src.shape[1], tc):
                    def cast(tile_ref, dst=dst, r0=r0, c0=c0):
                        dst[r0:r0 + tr, c0:c0 + tc] = tile_ref[...].astype(BF16)
                    tiles.append((src.at[r0:r0 + tr, c0:c0 + tc], cast))
        _stream_tiles(tiles, stage_ref, sem_ref)

    o_gla = jnp.concatenate([og_ref[hh] for hh in range(GLA_HEADS)], axis=1)
    y_gla = _dot(o_gla, wog_ref[...])
    y_swa = _dot(os_ref[...], wos_ref[...])
    merged = ga_ref[...].astype(F32) * y_gla + gb_ref[...].astype(F32) * y_swa
    x1 = x_ref[...] + _dot(merged.astype(BF16), wout_ref[...])
    h2 = _rms(x1, g2_ref[...]).astype(BF16)
    acc = x1
    for c0 in range(0, D_FF, FFN_CHUNK):
        cols = slice(c0, c0 + FFN_CHUNK)
        gate = _dot(h2, wfi_ref[:, cols])
        up = _dot(h2, wfi_ref[:, D_FF + c0:D_FF + c0 + FFN_CHUNK])
        act = (jax.nn.silu(gate) * up).astype(BF16)
        acc = acc + _dot(act, wfo_ref[cols, :])
    o_ref[...] = acc


def _out_ffn(x2, o_gla, o_swa, ga, gb, norm_g, *weights):
    m = x2.shape[0]
    tm = OUT_TILE_M
    row = pl.BlockSpec((tm, D_MODEL), lambda i: (i, 0))
    return pl.pallas_call(
        _out_ffn_kernel,
        grid=(m // tm,),
        in_specs=[row, pl.BlockSpec((GLA_HEADS, tm, GLA_DV), lambda i: (0, i, 0))] + [row] * 3
                 + [_const_spec(norm_g.shape)]
                 + [pl.BlockSpec(memory_space=pl.ANY)] * len(weights),
        out_specs=row,
        out_shape=jax.ShapeDtypeStruct((m, D_MODEL), F32),
        scratch_shapes=[pltpu.VMEM(w.shape, BF16) for w in weights]
                       + [pltpu.VMEM((STAGE_SLOTS,) + OUT_STAGE_TILE, F32),
                          pltpu.SemaphoreType.DMA((STAGE_SLOTS,))],
        compiler_params=pltpu.CompilerParams(
            dimension_semantics=("arbitrary",), vmem_limit_bytes=VMEM_LIMIT),
        name="out_ffn",
    )(x2, o_gla, o_swa, ga, gb, norm_g, *weights)


def _rope_tables(seq):
    half = SWA_HEAD_DIM // 2
    inv_freq = ROPE_THETA ** (-np.arange(half, dtype=np.float64) / half)
    ang = np.arange(seq, dtype=np.float64)[:, None] * inv_freq[None, :]
    cos, sin = np.cos(ang), np.sin(ang)
    return (jnp.asarray(np.concatenate([cos, cos], axis=-1), F32),
            jnp.asarray(np.concatenate([-sin, sin], axis=-1), F32))


def _gate_up(up_f, up_b):
    zeros = jnp.zeros_like(up_f)
    top = jnp.concatenate([up_f, zeros], axis=1)
    bot = jnp.concatenate([zeros, up_b], axis=1)
    pad = jnp.zeros((LANES - 2 * GLA_GATE_RANK, 2 * GLA_QK_W), up_f.dtype)
    return jnp.concatenate([top, bot, pad], axis=0).astype(BF16)


def _layer(x2, batch, seq, norm_mix_g, w_in, up_f, bias_f, up_b, bias_b, gla_out_g, w_o_gla,
           q_g, k_g, sinks, w_o_swa, w_out, norm_ffn_g, w_ffn_in, w_ffn_out):
    cos, sin = _rope_tables(seq)
    row = lambda t: t.reshape(1, -1)
    qd, ki, kt, tot, v, r, sq, sk, sv, ga, gb = _in_projection(
        x2, row(norm_mix_g), cos, sin, row(q_g), row(k_g),
        (w_in.T, _gate_up(up_f, up_b), row(jnp.concatenate([bias_f, bias_b]))), seq)
    o_gla = _gla_branch(qd, ki, kt, tot, v, r, row(gla_out_g), batch, seq)
    o_swa = _swa_branch(sq, sk, sv, sinks, batch, seq)
    return _out_ffn(x2, o_gla, o_swa, ga, gb, row(norm_ffn_g),
                    w_o_gla, w_o_swa, w_out, w_ffn_in, w_ffn_out)


def kernel(x, norm_mix_g, w_in, gla_gate_up_fwd, gla_gate_bias_fwd, gla_gate_up_bwd,
           gla_gate_bias_bwd, gla_out_norm_g, w_o_gla, swa_q_norm_g, swa_k_norm_g,
           swa_sinks, w_o_swa, w_out, norm_ffn_g, w_ffn_in, w_ffn_out):
    batch, seq, d = x.shape
    x2 = x.reshape(batch * seq, d)
    for l in range(w_in.shape[0]):
        x2 = _layer(x2, batch, seq, norm_mix_g[l], w_in[l], gla_gate_up_fwd[l],
                    gla_gate_bias_fwd[l], gla_gate_up_bwd[l], gla_gate_bias_bwd[l],
                    gla_out_norm_g[l], w_o_gla[l], swa_q_norm_g[l], swa_k_norm_g[l],
                    swa_sinks[l], w_o_swa[l], w_out[l], norm_ffn_g[l], w_ffn_in[l],
                    w_ffn_out[l])
    return x2.reshape(batch, seq, d)
```

```python
import math

import jax
import jax.numpy as jnp
import numpy as np
from jax import lax
from jax.experimental import pallas as pl
from jax.experimental.pallas import tpu as pltpu

F32 = jnp.float32
BF16 = jnp.bfloat16

D_MODEL = 1024
NORM_EPS = 1e-6
GLA_HEADS = 4
GLA_DK = 128
GLA_DV = 256
GLA_QK_W = GLA_HEADS * GLA_DK
GLA_V_W = GLA_HEADS * GLA_DV
GLA_GATE_RANK = 16
GLA_GATE_NORMALIZER = 16.0
GLA_LOG_GATE_MIN = -0.5
GLA_CHUNK = 128
GLA_STATE_UNROLL = 32
GLA_OUT_UNROLL = 32
SWA_HEADS = 8
SWA_KV_HEADS = 2
SWA_GROUP = SWA_HEADS // SWA_KV_HEADS
SWA_HEAD_DIM = 128
SWA_Q_W = SWA_HEADS * SWA_HEAD_DIM
SWA_KV_W = SWA_KV_HEADS * SWA_HEAD_DIM
SWA_BLOCK = 128
ROPE_THETA = 10000.0
D_FF = 2816
IN_SPLITS = (GLA_QK_W, GLA_QK_W, GLA_V_W, GLA_V_W, GLA_GATE_RANK, GLA_GATE_RANK,
             SWA_Q_W, SWA_KV_W, SWA_KV_W, D_MODEL, D_MODEL)

LANES = 128
W_IN_OFFSETS = tuple(int(o) for o in np.cumsum((0,) + IN_SPLITS))
W_IN_CODES = W_IN_OFFSETS[4]
W_IN_TAIL = W_IN_OFFSETS[6]
MXU_COLS = 256
MASK_VALUE = -1e30
LOG2_E = math.log2(math.e)
VMEM_LIMIT = 56 * 1024 * 1024

IN_TILE_M = 512
SWA_TILE_Q = 2048
SWA_UNROLL = 4
OUT_TILE_M = 512
FFN_CHUNK = D_FF // 2
STAGE_SLOTS = 8
OUT_STAGE_TILE = (256, 512)


def _const_spec(shape):
    zeros = (0,) * len(shape)
    return pl.BlockSpec(shape, lambda *_: zeros, pipeline_mode=pl.Buffered(1))


def _dot(a, b):
    return jnp.dot(a, b, preferred_element_type=F32)


def _dot_nt(a, b):
    return lax.dot_general(a, b, (((1,), (1,)), ((), ())), preferred_element_type=F32)


def _dot_tn(a, b):
    return lax.dot_general(a, b, (((0,), (0,)), ((), ())), preferred_element_type=F32)


def _rms(x, gain):
    ms = jnp.mean(x * x, axis=-1, keepdims=True)
    return x * lax.rsqrt(ms + NORM_EPS) * gain


def _stream_tiles(tiles, stage_ref, sem_ref):
    copies = [pltpu.make_async_copy(src, stage_ref.at[t % STAGE_SLOTS], sem_ref.at[t % STAGE_SLOTS])
              for t, (src, _) in enumerate(tiles)]
    depth = STAGE_SLOTS - 1
    for cp in copies[:depth]:
        cp.start()
    for t, (_, consume) in enumerate(tiles):
        if t + depth < len(tiles):
            copies[t + depth].start()
        copies[t].wait()
        consume(stage_ref.at[t % STAGE_SLOTS])


def _inproj_kernel(x_ref, g_ref, cos_ref, sin_ref, qg_ref, kg_ref, w_ref, up_ref, gbias_ref,
                   qd_ref, ki_ref, kt_ref, tot_ref, v_ref, r_ref, sq_ref, sk_ref, sv_ref,
                   ga_ref, gb_ref, whead_ref, wtail_ref, stage_ref, sem_ref):
    tm = x_ref.shape[0]
    c = GLA_CHUNK

    @pl.when(pl.program_id(0) == 0)
    def _():
        tiles = []
        for dst_ref, row0 in ((whead_ref, 0), (wtail_ref, W_IN_TAIL)):
            for c0 in range(0, dst_ref.shape[1], LANES):
                def cast(tile_ref, dst_ref=dst_ref, c0=c0):
                    dst_ref[:, c0:c0 + LANES] = tile_ref[...].T.astype(BF16)
                tiles.append((w_ref.at[row0 + c0:row0 + c0 + LANES, :], cast))
        _stream_tiles(tiles, stage_ref, sem_ref)

    w_ref = whead_ref
    wq_ref, wk_ref, wv_ref, wr_ref = [
        w_ref.at[:, a:b] for a, b in zip(W_IN_OFFSETS[:4], W_IN_OFFSETS[1:5])]
    wsq_ref, wsk_ref, wsv_ref, wga_ref, wgb_ref = [
        wtail_ref.at[:, a - W_IN_TAIL:b - W_IN_TAIL]
        for a, b in zip(W_IN_OFFSETS[6:-1], W_IN_OFFSETS[7:])]
    wlr_ref = w_ref.at[:, W_IN_CODES:W_IN_CODES + LANES]
    h = _rms(x_ref[...], g_ref[...]).astype(BF16)
    cos = cos_ref[...]
    sin = sin_ref[...]

    def norm_rope(t, gain, scale):
        y = _rms(t, gain)
        rot = pltpu.roll(y, SWA_HEAD_DIM // 2, axis=1)
        return ((y * cos + rot * sin) * scale).astype(BF16)

    hd = SWA_HEAD_DIM

    def rope_pair(w_ref, o_ref, gain_ref, scale, c0):
        t = _dot(h, w_ref[:, c0:c0 + MXU_COLS])
        for c1 in range(0, MXU_COLS, hd):
            o_ref[:, c0 + c1:c0 + c1 + hd] = norm_rope(t[:, c1:c1 + hd], gain_ref[...], scale)

    def v_head(hh):
        v_ref[hh] = _dot(h, wv_ref[:, hh * GLA_DV:(hh + 1) * GLA_DV]).astype(BF16)

    def r_head(hh):
        r_ref[hh] = jax.nn.silu(_dot(h, wr_ref[:, hh * GLA_DV:(hh + 1) * GLA_DV])).astype(BF16)

    rope_pair(wsk_ref, sk_ref, kg_ref, 1.0, 0)
    for pair in range(SWA_Q_W // MXU_COLS):
        rope_pair(wsq_ref, sq_ref, qg_ref, hd ** -0.5 * LOG2_E, pair * MXU_COLS)

    lr = _dot(h, wlr_ref[...]).astype(BF16)
    ga_ref[...] = jax.nn.sigmoid(_dot(h, wga_ref[...])).astype(BF16)
    z = _dot(lr, up_ref[...]) + gbias_ref[...]
    log2_g = jnp.maximum(
        (jnp.minimum(z, 0.0) * LOG2_E - jnp.log2(1.0 + jnp.exp2(jnp.abs(z) * -LOG2_E)))
        * (1.0 / GLA_GATE_NORMALIZER), GLA_LOG_GATE_MIN * LOG2_E)
    hi = log2_g.astype(BF16)
    lo = (log2_g - hi.astype(F32)).astype(BF16)

    v_head(0)
    v_head(1)
    gb_ref[...] = jax.nn.sigmoid(_dot(h, wgb_ref[...])).astype(BF16)
    q = _dot(h, wq_ref[...]) * (GLA_DK ** -0.5)
    k = _dot(h, wk_ref[...])
    v_head(2)
    v_head(3)

    ri = lax.broadcasted_iota(jnp.int32, (c, 2 * c), 0)
    ci = lax.broadcasted_iota(jnp.int32, (c, 2 * c), 1) % c
    tri = (jnp.where(ri >= ci, 1.0, 0.0).astype(BF16), jnp.where(ci >= ri, 1.0, 0.0).astype(BF16))
    total_row = (c - 1, 0)
    for t in range(tm // c):
        rows = slice(t * c, (t + 1) * c)
        totals = []
        for d in range(2):
            cols = slice(d * GLA_QK_W, (d + 1) * GLA_QK_W)
            b = _dot(tri[d], jnp.concatenate([hi[rows, cols], lo[rows, cols]], axis=0))
            decay = jnp.exp2(b)
            total = decay[total_row[d]:total_row[d] + 1, :]
            totals.append(total)
            q_dec = (q[rows, :] * decay).astype(BF16)
            k_inv = k[rows, :] * jnp.exp2(-b)
            k_tail = (k_inv * total).astype(BF16)
            k_inv = k_inv.astype(BF16)
            for hh in range(GLA_HEADS):
                src = slice(hh * GLA_DK, (hh + 1) * GLA_DK)
                dst = slice(d * GLA_DK, (d + 1) * GLA_DK)
                qd_ref[hh, rows, dst] = q_dec[:, src]
                ki_ref[hh, dst, rows] = k_inv[:, src].T
                kt_ref[hh, rows, dst] = k_tail[:, src]
        tot_ref[t] = jnp.concatenate(
            [totals[d][:, hh * GLA_DK:(hh + 1) * GLA_DK]
             for hh in range(GLA_HEADS) for d in range(2)], axis=1)
        if t < GLA_HEADS:
            r_head(t)

    for hh in range(tm // c, GLA_HEADS):
        r_head(hh)
    sv_ref[...] = _dot(h, wsv_ref[...]).astype(BF16)


def _in_projection(x2, norm_g, cos, sin, q_g, k_g, weights, seq):
    m = x2.shape[0]
    tm = IN_TILE_M
    pos_blocks = seq // tm
    row = lambda w: pl.BlockSpec((tm, w), lambda i: (i, 0))
    pos = pl.BlockSpec((tm, SWA_HEAD_DIM), lambda i: (i % pos_blocks, 0))
    head_major = (pl.BlockSpec((GLA_HEADS, tm, GLA_DV), lambda i: (0, i, 0)),
                  jax.ShapeDtypeStruct((GLA_HEADS, m, GLA_DV), BF16))
    tot = (pl.BlockSpec((tm // GLA_CHUNK, 1, 2 * GLA_QK_W), lambda i: (i, 0, 0)),
           jax.ShapeDtypeStruct((m // GLA_CHUNK, 1, 2 * GLA_QK_W), F32))
    flat = lambda w: (row(w), jax.ShapeDtypeStruct((m, w), BF16))
    ki_t = (pl.BlockSpec((GLA_HEADS, 2 * GLA_DK, tm), lambda i: (0, 0, i)),
            jax.ShapeDtypeStruct((GLA_HEADS, 2 * GLA_DK, m), BF16))
    outs = (head_major, ki_t, head_major, tot, head_major, head_major,
            flat(SWA_Q_W), flat(SWA_KV_W), flat(SWA_KV_W), flat(D_MODEL), flat(D_MODEL))
    return pl.pallas_call(
        _inproj_kernel,
        grid=(m // tm,),
        in_specs=[row(D_MODEL), _const_spec((1, D_MODEL)), pos, pos,
                  _const_spec((1, SWA_HEAD_DIM)), _const_spec((1, SWA_HEAD_DIM)),
                  pl.BlockSpec(memory_space=pl.ANY)]
                 + [_const_spec(w.shape) for w in weights[1:]],
        out_specs=[spec for spec, _ in outs],
        out_shape=[shape for _, shape in outs],
        scratch_shapes=[pltpu.VMEM((D_MODEL, W_IN_CODES + LANES), BF16),
                        pltpu.VMEM((D_MODEL, W_IN_OFFSETS[-1] - W_IN_TAIL), BF16),
                        pltpu.VMEM((STAGE_SLOTS, LANES, D_MODEL), F32),
                        pltpu.SemaphoreType.DMA((STAGE_SLOTS,))],
        compiler_params=pltpu.CompilerParams(
            dimension_semantics=("arbitrary",), vmem_limit_bytes=VMEM_LIMIT),
        name="in_projection",
    )(x2, norm_g, cos, sin, q_g, k_g, *weights)


def _gla_kernel(qd_ref, ki_ref, kt_ref, tot_ref, v_ref, r_ref, og_ref,
                o_ref, sf_ref, kvb_ref, state_ref, oraw_ref):
    c = GLA_CHUNK
    dk = GLA_DK
    n_chunks = v_ref.shape[0] // c
    lower_incl = (lax.broadcasted_iota(jnp.int32, (c, c), 0)
                  >= lax.broadcasted_iota(jnp.int32, (c, c), 1))
    rows_of = lambda n: pl.ds(n * c if isinstance(n, int) else pl.multiple_of(n * c, c), c)

    state_ref[...] = jnp.zeros_like(state_ref)

    def sweep_right(i, carry):
        chunks = [i * GLA_STATE_UNROLL + t for t in range(GLA_STATE_UNROLL)]
        kv_t = [_dot_tn(v_ref[rows_of(n), :], kt_ref[rows_of(n), :]) for n in chunks]
        s = state_ref[...]
        for n, kv in zip(chunks, kv_t):
            sf_ref[n] = s.astype(BF16).T
            kvb_ref[n] = kv[:, dk:]
            s = s * tot_ref[n][:, :dk] + kv[:, :dk]
        state_ref[...] = s
        return carry

    lax.fori_loop(0, n_chunks // GLA_STATE_UNROLL, sweep_right, 0)

    state_ref[...] = jnp.zeros_like(state_ref)
    u = GLA_OUT_UNROLL
    trips = n_chunks // u
    chunks_of = lambda i: [n_chunks - 1 - (i * u + t) for t in range(u)]

    def matmuls(i):
        chunks = chunks_of(i)
        rows = [rows_of(n) for n in chunks]
        qd = [qd_ref[r, :] for r in rows]
        ki_t = [ki_ref[:, r] for r in rows]
        a = [jnp.where(lower_incl, _dot(q[:, :dk], k[:dk, :]),
                       _dot(q[:, dk:], k[dk:, :])).astype(BF16) for q, k in zip(qd, ki_t)]
        s = state_ref[...]
        for j, n in enumerate(chunks):
            both = jnp.concatenate([sf_ref[n], s.astype(BF16).T], axis=0)
            oraw_ref[i % 2, j] = _dot(a[j], v_ref[rows[j], :]) + _dot(qd[j], both)
            s = s * tot_ref[n][:, dk:] + kvb_ref[n]
        state_ref[...] = s

    def tail(i):
        rows = [rows_of(n) for n in chunks_of(i)]
        finals = [(_rms(oraw_ref[i % 2, j], og_ref[...])
                   * r_ref[r, :].astype(F32)).astype(BF16)
                  for j, r in enumerate(rows)]
        for r, o in zip(rows, finals):
            o_ref[r, :] = o

    def sweep_left(i, carry):
        tail(i - 1)
        matmuls(i)
        return carry

    matmuls(0)
    lax.fori_loop(1, trips, sweep_left, 0)
    tail(trips - 1)


def _gla_branch(qd, ki, kt, tot, v, r, out_g, batch, seq):
    n_chunks = seq // GLA_CHUNK
    assert 2 * GLA_DK == GLA_DV
    vv = pl.BlockSpec((None, seq, GLA_DV), lambda b, h: (h, b, 0))
    return pl.pallas_call(
        _gla_kernel,
        grid=(batch, GLA_HEADS),
        in_specs=[vv, pl.BlockSpec((None, 2 * GLA_DK, seq), lambda b, h: (h, 0, b)), vv,
                  pl.BlockSpec((n_chunks, 1, 2 * GLA_DK), lambda b, h: (b, 0, h)),
                  vv, vv, _const_spec((1, GLA_DV))],
        out_specs=vv,
        out_shape=jax.ShapeDtypeStruct((GLA_HEADS, batch * seq, GLA_DV), BF16),
        scratch_shapes=[pltpu.VMEM((n_chunks, GLA_DK, GLA_DV), BF16),
                        pltpu.VMEM((n_chunks, GLA_DV, GLA_DK), F32),
                        pltpu.VMEM((GLA_DV, GLA_DK), F32),
                        pltpu.VMEM((2, GLA_OUT_UNROLL, GLA_CHUNK, GLA_DV), F32)],
        compiler_params=pltpu.CompilerParams(
            dimension_semantics=("parallel", "parallel"), vmem_limit_bytes=VMEM_LIMIT),
        name="gla_branch",
    )(qd, ki, kt, tot, v, r, out_g)


def _swa_kernel(sink_ref, q_ref, k_ref, v_ref, o_ref):
    blk = SWA_BLOCK
    hd = SWA_HEAD_DIM
    tile_blocks = q_ref.shape[0] // blk
    seq_blocks = k_ref.shape[0] // blk
    tile = pl.program_id(1)
    rows_g = SWA_GROUP * blk
    qi = lax.broadcasted_iota(jnp.int32, (rows_g, blk), 0) % blk
    kj = lax.broadcasted_iota(jnp.int32, (rows_g, blk), 1)
    ones = jnp.ones((3 * blk, hd), BF16)
    head_cols = lambda hh: slice(hh * hd, (hh + 1) * hd)

    def logits(t0):
        jobs = [(t0 + t, hk) for t in range(SWA_UNROLL) for hk in range(SWA_KV_HEADS)]
        out = []
        for t, hk in jobs:
            n = tile * tile_blocks + t
            q4 = jnp.concatenate([q_ref[t * blk:(t + 1) * blk, head_cols(hk * SWA_GROUP + g)]
                                  for g in range(SWA_GROUP)], axis=0)
            rows = [pl.ds(pl.multiple_of(nn * blk, blk), blk) for nn in
                    (jnp.maximum(n - 1, 0), n, jnp.minimum(n + 1, seq_blocks - 1))]
            masks = [kj >= qi + jnp.where(n > 0, 0, blk), None,
                     kj <= qi - jnp.where(n < seq_blocks - 1, 0, blk)]
            s = [_dot_nt(q4, k_ref[r, head_cols(hk)]) for r in rows]
            s = [sj if mask is None else jnp.where(mask, sj, MASK_VALUE)
                 for sj, mask in zip(s, masks)]
            out.append((t, hk, rows, s))
        return out

    def attend(group):
        for t, hk, rows, s in group:
            row_max = jnp.broadcast_to(
                jnp.maximum(jnp.maximum(s[0], s[1]), s[2]).max(axis=-1, keepdims=True),
                (rows_g, blk))
            sinks = [sink_ref[hk * SWA_GROUP + g] * LOG2_E for g in range(SWA_GROUP)]
            mg = [jnp.maximum(row_max[g * blk:(g + 1) * blk], sk) for g, sk in enumerate(sinks)]
            m = jnp.concatenate(mg, axis=0)
            sink_w = jnp.concatenate([jnp.exp2(sk - x) for sk, x in zip(sinks, mg)], axis=0)
            p = jnp.concatenate([jnp.exp2(sj - m).astype(BF16) for sj in s], axis=1)
            acc = _dot(p, jnp.concatenate(
                [jnp.concatenate([v_ref[r, head_cols(hk)] for r in rows], axis=0), ones], axis=1))
            o = acc[:, :hd] / (acc[:, hd:] + sink_w)
            for g in range(SWA_GROUP):
                o_ref[t * blk:(t + 1) * blk, head_cols(hk * SWA_GROUP + g)] = (
                    o[g * blk:(g + 1) * blk, :].astype(BF16))

    starts = list(range(0, tile_blocks, SWA_UNROLL))
    pending = logits(starts[0])
    for t0 in starts[1:]:
        upcoming = logits(t0)
        attend(pending)
        pending = upcoming
    attend(pending)


def _swa_branch(sq, sk, sv, sinks, batch, seq):
    tq = SWA_TILE_Q
    tiles = seq // tq
    return pl.pallas_call(
        _swa_kernel,
        grid=(batch, tiles),
        in_specs=[pl.BlockSpec(memory_space=pltpu.SMEM),
                  pl.BlockSpec((tq, SWA_Q_W), lambda b, t: (b * tiles + t, 0)),
                  pl.BlockSpec((seq, SWA_KV_W), lambda b, t: (b, 0)),
                  pl.BlockSpec((seq, SWA_KV_W), lambda b, t: (b, 0))],
        out_specs=pl.BlockSpec((tq, SWA_Q_W), lambda b, t: (b * tiles + t, 0)),
        out_shape=jax.ShapeDtypeStruct((batch * seq, SWA_Q_W), BF16),
        compiler_params=pltpu.CompilerParams(
            dimension_semantics=("parallel", "parallel"), vmem_limit_bytes=VMEM_LIMIT),
        name="swa_branch",
    )(sinks, sq, sk, sv)


def _out_ffn_kernel(x_ref, og_ref, os_ref, ga_ref, gb_ref, g2_ref,
                    wog_hbm, wos_hbm, wout_hbm, wfi_hbm, wfo_hbm, o_ref,
                    wog_ref, wos_ref, wout_ref, wfi_ref, wfo_ref, stage_ref, sem_ref):
    @pl.when(pl.program_id(0) == 0)
    def _():
        tr, tc = stage_ref.shape[1:]
        tiles = []
        for src, dst in ((wog_hbm, wog_ref), (wos_hbm, wos_ref), (wout_hbm, wout_ref),
                         (wfi_hbm, wfi_ref), (wfo_hbm, wfo_ref)):
            for r0 in range(0, src.shape[0], tr):
                for c0 in range(0, src.shape[1], tc):
                    def cast(tile_ref, dst=dst, r0=r0, c0=c0):
                        dst[r0:r0 + tr, c0:c0 + tc] = tile_ref[...].astype(BF16)
                    tiles.append((src.at[r0:r0 + tr, c0:c0 + tc], cast))
        _stream_tiles(tiles, stage_ref, sem_ref)

    o_gla = jnp.concatenate([og_ref[hh] for hh in range(GLA_HEADS)], axis=1)
    y_gla = _dot(o_gla, wog_ref[...])
    y_swa = _dot(os_ref[...], wos_ref[...])
    merged = ga_ref[...].astype(F32) * y_gla + gb_ref[...].astype(F32) * y_swa
    x1 = x_ref[...] + _dot(merged.astype(BF16), wout_ref[...])
    h2 = _rms(x1, g2_ref[...]).astype(BF16)
    acc = x1
    for c0 in range(0, D_FF, FFN_CHUNK):
        cols = slice(c0, c0 + FFN_CHUNK)
        gate = _dot(h2, wfi_ref[:, cols])
        up = _dot(h2, wfi_ref[:, D_FF + c0:D_FF + c0 + FFN_CHUNK])
        act = (jax.nn.silu(gate) * up).astype(BF16)
        acc = acc + _dot(act, wfo_ref[cols, :])
    o_ref[...] = acc


def _out_ffn(x2, o_gla, o_swa, ga, gb, norm_g, *weights):
    m = x2.shape[0]
    tm = OUT_TILE_M
    row = pl.BlockSpec((tm, D_MODEL), lambda i: (i, 0))
    return pl.pallas_call(
        _out_ffn_kernel,
        grid=(m // tm,),
        in_specs=[row, pl.BlockSpec((GLA_HEADS, tm, GLA_DV), lambda i: (0, i, 0))] + [row] * 3
                 + [_const_spec(norm_g.shape)]
                 + [pl.BlockSpec(memory_space=pl.ANY)] * len(weights),
        out_specs=row,
        out_shape=jax.ShapeDtypeStruct((m, D_MODEL), F32),
        scratch_shapes=[pltpu.VMEM(w.shape, BF16) for w in weights]
                       + [pltpu.VMEM((STAGE_SLOTS,) + OUT_STAGE_TILE, F32),
                          pltpu.SemaphoreType.DMA((STAGE_SLOTS,))],
        compiler_params=pltpu.CompilerParams(
            dimension_semantics=("arbitrary",), vmem_limit_bytes=VMEM_LIMIT),
        name="out_ffn",
    )(x2, o_gla, o_swa, ga, gb, norm_g, *weights)


def _rope_tables(seq):
    half = SWA_HEAD_DIM // 2
    inv_freq = ROPE_THETA ** (-np.arange(half, dtype=np.float64) / half)
    ang = np.arange(seq, dtype=np.float64)[:, None] * inv_freq[None, :]
    cos, sin = np.cos(ang), np.sin(ang)
    return (jnp.asarray(np.concatenate([cos, cos], axis=-1), F32),
            jnp.asarray(np.concatenate([-sin, sin], axis=-1), F32))


def _gate_up(up_f, up_b):
    zeros = jnp.zeros_like(up_f)
    top = jnp.concatenate([up_f, zeros], axis=1)
    bot = jnp.concatenate([zeros, up_b], axis=1)
    pad = jnp.zeros((LANES - 2 * GLA_GATE_RANK, 2 * GLA_QK_W), up_f.dtype)
    return jnp.concatenate([top, bot, pad], axis=0).astype(BF16)


def _layer(x2, batch, seq, norm_mix_g, w_in, up_f, bias_f, up_b, bias_b, gla_out_g, w_o_gla,
           q_g, k_g, sinks, w_o_swa, w_out, norm_ffn_g, w_ffn_in, w_ffn_out):
    cos, sin = _rope_tables(seq)
    row = lambda t: t.reshape(1, -1)
    qd, ki, kt, tot, v, r, sq, sk, sv, ga, gb = _in_projection(
        x2, row(norm_mix_g), cos, sin, row(q_g), row(k_g),
        (w_in.T, _gate_up(up_f, up_b), row(jnp.concatenate([bias_f, bias_b]))), seq)
    o_gla = _gla_branch(qd, ki, kt, tot, v, r, row(gla_out_g), batch, seq)
    o_swa = _swa_branch(sq, sk, sv, sinks, batch, seq)
    return _out_ffn(x2, o_gla, o_swa, ga, gb, row(norm_ffn_g),
                    w_o_gla, w_o_swa, w_out, w_ffn_in, w_ffn_out)


def kernel(x, norm_mix_g, w_in, gla_gate_up_fwd, gla_gate_bias_fwd, gla_gate_up_bwd,
           gla_gate_bias_bwd, gla_out_norm_g, w_o_gla, swa_q_norm_g, swa_k_norm_g,
           swa_sinks, w_o_swa, w_out, norm_ffn_g, w_ffn_in, w_ffn_out):
    batch, seq, d = x.shape
    x2 = x.reshape(batch * seq, d)
    for l in range(w_in.shape[0]):
        x2 = _layer(x2, batch, seq, norm_mix_g[l], w_in[l], gla_gate_up_fwd[l],
                    gla_gate_bias_fwd[l], gla_gate_up_bwd[l], gla_gate_bias_bwd[l],
                    gla_out_norm_g[l], w_o_gla[l], swa_q_norm_g[l], swa_k_norm_g[l],
                    swa_sinks[l], w_o_swa[l], w_out[l], norm_ffn_g[l], w_ffn_in[l],
                    w_ffn_out[l])
    return x2.reshape(batch, seq, d)
```

```python
import math

import jax
import jax.numpy as jnp
import numpy as np
from jax import lax
from jax.experimental import pallas as pl
from jax.experimental.pallas import tpu as pltpu

F32 = jnp.float32
BF16 = jnp.bfloat16

D_MODEL = 1024
NORM_EPS = 1e-6
GLA_HEADS = 4
GLA_DK = 128
GLA_DV = 256
GLA_QK_W = GLA_HEADS * GLA_DK
GLA_V_W = GLA_HEADS * GLA_DV
GLA_GATE_RANK = 16
GLA_GATE_NORMALIZER = 16.0
GLA_LOG_GATE_MIN = -0.5
GLA_CHUNK = 128
SWA_HEADS = 8
SWA_KV_HEADS = 2
SWA_GROUP = SWA_HEADS // SWA_KV_HEADS
SWA_HEAD_DIM = 128
SWA_Q_W = SWA_HEADS * SWA_HEAD_DIM
SWA_KV_W = SWA_KV_HEADS * SWA_HEAD_DIM
SWA_BLOCK = 128
ROPE_THETA = 10000.0
D_FF = 2816
IN_SPLITS = (GLA_QK_W, GLA_QK_W, GLA_V_W, GLA_V_W, GLA_GATE_RANK, GLA_GATE_RANK,
             SWA_Q_W, SWA_KV_W, SWA_KV_W, D_MODEL, D_MODEL)

LANES = 128
W_IN_OFFSETS = tuple(int(o) for o in np.cumsum((0,) + IN_SPLITS))
W_IN_CODES = W_IN_OFFSETS[4]
W_IN_TAIL = W_IN_OFFSETS[6]
MXU_COLS = 256
MASK_VALUE = -1e30
LOG2_E = math.log2(math.e)
VMEM_LIMIT = 56 * 1024 * 1024

IN_TILE_M = 512
SWA_TILE_Q = 2048
SWA_UNROLL = 4
OUT_TILE_M = 512
FFN_CHUNK = MXU_COLS
STAGE_SLOTS = 8
OUT_STAGE_TILE = (256, 512)


def _const_spec(shape):
    zeros = (0,) * len(shape)
    return pl.BlockSpec(shape, lambda *_: zeros, pipeline_mode=pl.Buffered(1))


def _dot(a, b):
    return jnp.dot(a, b, preferred_element_type=F32)


def _dot_nt(a, b):
    return lax.dot_general(a, b, (((1,), (1,)), ((), ())), preferred_element_type=F32)


def _dot_tn(a, b):
    return lax.dot_general(a, b, (((0,), (0,)), ((), ())), preferred_element_type=F32)


def _rms(x, gain):
    ms = jnp.mean(x * x, axis=-1, keepdims=True)
    return x * lax.rsqrt(ms + NORM_EPS) * gain


def _stream_tiles(tiles, stage_ref, sem_ref):
    copies = [pltpu.make_async_copy(src, stage_ref.at[t % STAGE_SLOTS], sem_ref.at[t % STAGE_SLOTS])
              for t, (src, _) in enumerate(tiles)]
    depth = STAGE_SLOTS - 1
    for cp in copies[:depth]:
        cp.start()
    for t, (_, consume) in enumerate(tiles):
        if t + depth < len(tiles):
            copies[t + depth].start()
        copies[t].wait()
        consume(stage_ref.at[t % STAGE_SLOTS])


def _inproj_kernel(x_ref, g_ref, cos_ref, sin_ref, qg_ref, kg_ref, w_ref, up_ref, gbias_ref,
                   qd_ref, ki_ref, kt_ref, tot_ref, v_ref, r_ref, sq_ref, sk_ref, sv_ref,
                   ga_ref, gb_ref, whead_ref, wtail_ref, stage_ref, sem_ref):
    tm = x_ref.shape[0]
    c = GLA_CHUNK

    @pl.when(pl.program_id(0) == 0)
    def _():
        tiles = []
        for dst_ref, row0 in ((whead_ref, 0), (wtail_ref, W_IN_TAIL)):
            for c0 in range(0, dst_ref.shape[1], LANES):
                def cast(tile_ref, dst_ref=dst_ref, c0=c0):
                    dst_ref[:, c0:c0 + LANES] = tile_ref[...].T.astype(BF16)
                tiles.append((w_ref.at[row0 + c0:row0 + c0 + LANES, :], cast))
        _stream_tiles(tiles, stage_ref, sem_ref)

    w_ref = whead_ref
    wq_ref, wk_ref, wv_ref, wr_ref = [
        w_ref.at[:, a:b] for a, b in zip(W_IN_OFFSETS[:4], W_IN_OFFSETS[1:5])]
    wsq_ref, wsk_ref, wsv_ref, wga_ref, wgb_ref = [
        wtail_ref.at[:, a - W_IN_TAIL:b - W_IN_TAIL]
        for a, b in zip(W_IN_OFFSETS[6:-1], W_IN_OFFSETS[7:])]
    wlr_ref = w_ref.at[:, W_IN_CODES:W_IN_CODES + LANES]
    h = _rms(x_ref[...], g_ref[...]).astype(BF16)
    cos = cos_ref[...]
    sin = sin_ref[...]

    def norm_rope(t, gain, scale):
        y = _rms(t, gain)
        rot = pltpu.roll(y, SWA_HEAD_DIM // 2, axis=1)
        return ((y * cos + rot * sin) * scale).astype(BF16)

    hd = SWA_HEAD_DIM

    def rope_pair(w_ref, o_ref, gain_ref, scale, c0):
        t = _dot(h, w_ref[:, c0:c0 + MXU_COLS])
        for c1 in range(0, MXU_COLS, hd):
            o_ref[:, c0 + c1:c0 + c1 + hd] = norm_rope(t[:, c1:c1 + hd], gain_ref[...], scale)

    def v_head(hh):
        v_ref[hh] = _dot(h, wv_ref[:, hh * GLA_DV:(hh + 1) * GLA_DV]).astype(BF16)

    def r_head(hh):
        r_ref[hh] = jax.nn.silu(_dot(h, wr_ref[:, hh * GLA_DV:(hh + 1) * GLA_DV])).astype(BF16)

    rope_pair(wsk_ref, sk_ref, kg_ref, 1.0, 0)
    for pair in range(SWA_Q_W // MXU_COLS):
        rope_pair(wsq_ref, sq_ref, qg_ref, hd ** -0.5 * LOG2_E, pair * MXU_COLS)

    lr = _dot(h, wlr_ref[...]).astype(BF16)
    ga_ref[...] = jax.nn.sigmoid(_dot(h, wga_ref[...])).astype(BF16)
    z = _dot(lr, up_ref[...]) + gbias_ref[...]
    log2_g = jnp.maximum(
        (jnp.minimum(z, 0.0) * LOG2_E - jnp.log2(1.0 + jnp.exp2(jnp.abs(z) * -LOG2_E)))
        * (1.0 / GLA_GATE_NORMALIZER), GLA_LOG_GATE_MIN * LOG2_E)
    hi = log2_g.astype(BF16)
    lo = (log2_g - hi.astype(F32)).astype(BF16)

    v_head(0)
    v_head(1)
    gb_ref[...] = jax.nn.sigmoid(_dot(h, wgb_ref[...])).astype(BF16)
    q = _dot(h, wq_ref[...]) * (GLA_DK ** -0.5)
    k = _dot(h, wk_ref[...])
    v_head(2)
    v_head(3)

    ri = lax.broadcasted_iota(jnp.int32, (c, 2 * c), 0)
    ci = lax.broadcasted_iota(jnp.int32, (c, 2 * c), 1) % c
    tri = (jnp.where(ri >= ci, 1.0, 0.0).astype(BF16), jnp.where(ci >= ri, 1.0, 0.0).astype(BF16))
    total_row = (c - 1, 0)
    for t in range(tm // c):
        rows = slice(t * c, (t + 1) * c)
        totals = []
        for d in range(2):
            cols = slice(d * GLA_QK_W, (d + 1) * GLA_QK_W)
            b = _dot(tri[d], jnp.concatenate([hi[rows, cols], lo[rows, cols]], axis=0))
            decay = jnp.exp2(b)
            total = decay[total_row[d]:total_row[d] + 1, :]
            totals.append(total)
            q_dec = (q[rows, :] * decay).astype(BF16)
            k_inv = k[rows, :] * jnp.exp2(-b)
            k_tail = (k_inv * total).astype(BF16)
            k_inv = k_inv.astype(BF16)
            for hh in range(GLA_HEADS):
                src = slice(hh * GLA_DK, (hh + 1) * GLA_DK)
                dst = slice(d * GLA_DK, (d + 1) * GLA_DK)
                qd_ref[hh, rows, dst] = q_dec[:, src]
                ki_ref[hh, dst, rows] = k_inv[:, src].T
                kt_ref[hh, rows, dst] = k_tail[:, src]
        tot_ref[t] = jnp.concatenate(
            [totals[d][:, hh * GLA_DK:(hh + 1) * GLA_DK]
             for hh in range(GLA_HEADS) for d in range(2)], axis=1)
        if t < GLA_HEADS:
            r_head(t)

    for hh in range(tm // c, GLA_HEADS):
        r_head(hh)
    sv_ref[...] = _dot(h, wsv_ref[...]).astype(BF16)


def _in_projection(x2, norm_g, cos, sin, q_g, k_g, weights, seq):
    m = x2.shape[0]
    tm = IN_TILE_M
    pos_blocks = seq // tm
    row = lambda w: pl.BlockSpec((tm, w), lambda i: (i, 0))
    pos = pl.BlockSpec((tm, SWA_HEAD_DIM), lambda i: (i % pos_blocks, 0))
    head_major = (pl.BlockSpec((GLA_HEADS, tm, GLA_DV), lambda i: (0, i, 0)),
                  jax.ShapeDtypeStruct((GLA_HEADS, m, GLA_DV), BF16))
    tot = (pl.BlockSpec((tm // GLA_CHUNK, 1, 2 * GLA_QK_W), lambda i: (i, 0, 0)),
           jax.ShapeDtypeStruct((m // GLA_CHUNK, 1, 2 * GLA_QK_W), F32))
    flat = lambda w: (row(w), jax.ShapeDtypeStruct((m, w), BF16))
    ki_t = (pl.BlockSpec((GLA_HEADS, 2 * GLA_DK, tm), lambda i: (0, 0, i)),
            jax.ShapeDtypeStruct((GLA_HEADS, 2 * GLA_DK, m), BF16))
    outs = (head_major, ki_t, head_major, tot, head_major, head_major,
            flat(SWA_Q_W), flat(SWA_KV_W), flat(SWA_KV_W), flat(D_MODEL), flat(D_MODEL))
    return pl.pallas_call(
        _inproj_kernel,
        grid=(m // tm,),
        in_specs=[row(D_MODEL), _const_spec((1, D_MODEL)), pos, pos,
                  _const_spec((1, SWA_HEAD_DIM)), _const_spec((1, SWA_HEAD_DIM)),
                  pl.BlockSpec(memory_space=pl.ANY)]
                 + [_const_spec(w.shape) for w in weights[1:]],
        out_specs=[spec for spec, _ in outs],
        out_shape=[shape for _, shape in outs],
        scratch_shapes=[pltpu.VMEM((D_MODEL, W_IN_CODES + LANES), BF16),
                        pltpu.VMEM((D_MODEL, W_IN_OFFSETS[-1] - W_IN_TAIL), BF16),
                        pltpu.VMEM((STAGE_SLOTS, LANES, D_MODEL), F32),
                        pltpu.SemaphoreType.DMA((STAGE_SLOTS,))],
        compiler_params=pltpu.CompilerParams(
            dimension_semantics=("arbitrary",), vmem_limit_bytes=VMEM_LIMIT),
        name="in_projection",
    )(x2, norm_g, cos, sin, q_g, k_g, *weights)


def _gla_kernel(qd_ref, ki_ref, kt_ref, tot_ref, v_ref, r_ref, og_ref, o_ref, sf_ref, kvb_ref):
    c = GLA_CHUNK
    dk = GLA_DK
    n_chunks = v_ref.shape[0] // c
    lower_incl = (lax.broadcasted_iota(jnp.int32, (c, c), 0)
                  >= lax.broadcasted_iota(jnp.int32, (c, c), 1))
    rows = [slice(n * c, (n + 1) * c) for n in range(n_chunks)]

    kv_t = [_dot_tn(v_ref[r, :], kt_ref[r, :]) for r in rows]
    s = jnp.zeros((GLA_DV, dk), F32)
    for n in range(n_chunks):
        sf_ref[n] = s.astype(BF16).T
        kvb_ref[n] = kv_t[n][:, dk:]
        s = s * tot_ref[n][:, :dk] + kv_t[n][:, :dk]

    order = range(n_chunks - 1, -1, -1)
    qd = {n: qd_ref[rows[n], :] for n in order}
    a = {n: jnp.where(lower_incl, _dot(qd[n][:, :dk], ki_ref[:dk, rows[n]]),
                      _dot(qd[n][:, dk:], ki_ref[dk:, rows[n]])).astype(BF16) for n in order}
    s = jnp.zeros((GLA_DV, dk), F32)
    out = {}
    for n in order:
        both = jnp.concatenate([sf_ref[n], s.astype(BF16).T], axis=0)
        out[n] = _dot(a[n], v_ref[rows[n], :]) + _dot(qd[n], both)
        s = s * tot_ref[n][:, dk:] + kvb_ref[n]
    for n in order:
        o_ref[rows[n], :] = (_rms(out[n], og_ref[...])
                             * r_ref[rows[n], :].astype(F32)).astype(BF16)


def _gla_branch(qd, ki, kt, tot, v, r, out_g, batch, seq):
    n_chunks = seq // GLA_CHUNK
    assert 2 * GLA_DK == GLA_DV
    vv = pl.BlockSpec((None, seq, GLA_DV), lambda b, h: (h, b, 0))
    return pl.pallas_call(
        _gla_kernel,
        grid=(batch, GLA_HEADS),
        in_specs=[vv, pl.BlockSpec((None, 2 * GLA_DK, seq), lambda b, h: (h, 0, b)), vv,
                  pl.BlockSpec((n_chunks, 1, 2 * GLA_DK), lambda b, h: (b, 0, h)),
                  vv, vv, _const_spec((1, GLA_DV))],
        out_specs=vv,
        out_shape=jax.ShapeDtypeStruct((GLA_HEADS, batch * seq, GLA_DV), BF16),
        scratch_shapes=[pltpu.VMEM((n_chunks, GLA_DK, GLA_DV), BF16),
                        pltpu.VMEM((n_chunks, GLA_DV, GLA_DK), F32)],
        compiler_params=pltpu.CompilerParams(
            dimension_semantics=("parallel", "parallel"), vmem_limit_bytes=VMEM_LIMIT),
        name="gla_branch",
    )(qd, ki, kt, tot, v, r, out_g)


def _swa_kernel(sink_ref, q_ref, k_ref, v_ref, o_ref):
    blk = SWA_BLOCK
    hd = SWA_HEAD_DIM
    tile_blocks = q_ref.shape[0] // blk
    seq_blocks = k_ref.shape[0] // blk
    tile = pl.program_id(1)
    rows_g = SWA_GROUP * blk
    qi = lax.broadcasted_iota(jnp.int32, (rows_g, blk), 0) % blk
    kj = lax.broadcasted_iota(jnp.int32, (rows_g, blk), 1)
    ones = jnp.ones((3 * blk, hd), BF16)
    head_cols = lambda hh: slice(hh * hd, (hh + 1) * hd)

    def logits(t0):
        jobs = [(t0 + t, hk) for t in range(SWA_UNROLL) for hk in range(SWA_KV_HEADS)]
        out = []
        for t, hk in jobs:
            n = tile * tile_blocks + t
            q4 = jnp.concatenate([q_ref[t * blk:(t + 1) * blk, head_cols(hk * SWA_GROUP + g)]
                                  for g in range(SWA_GROUP)], axis=0)
            rows = [pl.ds(pl.multiple_of(nn * blk, blk), blk) for nn in
                    (jnp.maximum(n - 1, 0), n, jnp.minimum(n + 1, seq_blocks - 1))]
            masks = [kj >= qi + jnp.where(n > 0, 0, blk), None,
                     kj <= qi - jnp.where(n < seq_blocks - 1, 0, blk)]
            s = [_dot_nt(q4, k_ref[r, head_cols(hk)]) for r in rows]
            s = [sj if mask is None else jnp.where(mask, sj, MASK_VALUE)
                 for sj, mask in zip(s, masks)]
            out.append((t, hk, rows, s))
        return out

    def attend(group):
        for t, hk, rows, s in group:
            row_max = jnp.broadcast_to(
                jnp.maximum(jnp.maximum(s[0], s[1]), s[2]).max(axis=-1, keepdims=True),
                (rows_g, blk))
            sinks = [sink_ref[hk * SWA_GROUP + g] * LOG2_E for g in range(SWA_GROUP)]
            mg = [jnp.maximum(row_max[g * blk:(g + 1) * blk], sk) for g, sk in enumerate(sinks)]
            m = jnp.concatenate(mg, axis=0)
            sink_w = jnp.concatenate([jnp.exp2(sk - x) for sk, x in zip(sinks, mg)], axis=0)
            p = jnp.concatenate([jnp.exp2(sj - m).astype(BF16) for sj in s], axis=1)
            acc = _dot(p, jnp.concatenate(
                [jnp.concatenate([v_ref[r, head_cols(hk)] for r in rows], axis=0), ones], axis=1))
            o = acc[:, :hd] / (acc[:, hd:] + sink_w)
            for g in range(SWA_GROUP):
                o_ref[t * blk:(t + 1) * blk, head_cols(hk * SWA_GROUP + g)] = (
                    o[g * blk:(g + 1) * blk, :].astype(BF16))

    starts = list(range(0, tile_blocks, SWA_UNROLL))
    pending = logits(starts[0])
    for t0 in starts[1:]:
        upcoming = logits(t0)
        attend(pending)
        pending = upcoming
    attend(pending)


def _swa_branch(sq, sk, sv, sinks, batch, seq):
    tq = SWA_TILE_Q
    tiles = seq // tq
    return pl.pallas_call(
        _swa_kernel,
        grid=(batch, tiles),
        in_specs=[pl.BlockSpec(memory_space=pltpu.SMEM),
                  pl.BlockSpec((tq, SWA_Q_W), lambda b, t: (b * tiles + t, 0)),
                  pl.BlockSpec((seq, SWA_KV_W), lambda b, t: (b, 0)),
                  pl.BlockSpec((seq, SWA_KV_W), lambda b, t: (b, 0))],
        out_specs=pl.BlockSpec((tq, SWA_Q_W), lambda b, t: (b * tiles + t, 0)),
        out_shape=jax.ShapeDtypeStruct((batch * seq, SWA_Q_W), BF16),
        compiler_params=pltpu.CompilerParams(
            dimension_semantics=("parallel", "parallel"), vmem_limit_bytes=VMEM_LIMIT),
        name="swa_branch",
    )(sinks, sq, sk, sv)


def _out_ffn_kernel(x_ref, og_ref, os_ref, ga_ref, gb_ref, g2_ref,
                    wog_hbm, wos_hbm, wout_hbm, wfi_hbm, wfo_hbm, o_ref,
                    wog_ref, wos_ref, wout_ref, wfi_ref, wfo_ref, stage_ref, sem_ref):
    @pl.when(pl.program_id(0) == 0)
    def _():
        tr, tc = stage_ref.shape[1:]
        tiles = []
        for src, dst in ((wog_hbm, wog_ref), (wos_hbm, wos_ref), (wout_hbm, wout_ref),
                         (wfi_hbm, wfi_ref), (wfo_hbm, wfo_ref)):
            for r0 in range(0, src.shape[0], tr):
                for c0 in range(0, src.shape[1], tc):
                    def cast(tile_ref, dst=dst, r0=r0, c0=c0):
                        dst[r0:r0 + tr, c0:c0 + tc] = tile_ref[...].astype(BF16)
                    tiles.append((src.at[r0:r0 + tr, c0:c0 + tc], cast))
        _stream_tiles(tiles, stage_ref, sem_ref)

    o_gla = jnp.concatenate([og_ref[hh] for hh in range(GLA_HEADS)], axis=1)
    y_gla = _dot(o_gla, wog_ref[...])
    y_swa = _dot(os_ref[...], wos_ref[...])
    merged = ga_ref[...].astype(F32) * y_gla + gb_ref[...].astype(F32) * y_swa
    x1 = x_ref[...] + _dot(merged.astype(BF16), wout_ref[...])
    h2 = _rms(x1, g2_ref[...]).astype(BF16)
    acc = x1
    for c0 in range(0, D_FF, FFN_CHUNK):
        cols = slice(c0, c0 + FFN_CHUNK)
        gate = _dot(h2, wfi_ref[:, cols])
        up = _dot(h2, wfi_ref[:, D_FF + c0:D_FF + c0 + FFN_CHUNK])
        act = (jax.nn.silu(gate) * up).astype(BF16)
        acc = acc + _dot(act, wfo_ref[cols, :])
    o_ref[...] = acc


def _out_ffn(x2, o_gla, o_swa, ga, gb, norm_g, *weights):
    m = x2.shape[0]
    tm = OUT_TILE_M
    row = pl.BlockSpec((tm, D_MODEL), lambda i: (i, 0))
    return pl.pallas_call(
        _out_ffn_kernel,
        grid=(m // tm,),
        in_specs=[row, pl.BlockSpec((GLA_HEADS, tm, GLA_DV), lambda i: (0, i, 0))] + [row] * 3
                 + [_const_spec(norm_g.shape)]
                 + [pl.BlockSpec(memory_space=pl.ANY)] * len(weights),
        out_specs=row,
        out_shape=jax.ShapeDtypeStruct((m, D_MODEL), F32),
        scratch_shapes=[pltpu.VMEM(w.shape, BF16) for w in weights]
                       + [pltpu.VMEM((STAGE_SLOTS,) + OUT_STAGE_TILE, F32),
                          pltpu.SemaphoreType.DMA((STAGE_SLOTS,))],
        compiler_params=pltpu.CompilerParams(
            dimension_semantics=("arbitrary",), vmem_limit_bytes=VMEM_LIMIT),
        name="out_ffn",
    )(x2, o_gla, o_swa, ga, gb, norm_g, *weights)


def _rope_tables(seq):
    half = SWA_HEAD_DIM // 2
    inv_freq = ROPE_THETA ** (-np.arange(half, dtype=np.float64) / half)
    ang = np.arange(seq, dtype=np.float64)[:, None] * inv_freq[None, :]
    cos, sin = np.cos(ang), np.sin(ang)
    return (jnp.asarray(np.concatenate([cos, cos], axis=-1), F32),
            jnp.asarray(np.concatenate([-sin, sin], axis=-1), F32))


def _gate_up(up_f, up_b):
    zeros = jnp.zeros_like(up_f)
    top = jnp.concatenate([up_f, zeros], axis=1)
    bot = jnp.concatenate([zeros, up_b], axis=1)
    pad = jnp.zeros((LANES - 2 * GLA_GATE_RANK, 2 * GLA_QK_W), up_f.dtype)
    return jnp.concatenate([top, bot, pad], axis=0).astype(BF16)


def _layer(x2, batch, seq, norm_mix_g, w_in, up_f, bias_f, up_b, bias_b, gla_out_g, w_o_gla,
           q_g, k_g, sinks, w_o_swa, w_out, norm_ffn_g, w_ffn_in, w_ffn_out):
    cos, sin = _rope_tables(seq)
    row = lambda t: t.reshape(1, -1)
    qd, ki, kt, tot, v, r, sq, sk, sv, ga, gb = _in_projection(
        x2, row(norm_mix_g), cos, sin, row(q_g), row(k_g),
        (w_in.T, _gate_up(up_f, up_b), row(jnp.concatenate([bias_f, bias_b]))), seq)
    o_gla = _gla_branch(qd, ki, kt, tot, v, r, row(gla_out_g), batch, seq)
    o_swa = _swa_branch(sq, sk, sv, sinks, batch, seq)
    return _out_ffn(x2, o_gla, o_swa, ga, gb, row(norm_ffn_g),
                    w_o_gla, w_o_swa, w_out, w_ffn_in, w_ffn_out)


def kernel(x, norm_mix_g, w_in, gla_gate_up_fwd, gla_gate_bias_fwd, gla_gate_up_bwd,
           gla_gate_bias_bwd, gla_out_norm_g, w_o_gla, swa_q_norm_g, swa_k_norm_g,
           swa_sinks, w_o_swa, w_out, norm_ffn_g, w_ffn_in, w_ffn_out):
    batch, seq, d = x.shape
    x2 = x.reshape(batch * seq, d)
    for l in range(w_in.shape[0]):
        x2 = _layer(x2, batch, seq, norm_mix_g[l], w_in[l], gla_gate_up_fwd[l],
                    gla_gate_bias_fwd[l], gla_gate_up_bwd[l], gla_gate_bias_bwd[l],
                    gla_out_norm_g[l], w_o_gla[l], swa_q_norm_g[l], swa_k_norm_g[l],
                    swa_sinks[l], w_o_swa[l], w_out[l], norm_ffn_g[l], w_ffn_in[l],
                    w_ffn_out[l])
    return x2.reshape(batch, seq, d)
```

```python
import math

import jax
import jax.numpy as jnp
import numpy as np
from jax import lax
from jax.experimental import pallas as pl
from jax.experimental.pallas import tpu as pltpu

F32 = jnp.float32
BF16 = jnp.bfloat16

D_MODEL = 1024
NORM_EPS = 1e-6
GLA_HEADS = 4
GLA_DK = 128
GLA_DV = 256
GLA_QK_W = GLA_HEADS * GLA_DK
GLA_V_W = GLA_HEADS * GLA_DV
GLA_GATE_RANK = 16
GLA_GATE_NORMALIZER = 16.0
GLA_LOG_GATE_MIN = -0.5
GLA_CHUNK = 128
SWA_HEADS = 8
SWA_KV_HEADS = 2
SWA_GROUP = SWA_HEADS // SWA_KV_HEADS
SWA_HEAD_DIM = 128
SWA_Q_W = SWA_HEADS * SWA_HEAD_DIM
SWA_KV_W = SWA_KV_HEADS * SWA_HEAD_DIM
SWA_BLOCK = 128
ROPE_THETA = 10000.0
D_FF = 2816
IN_SPLITS = (GLA_QK_W, GLA_QK_W, GLA_V_W, GLA_V_W, GLA_GATE_RANK, GLA_GATE_RANK,
             SWA_Q_W, SWA_KV_W, SWA_KV_W, D_MODEL, D_MODEL)

LANES = 128
W_IN_OFFSETS = tuple(int(o) for o in np.cumsum((0,) + IN_SPLITS))
W_IN_CODES = W_IN_OFFSETS[4]
W_IN_TAIL = W_IN_OFFSETS[6]
MXU_COLS = 256
MASK_VALUE = -1e30
LOG2_E = math.log2(math.e)
VMEM_LIMIT = 56 * 1024 * 1024

IN_TILE_M = 512
SWA_TILE_Q = 2048
SWA_UNROLL = 4
OUT_TILE_M = 512
FFN_CHUNK = MXU_COLS
STAGE_SLOTS = 8
OUT_STAGE_TILE = (256, 512)


def _const_spec(shape):
    zeros = (0,) * len(shape)
    return pl.BlockSpec(shape, lambda *_: zeros, pipeline_mode=pl.Buffered(1))


def _dot(a, b):
    return jnp.dot(a, b, preferred_element_type=F32)


def _dot_nt(a, b):
    return lax.dot_general(a, b, (((1,), (1,)), ((), ())), preferred_element_type=F32)


def _dot_tn(a, b):
    return lax.dot_general(a, b, (((0,), (0,)), ((), ())), preferred_element_type=F32)


def _rms(x, gain):
    ms = jnp.mean(x * x, axis=-1, keepdims=True)
    return x * lax.rsqrt(ms + NORM_EPS) * gain


def _stream_tiles(tiles, stage_ref, sem_ref):
    copies = [pltpu.make_async_copy(src, stage_ref.at[t % STAGE_SLOTS], sem_ref.at[t % STAGE_SLOTS])
              for t, (src, _) in enumerate(tiles)]
    depth = STAGE_SLOTS - 1
    for cp in copies[:depth]:
        cp.start()
    for t, (_, consume) in enumerate(tiles):
        if t + depth < len(tiles):
            copies[t + depth].start()
        copies[t].wait()
        consume(stage_ref.at[t % STAGE_SLOTS])


def _inproj_kernel(x_ref, g_ref, cos_ref, sin_ref, qg_ref, kg_ref, w_ref, up_ref, gbias_ref,
                   qd_ref, ki_ref, tot_ref, v_ref, r_ref, sq_ref, sk_ref, sv_ref,
                   ga_ref, gb_ref, whead_ref, wtail_ref, stage_ref, sem_ref):
    tm = x_ref.shape[0]
    c = GLA_CHUNK

    @pl.when(pl.program_id(0) == 0)
    def _():
        tiles = []
        for dst_ref, row0 in ((whead_ref, 0), (wtail_ref, W_IN_TAIL)):
            for c0 in range(0, dst_ref.shape[1], LANES):
                def cast(tile_ref, dst_ref=dst_ref, c0=c0):
                    dst_ref[:, c0:c0 + LANES] = tile_ref[...].T.astype(BF16)
                tiles.append((w_ref.at[row0 + c0:row0 + c0 + LANES, :], cast))
        _stream_tiles(tiles, stage_ref, sem_ref)

    w_ref = whead_ref
    wq_ref, wk_ref, wv_ref, wr_ref = [
        w_ref.at[:, a:b] for a, b in zip(W_IN_OFFSETS[:4], W_IN_OFFSETS[1:5])]
    wsq_ref, wsk_ref, wsv_ref, wga_ref, wgb_ref = [
        wtail_ref.at[:, a - W_IN_TAIL:b - W_IN_TAIL]
        for a, b in zip(W_IN_OFFSETS[6:-1], W_IN_OFFSETS[7:])]
    wlr_ref = w_ref.at[:, W_IN_CODES:W_IN_CODES + LANES]
    h = _rms(x_ref[...], g_ref[...]).astype(BF16)
    cos = cos_ref[...]
    sin = sin_ref[...]

    def norm_rope(t, gain, scale):
        y = _rms(t, gain)
        rot = pltpu.roll(y, SWA_HEAD_DIM // 2, axis=1)
        return ((y * cos + rot * sin) * scale).astype(BF16)

    hd = SWA_HEAD_DIM

    def rope_pair(w_ref, o_ref, gain_ref, scale, c0):
        t = _dot(h, w_ref[:, c0:c0 + MXU_COLS])
        for c1 in range(0, MXU_COLS, hd):
            o_ref[:, c0 + c1:c0 + c1 + hd] = norm_rope(t[:, c1:c1 + hd], gain_ref[...], scale)

    def v_head(hh):
        v_ref[hh] = _dot(h, wv_ref[:, hh * GLA_DV:(hh + 1) * GLA_DV]).astype(BF16)

    def r_head(hh):
        r_ref[hh] = jax.nn.silu(_dot(h, wr_ref[:, hh * GLA_DV:(hh + 1) * GLA_DV])).astype(BF16)

    rope_pair(wsk_ref, sk_ref, kg_ref, 1.0, 0)
    for pair in range(SWA_Q_W // MXU_COLS):
        rope_pair(wsq_ref, sq_ref, qg_ref, hd ** -0.5 * LOG2_E, pair * MXU_COLS)

    lr = _dot(h, wlr_ref[...]).astype(BF16)
    ga_ref[...] = jax.nn.sigmoid(_dot(h, wga_ref[...])).astype(BF16)
    z = _dot(lr, up_ref[...]) + gbias_ref[...]
    log2_g = jnp.maximum(
        (jnp.minimum(z, 0.0) * LOG2_E - jnp.log2(1.0 + jnp.exp2(jnp.abs(z) * -LOG2_E)))
        * (1.0 / GLA_GATE_NORMALIZER), GLA_LOG_GATE_MIN * LOG2_E)
    hi = log2_g.astype(BF16)
    lo = (log2_g - hi.astype(F32)).astype(BF16)

    v_head(0)
    v_head(1)
    gb_ref[...] = jax.nn.sigmoid(_dot(h, wgb_ref[...])).astype(BF16)
    q = _dot(h, wq_ref[...]) * (GLA_DK ** -0.5)
    k = _dot(h, wk_ref[...])
    v_head(2)
    v_head(3)

    ri = lax.broadcasted_iota(jnp.int32, (c, 2 * c), 0)
    ci = lax.broadcasted_iota(jnp.int32, (c, 2 * c), 1) % c
    tri = (jnp.where(ri >= ci, 1.0, 0.0).astype(BF16), jnp.where(ci >= ri, 1.0, 0.0).astype(BF16))
    total_row = (c - 1, 0)
    for t in range(tm // c):
        rows = slice(t * c, (t + 1) * c)
        totals = []
        for d in range(2):
            cols = slice(d * GLA_QK_W, (d + 1) * GLA_QK_W)
            b = _dot(tri[d], jnp.concatenate([hi[rows, cols], lo[rows, cols]], axis=0))
            decay = jnp.exp2(b)
            total = decay[total_row[d]:total_row[d] + 1, :]
            totals.append(total)
            q_dec = (q[rows, :] * decay).astype(BF16)
            k_inv = (k[rows, :] * jnp.exp2(-b)).astype(BF16)
            for hh in range(GLA_HEADS):
                src = slice(hh * GLA_DK, (hh + 1) * GLA_DK)
                dst = slice(d * GLA_DK, (d + 1) * GLA_DK)
                qd_ref[hh, rows, dst] = q_dec[:, src]
                ki_ref[hh, rows, dst] = k_inv[:, src]
        tot_ref[t] = jnp.concatenate(
            [totals[d][:, hh * GLA_DK:(hh + 1) * GLA_DK]
             for hh in range(GLA_HEADS) for d in range(2)], axis=1)
        if t < GLA_HEADS:
            r_head(t)

    for hh in range(tm // c, GLA_HEADS):
        r_head(hh)
    sv_ref[...] = _dot(h, wsv_ref[...]).astype(BF16)


def _in_projection(x2, norm_g, cos, sin, q_g, k_g, weights, seq):
    m = x2.shape[0]
    tm = IN_TILE_M
    pos_blocks = seq // tm
    row = lambda w: pl.BlockSpec((tm, w), lambda i: (i, 0))
    pos = pl.BlockSpec((tm, SWA_HEAD_DIM), lambda i: (i % pos_blocks, 0))
    head_major = (pl.BlockSpec((GLA_HEADS, tm, GLA_DV), lambda i: (0, i, 0)),
                  jax.ShapeDtypeStruct((GLA_HEADS, m, GLA_DV), BF16))
    tot = (pl.BlockSpec((tm // GLA_CHUNK, 1, 2 * GLA_QK_W), lambda i: (i, 0, 0)),
           jax.ShapeDtypeStruct((m // GLA_CHUNK, 1, 2 * GLA_QK_W), F32))
    flat = lambda w: (row(w), jax.ShapeDtypeStruct((m, w), BF16))
    outs = (head_major, head_major, tot, head_major, head_major,
            flat(SWA_Q_W), flat(SWA_KV_W), flat(SWA_KV_W), flat(D_MODEL), flat(D_MODEL))
    return pl.pallas_call(
        _inproj_kernel,
        grid=(m // tm,),
        in_specs=[row(D_MODEL), _const_spec((1, D_MODEL)), pos, pos,
                  _const_spec((1, SWA_HEAD_DIM)), _const_spec((1, SWA_HEAD_DIM)),
                  pl.BlockSpec(memory_space=pl.ANY)]
                 + [_const_spec(w.shape) for w in weights[1:]],
        out_specs=[spec for spec, _ in outs],
        out_shape=[shape for _, shape in outs],
        scratch_shapes=[pltpu.VMEM((D_MODEL, W_IN_CODES + LANES), BF16),
                        pltpu.VMEM((D_MODEL, W_IN_OFFSETS[-1] - W_IN_TAIL), BF16),
                        pltpu.VMEM((STAGE_SLOTS, LANES, D_MODEL), F32),
                        pltpu.SemaphoreType.DMA((STAGE_SLOTS,))],
        compiler_params=pltpu.CompilerParams(
            dimension_semantics=("arbitrary",), vmem_limit_bytes=VMEM_LIMIT),
        name="in_projection",
    )(x2, norm_g, cos, sin, q_g, k_g, *weights)


def _gla_kernel(qd_ref, ki_ref, tot_ref, v_ref, r_ref, og_ref, o_ref, sf_ref, kvb_ref):
    c = GLA_CHUNK
    dk = GLA_DK
    n_chunks = v_ref.shape[0] // c
    lower_incl = (lax.broadcasted_iota(jnp.int32, (c, c), 0)
                  >= lax.broadcasted_iota(jnp.int32, (c, c), 1))
    rows = [slice(n * c, (n + 1) * c) for n in range(n_chunks)]

    s = jnp.zeros((GLA_DV, dk), F32)
    for n in range(n_chunks):
        kv_t = _dot_tn(v_ref[rows[n], :], ki_ref[rows[n], :])
        sf_ref[n] = s.astype(BF16).T
        kvb_ref[n] = kv_t[:, dk:]
        s = (s + kv_t[:, :dk]) * tot_ref[n][:, :dk]

    order = range(n_chunks - 1, -1, -1)
    qd = {n: qd_ref[rows[n], :] for n in order}
    a = {n: jnp.where(lower_incl, _dot_nt(qd[n][:, :dk], ki_ref[rows[n], :dk]),
                      _dot_nt(qd[n][:, dk:], ki_ref[rows[n], dk:])).astype(BF16) for n in order}
    s = jnp.zeros((GLA_DV, dk), F32)
    for n in order:
        both = jnp.concatenate([sf_ref[n], s.astype(BF16).T], axis=0)
        out = _dot(a[n], v_ref[rows[n], :]) + _dot(qd[n], both)
        s = (s + kvb_ref[n]) * tot_ref[n][:, dk:]
        o_ref[rows[n], :] = (_rms(out, og_ref[...])
                             * r_ref[rows[n], :].astype(F32)).astype(BF16)


def _gla_branch(qd, ki, tot, v, r, out_g, batch, seq):
    n_chunks = seq // GLA_CHUNK
    assert 2 * GLA_DK == GLA_DV
    vv = pl.BlockSpec((None, seq, GLA_DV), lambda b, h: (h, b, 0))
    return pl.pallas_call(
        _gla_kernel,
        grid=(batch, GLA_HEADS),
        in_specs=[vv, vv, pl.BlockSpec((n_chunks, 1, 2 * GLA_DK), lambda b, h: (b, 0, h)),
                  vv, vv, _const_spec((1, GLA_DV))],
        out_specs=vv,
        out_shape=jax.ShapeDtypeStruct((GLA_HEADS, batch * seq, GLA_DV), BF16),
        scratch_shapes=[pltpu.VMEM((n_chunks, GLA_DK, GLA_DV), BF16),
                        pltpu.VMEM((n_chunks, GLA_DV, GLA_DK), F32)],
        compiler_params=pltpu.CompilerParams(
            dimension_semantics=("parallel", "parallel"), vmem_limit_bytes=VMEM_LIMIT),
        name="gla_branch",
    )(qd, ki, tot, v, r, out_g)


def _swa_kernel(sink_ref, q_ref, k_ref, v_ref, o_ref):
    blk = SWA_BLOCK
    hd = SWA_HEAD_DIM
    tile_blocks = q_ref.shape[0] // blk
    seq_blocks = k_ref.shape[0] // blk
    tile = pl.program_id(1)
    rows_g = SWA_GROUP * blk
    qi = lax.broadcasted_iota(jnp.int32, (rows_g, blk), 0) % blk
    kj = lax.broadcasted_iota(jnp.int32, (rows_g, blk), 1)
    ones = jnp.ones((3 * blk, hd), BF16)
    head_cols = lambda hh: slice(hh * hd, (hh + 1) * hd)

    def logits(t0):
        jobs = [(t0 + t, hk) for t in range(SWA_UNROLL) for hk in range(SWA_KV_HEADS)]
        out = []
        for t, hk in jobs:
            n = tile * tile_blocks + t
            q4 = jnp.concatenate([q_ref[t * blk:(t + 1) * blk, head_cols(hk * SWA_GROUP + g)]
                                  for g in range(SWA_GROUP)], axis=0)
            rows = [pl.ds(pl.multiple_of(nn * blk, blk), blk) for nn in
                    (jnp.maximum(n - 1, 0), n, jnp.minimum(n + 1, seq_blocks - 1))]
            masks = [kj >= qi + jnp.where(n > 0, 0, blk), None,
                     kj <= qi - jnp.where(n < seq_blocks - 1, 0, blk)]
            s = [_dot_nt(q4, k_ref[r, head_cols(hk)]) for r in rows]
            s = [sj if mask is None else jnp.where(mask, sj, MASK_VALUE)
                 for sj, mask in zip(s, masks)]
            out.append((t, hk, rows, s))
        return out

    def attend(group):
        for t, hk, rows, s in group:
            row_max = jnp.broadcast_to(
                jnp.maximum(jnp.maximum(s[0], s[1]), s[2]).max(axis=-1, keepdims=True),
                (rows_g, blk))
            sinks = [sink_ref[hk * SWA_GROUP + g] * LOG2_E for g in range(SWA_GROUP)]
            mg = [jnp.maximum(row_max[g * blk:(g + 1) * blk], sk) for g, sk in enumerate(sinks)]
            m = jnp.concatenate(mg, axis=0)
            sink_w = jnp.concatenate([jnp.exp2(sk - x) for sk, x in zip(sinks, mg)], axis=0)
            p = jnp.concatenate([jnp.exp2(sj - m).astype(BF16) for sj in s], axis=1)
            acc = _dot(p, jnp.concatenate(
                [jnp.concatenate([v_ref[r, head_cols(hk)] for r in rows], axis=0), ones], axis=1))
            o = acc[:, :hd] / (acc[:, hd:] + sink_w)
            for g in range(SWA_GROUP):
                o_ref[t * blk:(t + 1) * blk, head_cols(hk * SWA_GROUP + g)] = (
                    o[g * blk:(g + 1) * blk, :].astype(BF16))

    starts = list(range(0, tile_blocks, SWA_UNROLL))
    pending = logits(starts[0])
    for t0 in starts[1:]:
        upcoming = logits(t0)
        attend(pending)
        pending = upcoming
    attend(pending)


def _swa_branch(sq, sk, sv, sinks, batch, seq):
    tq = SWA_TILE_Q
    tiles = seq // tq
    return pl.pallas_call(
        _swa_kernel,
        grid=(batch, tiles),
        in_specs=[pl.BlockSpec(memory_space=pltpu.SMEM),
                  pl.BlockSpec((tq, SWA_Q_W), lambda b, t: (b * tiles + t, 0)),
                  pl.BlockSpec((seq, SWA_KV_W), lambda b, t: (b, 0)),
                  pl.BlockSpec((seq, SWA_KV_W), lambda b, t: (b, 0))],
        out_specs=pl.BlockSpec((tq, SWA_Q_W), lambda b, t: (b * tiles + t, 0)),
        out_shape=jax.ShapeDtypeStruct((batch * seq, SWA_Q_W), BF16),
        compiler_params=pltpu.CompilerParams(
            dimension_semantics=("parallel", "parallel"), vmem_limit_bytes=VMEM_LIMIT),
        name="swa_branch",
    )(sinks, sq, sk, sv)


def _out_ffn_kernel(x_ref, og_ref, os_ref, ga_ref, gb_ref, g2_ref,
                    wog_hbm, wos_hbm, wout_hbm, wfi_hbm, wfo_hbm, o_ref,
                    wog_ref, wos_ref, wout_ref, wfi_ref, wfo_ref, stage_ref, sem_ref):
    @pl.when(pl.program_id(0) == 0)
    def _():
        tr, tc = stage_ref.shape[1:]
        tiles = []
        for src, dst in ((wog_hbm, wog_ref), (wos_hbm, wos_ref), (wout_hbm, wout_ref),
                         (wfi_hbm, wfi_ref), (wfo_hbm, wfo_ref)):
            for r0 in range(0, src.shape[0], tr):
                for c0 in range(0, src.shape[1], tc):
                    def cast(tile_ref, dst=dst, r0=r0, c0=c0):
                        dst[r0:r0 + tr, c0:c0 + tc] = tile_ref[...].astype(BF16)
                    tiles.append((src.at[r0:r0 + tr, c0:c0 + tc], cast))
        _stream_tiles(tiles, stage_ref, sem_ref)

    o_gla = jnp.concatenate([og_ref[hh] for hh in range(GLA_HEADS)], axis=1)
    y_gla = _dot(o_gla, wog_ref[...])
    y_swa = _dot(os_ref[...], wos_ref[...])
    merged = ga_ref[...].astype(F32) * y_gla + gb_ref[...].astype(F32) * y_swa
    x1 = x_ref[...] + _dot(merged.astype(BF16), wout_ref[...])
    h2 = _rms(x1, g2_ref[...]).astype(BF16)
    acc = x1
    for c0 in range(0, D_FF, FFN_CHUNK):
        cols = slice(c0, c0 + FFN_CHUNK)
        gate = _dot(h2, wfi_ref[:, cols])
        up = _dot(h2, wfi_ref[:, D_FF + c0:D_FF + c0 + FFN_CHUNK])
        act = (jax.nn.silu(gate) * up).astype(BF16)
        acc = acc + _dot(act, wfo_ref[cols, :])
    o_ref[...] = acc


def _out_ffn(x2, o_gla, o_swa, ga, gb, norm_g, *weights):
    m = x2.shape[0]
    tm = OUT_TILE_M
    row = pl.BlockSpec((tm, D_MODEL), lambda i: (i, 0))
    return pl.pallas_call(
        _out_ffn_kernel,
        grid=(m // tm,),
        in_specs=[row, pl.BlockSpec((GLA_HEADS, tm, GLA_DV), lambda i: (0, i, 0))] + [row] * 3
                 + [_const_spec(norm_g.shape)]
                 + [pl.BlockSpec(memory_space=pl.ANY)] * len(weights),
        out_specs=row,
        out_shape=jax.ShapeDtypeStruct((m, D_MODEL), F32),
        scratch_shapes=[pltpu.VMEM(w.shape, BF16) for w in weights]
                       + [pltpu.VMEM((STAGE_SLOTS,) + OUT_STAGE_TILE, F32),
                          pltpu.SemaphoreType.DMA((STAGE_SLOTS,))],
        compiler_params=pltpu.CompilerParams(
            dimension_semantics=("arbitrary",), vmem_limit_bytes=VMEM_LIMIT),
        name="out_ffn",
    )(x2, o_gla, o_swa, ga, gb, norm_g, *weights)


def _rope_tables(seq):
    half = SWA_HEAD_DIM // 2
    inv_freq = ROPE_THETA ** (-np.arange(half, dtype=np.float64) / half)
    ang = np.arange(seq, dtype=np.float64)[:, None] * inv_freq[None, :]
    cos, sin = np.cos(ang), np.sin(ang)
    return (jnp.asarray(np.concatenate([cos, cos], axis=-1), F32),
            jnp.asarray(np.concatenate([-sin, sin], axis=-1), F32))


def _gate_up(up_f, up_b):
    zeros = jnp.zeros_like(up_f)
    top = jnp.concatenate([up_f, zeros], axis=1)
    bot = jnp.concatenate([zeros, up_b], axis=1)
    pad = jnp.zeros((LANES - 2 * GLA_GATE_RANK, 2 * GLA_QK_W), up_f.dtype)
    return jnp.concatenate([top, bot, pad], axis=0).astype(BF16)


def _layer(x2, batch, seq, norm_mix_g, w_in, up_f, bias_f, up_b, bias_b, gla_out_g, w_o_gla,
           q_g, k_g, sinks, w_o_swa, w_out, norm_ffn_g, w_ffn_in, w_ffn_out):
    cos, sin = _rope_tables(seq)
    row = lambda t: t.reshape(1, -1)
    qd, ki, tot, v, r, sq, sk, sv, ga, gb = _in_projection(
        x2, row(norm_mix_g), cos, sin, row(q_g), row(k_g),
        (w_in.T, _gate_up(up_f, up_b), row(jnp.concatenate([bias_f, bias_b]))), seq)
    o_gla = _gla_branch(qd, ki, tot, v, r, row(gla_out_g), batch, seq)
    o_swa = _swa_branch(sq, sk, sv, sinks, batch, seq)
    return _out_ffn(x2, o_gla, o_swa, ga, gb, row(norm_ffn_g),
                    w_o_gla, w_o_swa, w_out, w_ffn_in, w_ffn_out)


def kernel(x, norm_mix_g, w_in, gla_gate_up_fwd, gla_gate_bias_fwd, gla_gate_up_bwd,
           gla_gate_bias_bwd, gla_out_norm_g, w_o_gla, swa_q_norm_g, swa_k_norm_g,
           swa_sinks, w_o_swa, w_out, norm_ffn_g, w_ffn_in, w_ffn_out):
    batch, seq, d = x.shape
    x2 = x.reshape(batch * seq, d)
    for l in range(w_in.shape[0]):
        x2 = _layer(x2, batch, seq, norm_mix_g[l], w_in[l], gla_gate_up_fwd[l],
                    gla_gate_bias_fwd[l], gla_gate_up_bwd[l], gla_gate_bias_bwd[l],
                    gla_out_norm_g[l], w_o_gla[l], swa_q_norm_g[l], swa_k_norm_g[l],
                    swa_sinks[l], w_o_swa[l], w_out[l], norm_ffn_g[l], w_ffn_in[l],
                    w_ffn_out[l])
    return x2.reshape(batch, seq, d)
```

```python
import math

import jax
import jax.numpy as jnp
import numpy as np
from jax import lax
from jax.experimental import pallas as pl
from jax.experimental.pallas import tpu as pltpu

F32 = jnp.float32
BF16 = jnp.bfloat16

D_MODEL = 1024
NORM_EPS = 1e-6
GLA_HEADS = 4
GLA_DK = 128
GLA_DV = 256
GLA_QK_W = GLA_HEADS * GLA_DK
GLA_V_W = GLA_HEADS * GLA_DV
GLA_GATE_RANK = 16
GLA_GATE_NORMALIZER = 16.0
GLA_LOG_GATE_MIN = -0.5
GLA_CHUNK = 128
SWA_HEADS = 8
SWA_KV_HEADS = 2
SWA_GROUP = SWA_HEADS // SWA_KV_HEADS
SWA_HEAD_DIM = 128
SWA_Q_W = SWA_HEADS * SWA_HEAD_DIM
SWA_KV_W = SWA_KV_HEADS * SWA_HEAD_DIM
SWA_BLOCK = 128
ROPE_THETA = 10000.0
D_FF = 2816
IN_SPLITS = (GLA_QK_W, GLA_QK_W, GLA_V_W, GLA_V_W, GLA_GATE_RANK, GLA_GATE_RANK,
             SWA_Q_W, SWA_KV_W, SWA_KV_W, D_MODEL, D_MODEL)

LANES = 128
W_IN_OFFSETS = tuple(int(o) for o in np.cumsum((0,) + IN_SPLITS))
W_IN_CODES = W_IN_OFFSETS[4]
W_IN_TAIL = W_IN_OFFSETS[6]
MXU_COLS = 256
MASK_VALUE = -1e30
LOG2_E = math.log2(math.e)
VMEM_LIMIT = 56 * 1024 * 1024

IN_TILE_M = 512
SWA_TILE_Q = 2048
SWA_UNROLL = 4
OUT_TILE_M = 512
FFN_CHUNK = MXU_COLS
STAGE_SLOTS = 8
OUT_STAGE_TILE = (256, 512)


def _const_spec(shape):
    zeros = (0,) * len(shape)
    return pl.BlockSpec(shape, lambda *_: zeros, pipeline_mode=pl.Buffered(1))


def _dot(a, b):
    return jnp.dot(a, b, preferred_element_type=F32)


def _dot_nt(a, b):
    return lax.dot_general(a, b, (((1,), (1,)), ((), ())), preferred_element_type=F32)


def _dot_tn(a, b):
    return lax.dot_general(a, b, (((0,), (0,)), ((), ())), preferred_element_type=F32)


def _rms(x, gain):
    ms = jnp.mean(x * x, axis=-1, keepdims=True)
    return x * lax.rsqrt(ms + NORM_EPS) * gain


def _stream_tiles(tiles, stage_ref, sem_ref):
    copies = [pltpu.make_async_copy(src, stage_ref.at[t % STAGE_SLOTS], sem_ref.at[t % STAGE_SLOTS])
              for t, (src, _) in enumerate(tiles)]
    depth = STAGE_SLOTS - 1
    for cp in copies[:depth]:
        cp.start()
    for t, (_, consume) in enumerate(tiles):
        if t + depth < len(tiles):
            copies[t + depth].start()
        copies[t].wait()
        consume(stage_ref.at[t % STAGE_SLOTS])


def _inproj_kernel(x_ref, g_ref, cos_ref, sin_ref, qg_ref, kg_ref, w_ref, up_ref, gbias_ref,
                   qd_ref, ki_ref, tot_ref, v_ref, r_ref, sq_ref, sk_ref, sv_ref,
                   ga_ref, gb_ref, whead_ref, wtail_ref, stage_ref, sem_ref):
    tm = x_ref.shape[0]
    c = GLA_CHUNK

    @pl.when(pl.program_id(0) == 0)
    def _():
        tiles = []
        for dst_ref, row0 in ((whead_ref, 0), (wtail_ref, W_IN_TAIL)):
            for c0 in range(0, dst_ref.shape[1], LANES):
                def cast(tile_ref, dst_ref=dst_ref, c0=c0):
                    dst_ref[:, c0:c0 + LANES] = tile_ref[...].T.astype(BF16)
                tiles.append((w_ref.at[row0 + c0:row0 + c0 + LANES, :], cast))
        _stream_tiles(tiles, stage_ref, sem_ref)

    w_ref = whead_ref
    wq_ref, wk_ref, wv_ref, wr_ref = [
        w_ref.at[:, a:b] for a, b in zip(W_IN_OFFSETS[:4], W_IN_OFFSETS[1:5])]
    wsq_ref, wsk_ref, wsv_ref, wga_ref, wgb_ref = [
        wtail_ref.at[:, a - W_IN_TAIL:b - W_IN_TAIL]
        for a, b in zip(W_IN_OFFSETS[6:-1], W_IN_OFFSETS[7:])]
    wlr_ref = w_ref.at[:, W_IN_CODES:W_IN_CODES + LANES]
    h = _rms(x_ref[...], g_ref[...]).astype(BF16)
    cos = cos_ref[...]
    sin = sin_ref[...]

    def norm_rope(t, gain, scale):
        y = _rms(t, gain)
        rot = pltpu.roll(y, SWA_HEAD_DIM // 2, axis=1)
        return ((y * cos + rot * sin) * scale).astype(BF16)

    hd = SWA_HEAD_DIM

    def rope_pair(w_ref, o_ref, gain_ref, scale, c0):
        t = _dot(h, w_ref[:, c0:c0 + MXU_COLS])
        for c1 in range(0, MXU_COLS, hd):
            o_ref[:, c0 + c1:c0 + c1 + hd] = norm_rope(t[:, c1:c1 + hd], gain_ref[...], scale)

    def v_head(hh):
        v_ref[hh] = _dot(h, wv_ref[:, hh * GLA_DV:(hh + 1) * GLA_DV]).astype(BF16)

    def r_head(hh):
        r_ref[hh] = jax.nn.silu(_dot(h, wr_ref[:, hh * GLA_DV:(hh + 1) * GLA_DV])).astype(BF16)

    rope_pair(wsk_ref, sk_ref, kg_ref, 1.0, 0)
    for pair in range(SWA_Q_W // MXU_COLS):
        rope_pair(wsq_ref, sq_ref, qg_ref, hd ** -0.5 * LOG2_E, pair * MXU_COLS)

    lr = _dot(h, wlr_ref[...]).astype(BF16)
    ga_ref[...] = jax.nn.sigmoid(_dot(h, wga_ref[...])).astype(BF16)
    z = _dot(lr, up_ref[...]) + gbias_ref[...]
    log2_g = jnp.maximum(
        (jnp.minimum(z, 0.0) * LOG2_E - jnp.log2(1.0 + jnp.exp2(jnp.abs(z) * -LOG2_E)))
        * (1.0 / GLA_GATE_NORMALIZER), GLA_LOG_GATE_MIN * LOG2_E)
    hi = log2_g.astype(BF16)
    lo = (log2_g - hi.astype(F32)).astype(BF16)

    r_head(0)
    r_head(1)
    gb_ref[...] = jax.nn.sigmoid(_dot(h, wgb_ref[...])).astype(BF16)
    q = _dot(h, wq_ref[...]) * (GLA_DK ** -0.5)
    k = _dot(h, wk_ref[...])
    r_head(2)
    r_head(3)

    ri = lax.broadcasted_iota(jnp.int32, (c, 2 * c), 0)
    ci = lax.broadcasted_iota(jnp.int32, (c, 2 * c), 1) % c
    tri = (jnp.where(ri >= ci, 1.0, 0.0).astype(BF16), jnp.where(ci >= ri, 1.0, 0.0).astype(BF16))
    total_row = (c - 1, 0)
    for t in range(tm // c):
        rows = slice(t * c, (t + 1) * c)
        totals = []
        for d in range(2):
            cols = slice(d * GLA_QK_W, (d + 1) * GLA_QK_W)
            b = _dot(tri[d], jnp.concatenate([hi[rows, cols], lo[rows, cols]], axis=0))
            decay = jnp.exp2(b)
            total = decay[total_row[d]:total_row[d] + 1, :]
            totals.append(total)
            q_dec = (q[rows, :] * decay).astype(BF16)
            k_inv = (k[rows, :] * jnp.exp2(-b)).astype(BF16)
            for hh in range(GLA_HEADS):
                src = slice(hh * GLA_DK, (hh + 1) * GLA_DK)
                dst = slice(d * GLA_DK, (d + 1) * GLA_DK)
                qd_ref[hh, rows, dst] = q_dec[:, src]
                ki_ref[hh, rows, dst] = k_inv[:, src]
        tot_ref[t] = jnp.concatenate(
            [totals[d][:, hh * GLA_DK:(hh + 1) * GLA_DK]
             for hh in range(GLA_HEADS) for d in range(2)], axis=1)
        if t < GLA_HEADS:
            v_head(t)

    for hh in range(tm // c, GLA_HEADS):
        v_head(hh)
    sv_ref[...] = _dot(h, wsv_ref[...]).astype(BF16)


def _in_projection(x2, norm_g, cos, sin, q_g, k_g, weights, seq):
    m = x2.shape[0]
    tm = IN_TILE_M
    pos_blocks = seq // tm
    row = lambda w: pl.BlockSpec((tm, w), lambda i: (i, 0))
    pos = pl.BlockSpec((tm, SWA_HEAD_DIM), lambda i: (i % pos_blocks, 0))
    head_major = (pl.BlockSpec((GLA_HEADS, tm, GLA_DV), lambda i: (0, i, 0)),
                  jax.ShapeDtypeStruct((GLA_HEADS, m, GLA_DV), BF16))
    tot = (pl.BlockSpec((tm // GLA_CHUNK, 1, 2 * GLA_QK_W), lambda i: (i, 0, 0)),
           jax.ShapeDtypeStruct((m // GLA_CHUNK, 1, 2 * GLA_QK_W), F32))
    flat = lambda w: (row(w), jax.ShapeDtypeStruct((m, w), BF16))
    outs = (head_major, head_major, tot, head_major, head_major,
            flat(SWA_Q_W), flat(SWA_KV_W), flat(SWA_KV_W), flat(D_MODEL), flat(D_MODEL))
    return pl.pallas_call(
        _inproj_kernel,
        grid=(m // tm,),
        in_specs=[row(D_MODEL), _const_spec((1, D_MODEL)), pos, pos,
                  _const_spec((1, SWA_HEAD_DIM)), _const_spec((1, SWA_HEAD_DIM)),
                  pl.BlockSpec(memory_space=pl.ANY)]
                 + [_const_spec(w.shape) for w in weights[1:]],
        out_specs=[spec for spec, _ in outs],
        out_shape=[shape for _, shape in outs],
        scratch_shapes=[pltpu.VMEM((D_MODEL, W_IN_CODES + LANES), BF16),
                        pltpu.VMEM((D_MODEL, W_IN_OFFSETS[-1] - W_IN_TAIL), BF16),
                        pltpu.VMEM((STAGE_SLOTS, LANES, D_MODEL), F32),
                        pltpu.SemaphoreType.DMA((STAGE_SLOTS,))],
        compiler_params=pltpu.CompilerParams(
            dimension_semantics=("arbitrary",), vmem_limit_bytes=VMEM_LIMIT),
        name="in_projection",
    )(x2, norm_g, cos, sin, q_g, k_g, *weights)


def _gla_kernel(qd_ref, ki_ref, tot_ref, v_ref, r_ref, og_ref, o_ref, sf_ref, kvb_ref):
    c = GLA_CHUNK
    dk = GLA_DK
    n_chunks = v_ref.shape[0] // c
    lower_incl = (lax.broadcasted_iota(jnp.int32, (c, c), 0)
                  >= lax.broadcasted_iota(jnp.int32, (c, c), 1))
    rows = [slice(n * c, (n + 1) * c) for n in range(n_chunks)]

    s = jnp.zeros((GLA_DV, dk), F32)
    for n in range(n_chunks):
        kv_t = _dot_tn(v_ref[rows[n], :], ki_ref[rows[n], :])
        sf_ref[n] = s.astype(BF16).T
        kvb_ref[n] = kv_t[:, dk:]
        s = (s + kv_t[:, :dk]) * tot_ref[n][:, :dk]

    order = range(n_chunks - 1, -1, -1)
    qd = {n: qd_ref[rows[n], :] for n in order}
    a = {n: jnp.where(lower_incl, _dot_nt(qd[n][:, :dk], ki_ref[rows[n], :dk]),
                      _dot_nt(qd[n][:, dk:], ki_ref[rows[n], dk:])).astype(BF16) for n in order}
    s = jnp.zeros((GLA_DV, dk), F32)
    for n in order:
        both = jnp.concatenate([sf_ref[n], s.astype(BF16).T], axis=0)
        out = _dot(a[n], v_ref[rows[n], :]) + _dot(qd[n], both)
        s = (s + kvb_ref[n]) * tot_ref[n][:, dk:]
        o_ref[rows[n], :] = (_rms(out, og_ref[...])
                             * r_ref[rows[n], :].astype(F32)).astype(BF16)


def _gla_branch(qd, ki, tot, v, r, out_g, batch, seq):
    n_chunks = seq // GLA_CHUNK
    assert 2 * GLA_DK == GLA_DV
    vv = pl.BlockSpec((None, seq, GLA_DV), lambda b, h: (h, b, 0))
    return pl.pallas_call(
        _gla_kernel,
        grid=(batch, GLA_HEADS),
        in_specs=[vv, vv, pl.BlockSpec((n_chunks, 1, 2 * GLA_DK), lambda b, h: (b, 0, h)),
                  vv, vv, _const_spec((1, GLA_DV))],
        out_specs=vv,
        out_shape=jax.ShapeDtypeStruct((GLA_HEADS, batch * seq, GLA_DV), BF16),
        scratch_shapes=[pltpu.VMEM((n_chunks, GLA_DK, GLA_DV), BF16),
                        pltpu.VMEM((n_chunks, GLA_DV, GLA_DK), F32)],
        compiler_params=pltpu.CompilerParams(
            dimension_semantics=("parallel", "parallel"), vmem_limit_bytes=VMEM_LIMIT),
        name="gla_branch",
    )(qd, ki, tot, v, r, out_g)


def _swa_kernel(sink_ref, q_ref, k_ref, v_ref, o_ref):
    blk = SWA_BLOCK
    hd = SWA_HEAD_DIM
    tile_blocks = q_ref.shape[0] // blk
    seq_blocks = k_ref.shape[0] // blk
    tile = pl.program_id(1)
    rows_g = SWA_GROUP * blk
    qi = lax.broadcasted_iota(jnp.int32, (rows_g, blk), 0) % blk
    kj = lax.broadcasted_iota(jnp.int32, (rows_g, blk), 1)
    ones = jnp.ones((3 * blk, hd), BF16)
    head_cols = lambda hh: slice(hh * hd, (hh + 1) * hd)

    def logits(t0):
        jobs = [(t0 + t, hk) for t in range(SWA_UNROLL) for hk in range(SWA_KV_HEADS)]
        out = []
        for t, hk in jobs:
            n = tile * tile_blocks + t
            q4 = jnp.concatenate([q_ref[t * blk:(t + 1) * blk, head_cols(hk * SWA_GROUP + g)]
                                  for g in range(SWA_GROUP)], axis=0)
            rows = [pl.ds(pl.multiple_of(nn * blk, blk), blk) for nn in
                    (jnp.maximum(n - 1, 0), n, jnp.minimum(n + 1, seq_blocks - 1))]
            masks = [kj >= qi + jnp.where(n > 0, 0, blk), None,
                     kj <= qi - jnp.where(n < seq_blocks - 1, 0, blk)]
            s = [_dot_nt(q4, k_ref[r, head_cols(hk)]) for r in rows]
            s = [sj if mask is None else jnp.where(mask, sj, MASK_VALUE)
                 for sj, mask in zip(s, masks)]
            out.append((t, hk, rows, s))
        return out

    def attend(group):
        for t, hk, rows, s in group:
            row_max = jnp.broadcast_to(
                jnp.maximum(jnp.maximum(s[0], s[1]), s[2]).max(axis=-1, keepdims=True),
                (rows_g, blk))
            sinks = [sink_ref[hk * SWA_GROUP + g] * LOG2_E for g in range(SWA_GROUP)]
            mg = [jnp.maximum(row_max[g * blk:(g + 1) * blk], sk) for g, sk in enumerate(sinks)]
            m = jnp.concatenate(mg, axis=0)
            sink_w = jnp.concatenate([jnp.exp2(sk - x) for sk, x in zip(sinks, mg)], axis=0)
            p = jnp.concatenate([jnp.exp2(sj - m).astype(BF16) for sj in s], axis=1)
            acc = _dot(p, jnp.concatenate(
                [jnp.concatenate([v_ref[r, head_cols(hk)] for r in rows], axis=0), ones], axis=1))
            o = acc[:, :hd] / (acc[:, hd:] + sink_w)
            for g in range(SWA_GROUP):
                o_ref[t * blk:(t + 1) * blk, head_cols(hk * SWA_GROUP + g)] = (
                    o[g * blk:(g + 1) * blk, :].astype(BF16))

    starts = list(range(0, tile_blocks, SWA_UNROLL))
    pending = logits(starts[0])
    for t0 in starts[1:]:
        upcoming = logits(t0)
        attend(pending)
        pending = upcoming
    attend(pending)


def _swa_branch(sq, sk, sv, sinks, batch, seq):
    tq = SWA_TILE_Q
    tiles = seq // tq
    return pl.pallas_call(
        _swa_kernel,
        grid=(batch, tiles),
        in_specs=[pl.BlockSpec(memory_space=pltpu.SMEM),
                  pl.BlockSpec((tq, SWA_Q_W), lambda b, t: (b * tiles + t, 0)),
                  pl.BlockSpec((seq, SWA_KV_W), lambda b, t: (b, 0)),
                  pl.BlockSpec((seq, SWA_KV_W), lambda b, t: (b, 0))],
        out_specs=pl.BlockSpec((tq, SWA_Q_W), lambda b, t: (b * tiles + t, 0)),
        out_shape=jax.ShapeDtypeStruct((batch * seq, SWA_Q_W), BF16),
        compiler_params=pltpu.CompilerParams(
            dimension_semantics=("parallel", "parallel"), vmem_limit_bytes=VMEM_LIMIT),
        name="swa_branch",
    )(sinks, sq, sk, sv)


def _out_ffn_kernel(x_ref, og_ref, os_ref, ga_ref, gb_ref, g2_ref,
                    wog_hbm, wos_hbm, wout_hbm, wfi_hbm, wfo_hbm, o_ref,
                    wog_ref, wos_ref, wout_ref, wfi_ref, wfo_ref, stage_ref, sem_ref):
    @pl.when(pl.program_id(0) == 0)
    def _():
        tr, tc = stage_ref.shape[1:]
        tiles = []
        for src, dst in ((wog_hbm, wog_ref), (wos_hbm, wos_ref), (wout_hbm, wout_ref),
                         (wfi_hbm, wfi_ref), (wfo_hbm, wfo_ref)):
            for r0 in range(0, src.shape[0], tr):
                for c0 in range(0, src.shape[1], tc):
                    def cast(tile_ref, dst=dst, r0=r0, c0=c0):
                        dst[r0:r0 + tr, c0:c0 + tc] = tile_ref[...].astype(BF16)
                    tiles.append((src.at[r0:r0 + tr, c0:c0 + tc], cast))
        _stream_tiles(tiles, stage_ref, sem_ref)

    o_gla = jnp.concatenate([og_ref[hh] for hh in range(GLA_HEADS)], axis=1)
    y_gla = _dot(o_gla, wog_ref[...])
    y_swa = _dot(os_ref[...], wos_ref[...])
    merged = ga_ref[...].astype(F32) * y_gla + gb_ref[...].astype(F32) * y_swa
    x1 = x_ref[...] + _dot(merged.astype(BF16), wout_ref[...])
    h2 = _rms(x1, g2_ref[...]).astype(BF16)
    acc = x1
    for c0 in range(0, D_FF, FFN_CHUNK):
        cols = slice(c0, c0 + FFN_CHUNK)
        gate = _dot(h2, wfi_ref[:, cols])
        up = _dot(h2, wfi_ref[:, D_FF + c0:D_FF + c0 + FFN_CHUNK])
        act = (jax.nn.silu(gate) * up).astype(BF16)
        acc = acc + _dot(act, wfo_ref[cols, :])
    o_ref[...] = acc


def _out_ffn(x2, o_gla, o_swa, ga, gb, norm_g, *weights):
    m = x2.shape[0]
    tm = OUT_TILE_M
    row = pl.BlockSpec((tm, D_MODEL), lambda i: (i, 0))
    return pl.pallas_call(
        _out_ffn_kernel,
        grid=(m // tm,),
        in_specs=[row, pl.BlockSpec((GLA_HEADS, tm, GLA_DV), lambda i: (0, i, 0))] + [row] * 3
                 + [_const_spec(norm_g.shape)]
                 + [pl.BlockSpec(memory_space=pl.ANY)] * len(weights),
        out_specs=row,
        out_shape=jax.ShapeDtypeStruct((m, D_MODEL), F32),
        scratch_shapes=[pltpu.VMEM(w.shape, BF16) for w in weights]
                       + [pltpu.VMEM((STAGE_SLOTS,) + OUT_STAGE_TILE, F32),
                          pltpu.SemaphoreType.DMA((STAGE_SLOTS,))],
        compiler_params=pltpu.CompilerParams(
            dimension_semantics=("arbitrary",), vmem_limit_bytes=VMEM_LIMIT),
        name="out_ffn",
    )(x2, o_gla, o_swa, ga, gb, norm_g, *weights)


def _rope_tables(seq):
    half = SWA_HEAD_DIM // 2
    inv_freq = ROPE_THETA ** (-np.arange(half, dtype=np.float64) / half)
    ang = np.arange(seq, dtype=np.float64)[:, None] * inv_freq[None, :]
    cos, sin = np.cos(ang), np.sin(ang)
    return (jnp.asarray(np.concatenate([cos, cos], axis=-1), F32),
            jnp.asarray(np.concatenate([-sin, sin], axis=-1), F32))


def _gate_up(up_f, up_b):
    zeros = jnp.zeros_like(up_f)
    top = jnp.concatenate([up_f, zeros], axis=1)
    bot = jnp.concatenate([zeros, up_b], axis=1)
    pad = jnp.zeros((LANES - 2 * GLA_GATE_RANK, 2 * GLA_QK_W), up_f.dtype)
    return jnp.concatenate([top, bot, pad], axis=0).astype(BF16)


def _layer(x2, batch, seq, norm_mix_g, w_in, up_f, bias_f, up_b, bias_b, gla_out_g, w_o_gla,
           q_g, k_g, sinks, w_o_swa, w_out, norm_ffn_g, w_ffn_in, w_ffn_out):
    cos, sin = _rope_tables(seq)
    row = lambda t: t.reshape(1, -1)
    qd, ki, tot, v, r, sq, sk, sv, ga, gb = _in_projection(
        x2, row(norm_mix_g), cos, sin, row(q_g), row(k_g),
        (w_in.T, _gate_up(up_f, up_b), row(jnp.concatenate([bias_f, bias_b]))), seq)
    o_gla = _gla_branch(qd, ki, tot, v, r, row(gla_out_g), batch, seq)
    o_swa = _swa_branch(sq, sk, sv, sinks, batch, seq)
    return _out_ffn(x2, o_gla, o_swa, ga, gb, row(norm_ffn_g),
                    w_o_gla, w_o_swa, w_out, w_ffn_in, w_ffn_out)


def kernel(x, norm_mix_g, w_in, gla_gate_up_fwd, gla_gate_bias_fwd, gla_gate_up_bwd,
           gla_gate_bias_bwd, gla_out_norm_g, w_o_gla, swa_q_norm_g, swa_k_norm_g,
           swa_sinks, w_o_swa, w_out, norm_ffn_g, w_ffn_in, w_ffn_out):
    batch, seq, d = x.shape
    x2 = x.reshape(batch * seq, d)
    for l in range(w_in.shape[0]):
        x2 = _layer(x2, batch, seq, norm_mix_g[l], w_in[l], gla_gate_up_fwd[l],
                    gla_gate_bias_fwd[l], gla_gate_up_bwd[l], gla_gate_bias_bwd[l],
                    gla_out_norm_g[l], w_o_gla[l], swa_q_norm_g[l], swa_k_norm_g[l],
                    swa_sinks[l], w_o_swa[l], w_out[l], norm_ffn_g[l], w_ffn_in[l],
                    w_ffn_out[l])
    return x2.reshape(batch, seq, d)
```
